```python
import math
import jax, jax.numpy as jnp
from jax import lax
import numpy as np

D_MODEL = 1024
BATCH = 16
SEQ = 256
DEPTH = 2
DEC_BATCH = 2
DEC_SEQ = 2048
PAST_LEN = 256

GRID_W = 64
BLOCK = 128
POOL_CH = 256
POOL_GROUPS = 4
POOL_WINDOWS = (2, 4, 8, 16)
DIFF_HEADS = 4
DIFF_QK = 32
DIFF_V = 64
WIN_Q_HEADS = 4
WIN_KV_HEADS = 2
WIN_HD = 64
WINDOW = 128
CONV_CH = 256
CONV_K = 31
MIX_WIDTH = POOL_CH + DIFF_HEADS * DIFF_V + WIN_Q_HEADS * WIN_HD + CONV_CH
IN_WIDTHS = (POOL_CH, DIFF_HEADS * 2 * DIFF_QK, DIFF_HEADS * 2 * DIFF_QK, DIFF_HEADS * DIFF_V,
             WIN_Q_HEADS * WIN_HD, WIN_KV_HEADS * WIN_HD, WIN_KV_HEADS * WIN_HD, 2 * CONV_CH)
IN_WIDTH = sum(IN_WIDTHS)
IN_SPLITS = tuple(int(v) for v in np.cumsum(IN_WIDTHS)[:-1])
D_FF = 2816
FFN_CONV_K = 3
ROPE_BASE = 10000.0
EPS = 1e-6

kernel_name = "hybrid_diffusion_prefix_trunk_step"


def rms_norm(x, g):
    xf = x.astype(jnp.float32)
    y = xf * lax.rsqrt(jnp.mean(xf * xf, axis=-1, keepdims=True) + EPS)
    return (y * g.astype(jnp.float32)).astype(x.dtype)


def layer_norm(x, g, b):
    xf = x.astype(jnp.float32)
    mu = jnp.mean(xf, axis=-1, keepdims=True)
    var = jnp.mean(jnp.square(xf - mu), axis=-1, keepdims=True)
    y = (xf - mu) * lax.rsqrt(var + EPS)
    return (y * g.astype(jnp.float32) + b.astype(jnp.float32)).astype(x.dtype)


def axial_rope_tables(L, d, dtype):
    rows = L // GRID_W
    r = np.repeat(np.arange(rows), GRID_W).astype(np.float32)
    col = np.tile(np.arange(GRID_W), rows).astype(np.float32)
    n = d // 4
    inv = (ROPE_BASE ** (-np.arange(n) / n)).astype(np.float32)
    ang = np.concatenate([r[:, None] * inv[None], col[:, None] * inv[None]], axis=-1)
    return jnp.asarray(np.cos(ang), dtype), jnp.asarray(np.sin(ang), dtype)


def apply_rope(x, cos, sin):
    h = x.shape[-1] // 2
    x1, x2 = x[..., :h], x[..., h:]
    c, s = cos[None, :, None, :], sin[None, :, None, :]
    return jnp.concatenate([x1 * c - x2 * s, x1 * s + x2 * c], axis=-1)


def to_blocks(t):
    B, L = t.shape[:2]
    return t.reshape((B, L // BLOCK, BLOCK) + t.shape[2:]).swapaxes(0, 1)


def from_blocks(o):
    nb, B, bl = o.shape[:3]
    return o.swapaxes(0, 1).reshape((B, nb * bl) + o.shape[3:])


def depthwise_conv(x, w, b):
    K = w.shape[0]
    y = lax.conv_general_dilated(x, w[:, None, :].astype(x.dtype), window_strides=(1,),
                                 padding=[(K // 2, K // 2)],
                                 dimension_numbers=('NWC', 'WIO', 'NWC'),
                                 feature_group_count=x.shape[-1])
    return y + b


def multi_pool(u, w, scale):
    B, L, C = u.shape
    cg = C // POOL_GROUPS
    t = np.arange(L)[:, None]
    win = np.array(POOL_WINDOWS)[None, :]
    lo = np.clip(t - win // 2, 0, L)
    hi = np.clip(t - win // 2 + win, 0, L)
    cnt = (hi - lo).astype(np.float32)
    ug = u.reshape(B, L, POOL_GROUPS, cg).astype(jnp.float32)
    cs = jnp.pad(jnp.cumsum(ug, axis=1), ((0, 0), (1, 0), (0, 0), (0, 0)))
    gi = np.arange(POOL_GROUPS)[None, :]
    mean = (cs[:, hi, gi] - cs[:, lo, gi]) / cnt[None, :, :, None]
    pooled = (mean - ug).astype(u.dtype)
    y = jnp.einsum('blgc,gcd->blgd', pooled, w).reshape(B, L, C)
    return y * scale


def diff_attention(q1, q2, k1, k2, v, lam):
    scale = q1.shape[-1] ** -0.5

    def block(qs):
        a, b = qs
        s1 = jnp.einsum('bqhd,bkhd->bhqk', a, k1).astype(jnp.float32) * scale
        s2 = jnp.einsum('bqhd,bkhd->bhqk', b, k2).astype(jnp.float32) * scale
        p = jax.nn.softmax(s1, axis=-1) - lam * jax.nn.softmax(s2, axis=-1)
        return jnp.einsum('bhqk,bkhd->bqhd', p.astype(v.dtype), v)

    return from_blocks(lax.map(block, (to_blocks(q1), to_blocks(q2))))


def gqa_sink_dense(q, k, v, sink):
    B, L, Hq, dh = q.shape
    Hkv = k.shape[2]
    G = Hq // Hkv
    scale = dh ** -0.5
    sk = sink.astype(jnp.float32).reshape(1, Hkv, G, 1, 1)

    def block(qb):
        s = jnp.einsum('bqhgd,bkhd->bhgqk', qb, k).astype(jnp.float32) * scale
        logits = jnp.concatenate([s, jnp.broadcast_to(sk, s.shape[:-1] + (1,))], axis=-1)
        p = jax.nn.softmax(logits, axis=-1)[..., :-1]
        return jnp.einsum('bhgqk,bkhd->bqhgd', p.astype(v.dtype), v)

    o = from_blocks(lax.map(block, to_blocks(q.reshape(B, L, Hkv, G, dh))))
    return o.reshape(B, L, Hq * dh)


def gqa_window_sink(q, k, v, kc, vc, sink):
    B, L, Hq, dh = q.shape
    Hkv = k.shape[2]
    G = Hq // Hkv
    nb = L // BLOCK
    Lc = kc.shape[1]
    W = 3 * BLOCK
    scale = dh ** -0.5
    qb = q.reshape(B, nb, BLOCK, Hkv, G, dh)

    def band(t):
        tp = jnp.pad(t, ((0, 0), (BLOCK, BLOCK), (0, 0), (0, 0))).reshape(B, nb + 2, BLOCK, Hkv, dh)
        return jnp.concatenate([tp[:, :-2], tp[:, 1:-1], tp[:, 2:]], axis=2)

    kb, vb = band(k), band(v)
    qpos = np.arange(nb)[:, None, None] * BLOCK + np.arange(BLOCK)[None, :, None]
    kpos = np.arange(nb)[:, None, None] * BLOCK - BLOCK + np.arange(W)[None, None, :]
    mask = (np.abs(qpos - kpos) <= WINDOW) & (kpos >= 0) & (kpos < L)
    s_loc = jnp.einsum('bnqhgd,bnkhd->bnhgqk', qb, kb).astype(jnp.float32) * scale
    s_loc = jnp.where(mask[None, :, None, None], s_loc, -1e30)
    s_ctx = jnp.einsum('bnqhgd,bkhd->bnhgqk', qb, kc).astype(jnp.float32) * scale
    sk = jnp.broadcast_to(sink.astype(jnp.float32).reshape(1, 1, Hkv, G, 1, 1), s_loc.shape[:-1] + (1,))
    p = jax.nn.softmax(jnp.concatenate([s_loc, s_ctx, sk], axis=-1), axis=-1)
    o = (jnp.einsum('bnhgqk,bnkhd->bnqhgd', p[..., :W].astype(v.dtype), vb)
         + jnp.einsum('bnhgqk,bkhd->bnqhgd', p[..., W:W + Lc].astype(v.dtype), vc))
    return o.reshape(B, L, Hq * dh)


def mixers(h, l, P, rope, ctx):
    B, L, _ = h.shape
    proj = h @ P['w_in'][l]
    pool_in, dq, dk, dv, wq, wk, wv, conv_in = jnp.split(proj, IN_SPLITS, axis=-1)
    dq = dq.reshape(B, L, DIFF_HEADS, 2 * DIFF_QK)
    dk = dk.reshape(B, L, DIFF_HEADS, 2 * DIFF_QK)
    dv = dv.reshape(B, L, DIFF_HEADS, DIFF_V)
    wq = wq.reshape(B, L, WIN_Q_HEADS, WIN_HD)
    wk = wk.reshape(B, L, WIN_KV_HEADS, WIN_HD)
    wv = wv.reshape(B, L, WIN_KV_HEADS, WIN_HD)

    y_pool = multi_pool(pool_in, P['pool_w'][l], P['pool_scale'][l])

    lam_init = 0.8 - 0.6 * math.exp(-0.3 * l)
    f32 = jnp.float32
    lam = (jnp.exp(jnp.sum(P['diff_lq1'][l].astype(f32) * P['diff_lk1'][l].astype(f32)))
           - jnp.exp(jnp.sum(P['diff_lq2'][l].astype(f32) * P['diff_lk2'][l].astype(f32))) + lam_init)
    q1, q2 = dq[..., :DIFF_QK], dq[..., DIFF_QK:]
    k1, k2 = dk[..., :DIFF_QK], dk[..., DIFF_QK:]
    sink = P['win_sink'][l]
    if ctx is None:
        y_diff = diff_attention(q1, q2, k1, k2, dv, lam)
        y_win = gqa_sink_dense(wq, wk, wv, sink)
    else:
        kd_c, vd_c, kw_c, vw_c = ctx
        cos_d, sin_d, cos_w, sin_w = rope
        k1a = jnp.concatenate([kd_c[..., :DIFF_QK], apply_rope(k1, cos_d, sin_d)], axis=1)
        k2a = jnp.concatenate([kd_c[..., DIFF_QK:], apply_rope(k2, cos_d, sin_d)], axis=1)
        va = jnp.concatenate([vd_c, dv], axis=1)
        y_diff = diff_attention(apply_rope(q1, cos_d, sin_d), apply_rope(q2, cos_d, sin_d), k1a, k2a, va, lam)
        y_win = gqa_window_sink(apply_rope(wq, cos_w, sin_w), apply_rope(wk, cos_w, sin_w), wv, kw_c, vw_c, sink)
    y_diff = (rms_norm(y_diff, P['diff_subln'][l]) * (1.0 - lam_init)).reshape(B, L, DIFF_HEADS * DIFF_V)

    a, g = jnp.split(conv_in, 2, axis=-1)
    u = a * jax.nn.sigmoid(g)
    u = depthwise_conv(u, P['conv_dw'][l], P['conv_dw_b'][l])
    u = jax.nn.silu(layer_norm(u, P['conv_ln_g'][l], P['conv_ln_b'][l]))
    y_conv = u @ P['conv_pw'][l]

    out = jnp.concatenate([y_pool, y_diff, y_win, y_conv], axis=-1) @ P['w_out'][l]
    return out, (dk, dv, wk, wv)


def conv_ffn(h, l, P):
    u = depthwise_conv(h @ P['ffn_up'][l], P['ffn_dw'][l], P['ffn_dw_b'][l])
    gate, up = jnp.split(u, 2, axis=-1)
    return (jax.nn.silu(gate) * up) @ P['ffn_down'][l]


def layer(x, cond, l, P, rope, ctx):
    mod = (jax.nn.silu(cond) @ P['w_ada'][l] + P['b_ada'][l])[:, None, :]
    sh1, sc1, g1, sh2, sc2, g2 = jnp.split(mod, 6, axis=-1)
    h = rms_norm(x, P['g_pre_mix'][l]) * (1 + sc1) + sh1
    o, kv = mixers(h, l, P, rope, ctx)
    x = x + g1 * rms_norm(o, P['g_post_mix'][l])
    h = rms_norm(x, P['g_pre_ffn'][l]) * (1 + sc2) + sh2
    x = x + g2 * rms_norm(conv_ffn(h, l, P), P['g_post_ffn'][l])
    return x, kv


def setup_inputs(seed: int = 0) -> dict:
    key = jax.random.key(seed)
    ks = iter(jax.random.split(key, 40))
    f32 = jnp.float32

    def nrm(shape, scale=1.0):
        return jax.random.normal(next(ks), shape, f32) * scale

    def gain(shape):
        return 1.0 + nrm(shape, 0.05)

    D = D_MODEL
    return {
        "x_prompt": nrm((BATCH, SEQ, D)),
        "x_sample": nrm((DEC_BATCH, DEC_SEQ, D)),
        "c": nrm((DEC_BATCH, D)),
        "cache_diff_k": nrm((DEC_BATCH, DEPTH, PAST_LEN, DIFF_HEADS, 2 * DIFF_QK)),
        "cache_diff_v": nrm((DEC_BATCH, DEPTH, PAST_LEN, DIFF_HEADS, DIFF_V)),
        "cache_win_k": nrm((DEC_BATCH, DEPTH, PAST_LEN, WIN_KV_HEADS, WIN_HD)),
        "cache_win_v": nrm((DEC_BATCH, DEPTH, PAST_LEN, WIN_KV_HEADS, WIN_HD)),
        "c_ctx": nrm((D,)),
        "w_ada": nrm((DEPTH, D, 6 * D), D ** -0.5),
        "b_ada": nrm((DEPTH, 6 * D), 0.02),
        "g_pre_mix": gain((DEPTH, D)),
        "g_post_mix": gain((DEPTH, D)),
        "g_pre_ffn": gain((DEPTH, D)),
        "g_post_ffn": gain((DEPTH, D)),
        "w_in": nrm((DEPTH, D, IN_WIDTH), D ** -0.5),
        "w_out": nrm((DEPTH, MIX_WIDTH, D), MIX_WIDTH ** -0.5),
        "pool_w": nrm((DEPTH, POOL_GROUPS, POOL_CH // POOL_GROUPS, POOL_CH // POOL_GROUPS), (POOL_CH // POOL_GROUPS) ** -0.5),
        "pool_scale": gain((DEPTH, POOL_CH)),
        "diff_lq1": nrm((DEPTH, DIFF_QK), 0.1),
        "diff_lk1": nrm((DEPTH, DIFF_QK), 0.1),
        "diff_lq2": nrm((DEPTH, DIFF_QK), 0.1),
        "diff_lk2": nrm((DEPTH, DIFF_QK), 0.1),
        "diff_subln": gain((DEPTH, DIFF_V)),
        "win_sink": nrm((DEPTH, WIN_Q_HEADS), 0.5),
        "conv_dw": nrm((DEPTH, CONV_K, CONV_CH), CONV_K ** -0.5),
        "conv_dw_b": nrm((DEPTH, CONV_CH), 0.02),
        "conv_ln_g": gain((DEPTH, CONV_CH)),
        "conv_ln_b": nrm((DEPTH, CONV_CH), 0.02),
        "conv_pw": nrm((DEPTH, CONV_CH, CONV_CH), CONV_CH ** -0.5),
        "ffn_up": nrm((DEPTH, D, 2 * D_FF), D ** -0.5),
        "ffn_dw": nrm((DEPTH, FFN_CONV_K, 2 * D_FF), FFN_CONV_K ** -0.5),
        "ffn_dw_b": nrm((DEPTH, 2 * D_FF), 0.02),
        "ffn_down": nrm((DEPTH, D_FF, D), D_FF ** -0.5),
    }


def reference(x_prompt, x_sample, c, cache_diff_k, cache_diff_v, cache_win_k, cache_win_v, c_ctx,
              w_ada, b_ada, g_pre_mix, g_post_mix, g_pre_ffn, g_post_ffn, w_in, w_out,
              pool_w, pool_scale, diff_lq1, diff_lk1, diff_lq2, diff_lk2, diff_subln, win_sink,
              conv_dw, conv_dw_b, conv_ln_g, conv_ln_b, conv_pw, ffn_up, ffn_dw, ffn_dw_b, ffn_down):
    P = dict(w_ada=w_ada, b_ada=b_ada, g_pre_mix=g_pre_mix, g_post_mix=g_post_mix,
             g_pre_ffn=g_pre_ffn, g_post_ffn=g_post_ffn, w_in=w_in, w_out=w_out,
             pool_w=pool_w, pool_scale=pool_scale, diff_lq1=diff_lq1, diff_lk1=diff_lk1,
             diff_lq2=diff_lq2, diff_lk2=diff_lk2, diff_subln=diff_subln, win_sink=win_sink,
             conv_dw=conv_dw, conv_dw_b=conv_dw_b, conv_ln_g=conv_ln_g, conv_ln_b=conv_ln_b,
             conv_pw=conv_pw, ffn_up=ffn_up, ffn_dw=ffn_dw, ffn_dw_b=ffn_dw_b, ffn_down=ffn_down)

    xp = x_prompt
    kd_l, vd_l, kw_l, vw_l = [], [], [], []
    for l in range(DEPTH):
        xp, (kd, vd, kw, vw) = layer(xp, c_ctx[None, :], l, P, None, None)
        kd_l.append(kd)
        vd_l.append(vd)
        kw_l.append(kw)
        vw_l.append(vw)
    new_diff_k = jnp.stack(kd_l, axis=1)
    new_diff_v = jnp.stack(vd_l, axis=1)
    new_win_k = jnp.stack(kw_l, axis=1)
    new_win_v = jnp.stack(vw_l, axis=1)

    L = x_sample.shape[1]
    cos_d, sin_d = axial_rope_tables(L, DIFF_QK, x_sample.dtype)
    cos_w, sin_w = axial_rope_tables(L, WIN_HD, x_sample.dtype)
    rope = (cos_d, sin_d, cos_w, sin_w)
    xs = x_sample
    for l in range(DEPTH):
        ctx = (cache_diff_k[:, l], cache_diff_v[:, l], cache_win_k[:, l], cache_win_v[:, l])
        xs, _ = layer(xs, c, l, P, rope, ctx)

    return (xp, xs, new_diff_k, new_diff_v, new_win_k, new_win_v)
```

```python
import functools
import math

import jax
import jax.numpy as jnp
import numpy as np
from jax import lax
from jax.experimental import pallas as pl
from jax.experimental.pallas import tpu as pltpu

F32 = jnp.float32
BF16 = jnp.bfloat16

D_MODEL = 1024
BATCH = 16
SEQ = 256
DEPTH = 2
DEC_BATCH = 2
DEC_SEQ = 2048
PAST_LEN = 256
GRID_W = 64
POOL_WINDOWS = (2, 4, 8, 16)
DIFF_QK = 32
WIN_HD = 64
WINDOW = 128
CONV_K = 31
IN_WIDTH = 2048
D_FF = 2816
ROPE_BASE = 10000.0
EPS = 1e-6
LOG2E = 1.4426950408889634

LANES = 128
HALO = 128
CONV_PAD = 16
FFN_HALO = 16
FFN_CHUNK = 256
KV_WIDTH = 768
VMEM_LIMIT = 56 * 1024 * 1024

TM_IN = 512
TQ_DEC = 256
TM_FFN = 512


def _rms(x, g):
    return x * lax.rsqrt(jnp.mean(x * x, axis=-1, keepdims=True) + EPS) * g


def _dot(a, b):
    return jnp.dot(a, b, preferred_element_type=F32)


def _dot_nt(a, b):
    return lax.dot_general(a, b, (((1,), (1,)), ((), ())), preferred_element_type=F32)


def _lane_range(shape, lo, hi):
    lane = lax.broadcasted_iota(jnp.int32, shape, 1)
    return (lane >= lo) & (lane < hi)


def _swap_middle_heads(t0, t1):
    lo = _lane_range(t0.shape, 0, 64)
    return (jnp.where(lo, t0, pltpu.roll(t1, 64, 1)),
            jnp.where(lo, pltpu.roll(t0, 64, 1), t1))


def _ada_kernel(c_ref, w_ref, b_ref, o_ref):
    c = c_ref[...]
    s = (c * jax.nn.sigmoid(c)).astype(BF16)
    o_ref[...] = _dot(s, w_ref[...].astype(BF16)) + b_ref[...]


def _ada(cond8, w_ada, b_ada):
    tn = 1024
    return pl.pallas_call(
        _ada_kernel,
        out_shape=jax.ShapeDtypeStruct((DEPTH, 8, 6 * D_MODEL), F32),
        grid=(DEPTH, 6 * D_MODEL // tn),
        in_specs=[
            pl.BlockSpec((8, D_MODEL), lambda l, j: (0, 0)),
            pl.BlockSpec((None, D_MODEL, tn), lambda l, j: (l, 0, j)),
            pl.BlockSpec((None, 1, tn), lambda l, j: (l, 0, j)),
        ],
        out_specs=pl.BlockSpec((None, 8, tn), lambda l, j: (l, 0, j)),
        compiler_params=pltpu.CompilerParams(
            dimension_semantics=("arbitrary", "arbitrary"), vmem_limit_bytes=VMEM_LIMIT),
        name="ada_mod",
    )(cond8, w_ada, b_ada)


def _rope_tile(v, cos, sin_signed, half):
    is_a = (lax.broadcasted_iota(jnp.int32, v.shape, 1) % (2 * half)) < half
    partner = jnp.where(is_a, pltpu.roll(v, LANES - half, 1), pltpu.roll(v, half, 1))
    return v * cos + partner * sin_signed


def _inproj_kernel(*refs, rope):
    if rope:
        (x_ref, mod_ref, g_ref, w_ref, cd_ref, sd_ref, cw_ref, sw_ref, proj_ref, kv_ref) = refs
    else:
        (x_ref, mod_ref, g_ref, w_ref, proj_ref) = refs
    d = D_MODEL
    mod = mod_ref[...]
    h = _rms(x_ref[...], g_ref[...]) * (1.0 + mod[:, d:2 * d]) + mod[:, 0:d]
    proj = _dot(h.astype(BF16), w_ref[...])
    tiles = [proj[:, i * LANES:(i + 1) * LANES] for i in range(IN_WIDTH // LANES)]
    if rope:
        cd, sd, cw, sw = cd_ref[...], sd_ref[...], cw_ref[...], sw_ref[...]
        for i in (2, 3, 4, 5):
            tiles[i] = _rope_tile(tiles[i], cd, sd, DIFF_QK // 2)
        for i in (8, 9, 10):
            tiles[i] = _rope_tile(tiles[i], cw, sw, WIN_HD // 2)
    tiles[8], tiles[9] = _swap_middle_heads(tiles[8], tiles[9])
    for i, t in enumerate(tiles):
        proj_ref[:, i * LANES:(i + 1) * LANES] = t
    if rope:
        for j, i in enumerate((4, 5, 6, 7, 10, 11)):
            kv_ref[:, j * LANES:(j + 1) * LANES] = tiles[i].astype(BF16)


def _inproj(x, mod, g, w_in_b, l, rope_tabs):
    rows = x.shape[0]
    tm = TM_IN
    nb = mod.shape[0]
    tiles_per_b = rows // nb // tm
    rope = rope_tabs is not None
    in_specs = [
        pl.BlockSpec((tm, D_MODEL), lambda i: (i, 0)),
        pl.BlockSpec((None, 1, 6 * D_MODEL), lambda i: (i // tiles_per_b, 0, 0)),
        pl.BlockSpec((None, 1, D_MODEL), lambda i: (l, 0, 0)),
        pl.BlockSpec((None, D_MODEL, IN_WIDTH), lambda i: (l, 0, 0), pipeline_mode=pl.Buffered(1)),
    ]
    args = [x, mod, g, w_in_b]
    out_shape = [jax.ShapeDtypeStruct((rows, IN_WIDTH), F32)]
    out_specs = [pl.BlockSpec((tm, IN_WIDTH), lambda i: (i, 0))]
    if rope:
        for t in rope_tabs:
            in_specs.append(pl.BlockSpec((tm, LANES), lambda i: (i % tiles_per_b, 0)))
            args.append(t)
        out_shape.append(jax.ShapeDtypeStruct((rows, KV_WIDTH), BF16))
        out_specs.append(pl.BlockSpec((tm, KV_WIDTH), lambda i: (i, 0)))
    return pl.pallas_call(
        functools.partial(_inproj_kernel, rope=rope),
        out_shape=out_shape,
        grid=(rows // tm,),
        in_specs=in_specs,
        out_specs=out_specs,
        compiler_params=pltpu.CompilerParams(
            dimension_semantics=("arbitrary",), vmem_limit_bytes=VMEM_LIMIT),
        name="inproj_dec" if rope else "inproj_ctx",
    )(*args)


def _pool(xpad, u, band_ref, invcnt, wbd, scale):
    xhi = xpad.astype(BF16)
    xlo = (xpad - xhi.astype(F32)).astype(BF16)
    sums = []
    for g in range(len(POOL_WINDOWS)):
        b = band_ref[g]
        sums.append(_dot(b, xhi) + _dot(b, xlo))
    shape = u.shape
    s = jnp.where(_lane_range(shape, 0, 64), sums[0],
                  jnp.where(_lane_range(shape, 64, 128), sums[1],
                            jnp.where(_lane_range(shape, 128, 192), sums[2], sums[3])))
    pooled = s * invcnt - u
    return _dot(pooled.astype(BF16), wbd) * scale


def _conv_module(apad, gpad, dw_ref, dwb, lng, lnb, pw, tq):
    u = apad * jax.nn.sigmoid(gpad)
    acc = jnp.broadcast_to(dwb, (tq, u.shape[1]))
    span = tq + 2 * CONV_PAD - 8
    for r in range(8):
        ur = u[r:r + span]
        for a8 in range(4):
            k = 8 * a8 + r - 1
            if 0 <= k < CONV_K:
                acc = acc + ur[8 * a8:8 * a8 + tq] * dw_ref[k:k + 1, :]
    mu = jnp.mean(acc, axis=-1, keepdims=True)
    xc = acc - mu
    var = jnp.mean(xc * xc, axis=-1, keepdims=True)
    y = xc * lax.rsqrt(var + EPS) * lng + lnb
    y = y * jax.nn.sigmoid(y)
    return _dot(y.astype(BF16), pw)


def _diff_attn(dq, ks, vs, lam, gain, tq):
    cs = DIFF_QK ** -0.5 * LOG2E
    y = jnp.zeros((tq, 256), F32)
    for h in range(4):
        es, rs = [], []
        for w in range(2):
            lo = h * 64 + w * DIFF_QK
            qm = jnp.where(_lane_range(dq.shape, lo, lo + DIFF_QK), dq, 0.0).astype(BF16)
            ss = [_dot_nt(qm, k) for k in ks]
            mx = functools.reduce(jnp.maximum, [jnp.max(s, axis=-1, keepdims=True) for s in ss])
            e = [jnp.exp2((s - mx) * cs) for s in ss]
            den = functools.reduce(lambda a, b: a + b, [jnp.sum(t, axis=-1, keepdims=True) for t in e])
            es.append(e)
            rs.append(1.0 / den)
        r1 = rs[0]
        r2 = rs[1] * lam
        o = None
        for i, v in enumerate(vs):
            p = (es[0][i] * r1 - es[1][i] * r2).astype(BF16)
            t = _dot(p, v)
            o = t if o is None else o + t
        o = jnp.where(_lane_range(o.shape, h * 64, h * 64 + 64), o, 0.0)
        ms = jnp.sum(o * o, axis=-1, keepdims=True) * (1.0 / 64.0)
        y = y + o * lax.rsqrt(ms + EPS)
    return y * gain


def _win_attn(wq, ks, masks, vs, sink_ref, sink_base, tq):
    t0, t1 = wq[:, :LANES], wq[:, LANES:]
    c = WIN_HD ** -0.5
    y0 = jnp.zeros((tq, LANES), F32)
    y1 = jnp.zeros((tq, LANES), F32)
    row = lax.broadcasted_iota(jnp.int32, (2 * tq, 1), 0)
    for j in range(2):
        lm = _lane_range(t0.shape, j * 64, j * 64 + 64)
        q2 = jnp.concatenate([jnp.where(lm, t0, 0.0), jnp.where(lm, t1, 0.0)], axis=0).astype(BF16)
        sk = jnp.where(row < tq, sink_ref[sink_base + 2 * j], sink_ref[sink_base + 2 * j + 1])
        ss = []
        for k, m in zip(ks, masks):
            s = _dot_nt(q2, k)
            if m is not None:
                s = jnp.where(m, s, -1e30)
            ss.append(s)
        mx = functools.reduce(jnp.maximum, [jnp.max(s, axis=-1, keepdims=True) for s in ss])
        mx = jnp.maximum(mx * c, sk)
        mxl = mx * LOG2E
        e = [jnp.exp2(s * (c * LOG2E) - mxl) for s in ss]
        den = functools.reduce(lambda a, b: a + b, [jnp.sum(t, axis=-1, keepdims=True) for t in e])
        den = den + jnp.exp2((sk - mx) * LOG2E)
        o = None
        for t, v in zip(e, vs):
            u = _dot(t.astype(BF16), v)
            o = u if o is None else o + u
        o = o * (1.0 / den)
        y0 = y0 + jnp.where(lm, o[:tq], 0.0)
        y1 = y1 + jnp.where(lm, o[tq:], 0.0)
    return _swap_middle_heads(y0, y1)


def _mixer_kernel(*refs, dec, l, tq):
    if dec:
        (proj_ref, pool_p_ref, pool_n_ref, conv_p_ref, conv_n_ref, kv_ref,
         cdk_ref, cdv_ref, cwk_ref, cwv_ref, *rest) = refs
    else:
        proj_ref, *rest = refs
    (x_ref, mod_ref, band_ref, invcnt_ref, wbd_ref, pscale_ref, lamp_ref, subln_ref, sink_ref,
     dw_ref, dwb_ref, lng_ref, lnb_ref, pw_ref, wout_ref, gpost_ref, gpre_ref,
     x1_ref, h2_ref) = rest
    d = D_MODEL

    u_pool = proj_ref[:, 0:256]
    dq = proj_ref[:, 256:512]
    wq = proj_ref[:, 1024:1280]
    ca = proj_ref[:, 1536:1792]
    cg = proj_ref[:, 1792:2048]

    lam_init = 0.8 - 0.6 * math.exp(-0.3 * l)
    lp = lamp_ref[...]
    lam = (jnp.exp(jnp.sum(lp[0:1] * lp[1:2], axis=-1, keepdims=True))
           - jnp.exp(jnp.sum(lp[2:3] * lp[3:4], axis=-1, keepdims=True)) + lam_init)
    gain = subln_ref[...] * (1.0 - lam_init)

    if dec:
        q = pl.program_id(1)
        nq = pl.num_programs(1)
        pv = (q > 0).astype(F32)
        nv = (q < nq - 1).astype(F32)
        pool_pad = jnp.concatenate([pool_p_ref[...] * pv, u_pool, pool_n_ref[...] * nv], axis=0)
        cp = conv_p_ref[HALO - CONV_PAD:, :] * pv
        cn = conv_n_ref[:CONV_PAD, :] * nv
        apad = jnp.concatenate([cp[:, :256], ca, cn[:, :256]], axis=0)
        gpad = jnp.concatenate([cp[:, 256:], cg, cn[:, 256:]], axis=0)
        kl = kv_ref[:, 0:256]
        vl = kv_ref[:, 256:512]
        dks, dvs = [cdk_ref[...], kl], [cdv_ref[...], vl]
        start = jnp.clip(q * tq - WINDOW, 0, DEC_SEQ - (tq + 2 * WINDOW))
        start = pl.multiple_of(start, WINDOW)
        band = tq + 2 * WINDOW
        wkb = kv_ref[pl.ds(start, band), 512:640]
        wvb = kv_ref[pl.ds(start, band), 640:768]
        qpos = q * tq + (lax.broadcasted_iota(jnp.int32, (2 * tq, band), 0) & (tq - 1))
        kpos = start + lax.broadcasted_iota(jnp.int32, (2 * tq, band), 1)
        wmask = jnp.abs(qpos - kpos) <= WINDOW
        wks, wvs, wms = [wkb, cwk_ref[...]], [wvb, cwv_ref[...]], [wmask, None]
    else:
        pool_pad = u_pool
        zpad = jnp.zeros((CONV_PAD, 256), F32)
        apad = jnp.concatenate([zpad, ca, zpad], axis=0)
        gpad = jnp.concatenate([zpad, cg, zpad], axis=0)
        dks, dvs = [proj_ref[:, 512:768].astype(BF16)], [proj_ref[:, 768:1024].astype(BF16)]
        wks, wvs, wms = [proj_ref[:, 1280:1408].astype(BF16)], [proj_ref[:, 1408:1536].astype(BF16)], [None]

    y_pool = _pool(pool_pad, u_pool, band_ref, invcnt_ref[...], wbd_ref[...], pscale_ref[...])
    y_diff = _diff_attn(dq, dks, dvs, lam, gain, tq)
    yw0, yw1 = _win_attn(wq, wks, wms, wvs, sink_ref, 4 * l, tq)
    y_conv = _conv_module(apad, gpad, dw_ref, dwb_ref[...], lng_ref[...], lnb_ref[...], pw_ref[...], tq)

    o = (_dot(y_pool.astype(BF16), wout_ref[0:256, :])
         + _dot(y_diff.astype(BF16), wout_ref[256:512, :])
         + _dot(yw0.astype(BF16), wout_ref[512:640, :])
         + _dot(yw1.astype(BF16), wout_ref[640:768, :])
         + _dot(y_conv.astype(BF16), wout_ref[768:1024, :]))
    mod = mod_ref[...]
    x1 = x_ref[...] + mod[:, 2 * d:3 * d] * _rms(o, gpost_ref[...])
    x1_ref[...] = x1
    h2 = _rms(x1, gpre_ref[...]) * (1.0 + mod[:, 4 * d:5 * d]) + mod[:, 3 * d:4 * d]
    h2_ref[...] = h2.astype(BF16)


def _layer_spec(shape, l):
    nz = len(shape)
    return pl.BlockSpec((None,) + tuple(shape), lambda *_: (l,) + (0,) * nz,
                        pipeline_mode=pl.Buffered(1))


def _mixer(proj, x, mod, P, l, *, dec, kvb=None, caches=None):
    rows = x.shape[0]
    if dec:
        tq = TQ_DEC
        nq = DEC_SEQ // tq
        grid = (DEC_BATCH, nq)
        hb = tq // HALO
        nhb = rows // HALO

        def row_map(b, q):
            return (b * nq + q, 0)

        def prev_map(col):
            return lambda b, q: (jnp.maximum((b * nq + q) * hb - 1, 0), col)

        def next_map(col):
            return lambda b, q: (jnp.minimum((b * nq + q + 1) * hb, nhb - 1), col)

        in_specs = [
            pl.BlockSpec((tq, IN_WIDTH), row_map),
            pl.BlockSpec((HALO, 256), prev_map(0)),
            pl.BlockSpec((HALO, 256), next_map(0)),
            pl.BlockSpec((HALO, 512), prev_map(3)),
            pl.BlockSpec((HALO, 512), next_map(3)),
            pl.BlockSpec((DEC_SEQ, KV_WIDTH), lambda b, q: (b, 0)),
            pl.BlockSpec((None, None, PAST_LEN, 256), lambda b, q: (b, l, 0, 0)),
            pl.BlockSpec((None, None, PAST_LEN, 256), lambda b, q: (b, l, 0, 0)),
            pl.BlockSpec((None, None, PAST_LEN, 128), lambda b, q: (b, l, 0, 0)),
            pl.BlockSpec((None, None, PAST_LEN, 128), lambda b, q: (b, l, 0, 0)),
        ]
        args = [proj, proj, proj, proj, proj, kvb, *caches]
        mod_spec = pl.BlockSpec((None, 1, 6 * D_MODEL), lambda b, q: (b, 0, 0))
        band_spec = pl.BlockSpec((4, tq, tq + 2 * HALO), lambda b, q: (0, 0, 0), pipeline_mode=pl.Buffered(1))
        invcnt_spec = pl.BlockSpec((tq, 256), lambda b, q: (q, 0))
        band, invcnt = P["band_dec"], P["invcnt_dec"]
        sem = ("arbitrary", "arbitrary")
    else:
        tq = SEQ
        grid = (rows // tq,)

        def row_map(i):
            return (i, 0)

        in_specs = [pl.BlockSpec((tq, IN_WIDTH), row_map)]
        args = [proj]
        mod_spec = pl.BlockSpec((None, 1, 6 * D_MODEL), lambda i: (0, 0, 0))
        band_spec = pl.BlockSpec((4, tq, tq), lambda i: (0, 0, 0), pipeline_mode=pl.Buffered(1))
        invcnt_spec = pl.BlockSpec((tq, 256), lambda i: (0, 0))
        band, invcnt = P["band_ctx"], P["invcnt_ctx"]
        sem = ("arbitrary",)

    in_specs += [
        pl.BlockSpec((tq, D_MODEL), row_map),
        mod_spec,
        band_spec,
        invcnt_spec,
        _layer_spec((256, 256), l),
        _layer_spec((1, 256), l),
        _layer_spec((4, DIFF_QK), l),
        _layer_spec((1, 256), l),
        pl.BlockSpec(memory_space=pltpu.SMEM),
        _layer_spec((CONV_K, 256), l),
        _layer_spec((1, 256), l),
        _layer_spec((1, 256), l),
        _layer_spec((1, 256), l),
        _layer_spec((256, 256), l),
        _layer_spec((D_MODEL, D_MODEL), l),
        _layer_spec((1, D_MODEL), l),
        _layer_spec((1, D_MODEL), l),
    ]
    args += [x, mod, band, invcnt, P["pool_wbd"], P["pool_scale"], P["lam_p"], P["subln"], P["sink"],
             P["conv_dw"], P["conv_dw_b"], P["conv_ln_g"], P["conv_ln_b"], P["conv_pw"], P["w_out"],
             P["g_post_mix"], P["g_pre_ffn"]]
    return pl.pallas_call(
        functools.partial(_mixer_kernel, dec=dec, l=l, tq=tq),
        out_shape=[jax.ShapeDtypeStruct((rows, D_MODEL), F32),
                   jax.ShapeDtypeStruct((rows, D_MODEL), BF16)],
        grid=grid,
        in_specs=in_specs,
        out_specs=[pl.BlockSpec((tq, D_MODEL), row_map), pl.BlockSpec((tq, D_MODEL), row_map)],
        compiler_params=pltpu.CompilerParams(dimension_semantics=sem, vmem_limit_bytes=VMEM_LIMIT),
        name="mixer_dec" if dec else "mixer_ctx",
    )(*args)


def _ffn_kernel(h_ref, hp_ref, hn_ref, x_ref, mod_ref, wup_ref, dw_ref, dwb_ref, wd_ref, gpost_ref,
                o_ref, act_ref, *, tm, seq_len):
    i = pl.program_id(0)
    hx = jnp.concatenate([hp_ref[...], h_ref[...], hn_ref[...]], axis=0)
    pos = (i * tm + lax.broadcasted_iota(jnp.int32, (tm, 1), 0)) % seq_len
    has_left = pos != 0
    has_right = pos != seq_len - 1
    for j in range(D_FF // FFN_CHUNK):
        ys = []
        for part in range(2):
            c0 = part * D_FF + j * FFN_CHUNK
            u = _dot(hx, wup_ref[:, c0:c0 + FFN_CHUNK])
            w = dw_ref[:, c0:c0 + FFN_CHUNK]
            ys.append(jnp.where(has_left, u[FFN_HALO - 1:FFN_HALO - 1 + tm], 0.0) * w[0:1]
                      + u[FFN_HALO:FFN_HALO + tm] * w[1:2]
                      + jnp.where(has_right, u[FFN_HALO + 1:FFN_HALO + 1 + tm], 0.0) * w[2:3]
                      + dwb_ref[:, c0:c0 + FFN_CHUNK])
        gate, up = ys
        act_ref[:, j * FFN_CHUNK:(j + 1) * FFN_CHUNK] = (gate * jax.nn.sigmoid(gate) * up).astype(BF16)
    o = _dot(act_ref[...], wd_ref[...])
    g2 = mod_ref[:, 5 * D_MODEL:6 * D_MODEL]
    o_ref[...] = x_ref[...] + g2 * _rms(o, gpost_ref[...])


def _ffn(h2, x1, mod, P, l, seq_len):
    rows = x1.shape[0]
    tm = TM_FFN
    nb = mod.shape[0]
    tiles_per_b = rows // nb // tm
    hb = tm // FFN_HALO
    nhb = rows // FFN_HALO
    return pl.pallas_call(
        functools.partial(_ffn_kernel, tm=tm, seq_len=seq_len),
        out_shape=jax.ShapeDtypeStruct((rows, D_MODEL), F32),
        grid=(rows // tm,),
        in_specs=[
            pl.BlockSpec((tm, D_MODEL), lambda i: (i, 0)),
            pl.BlockSpec((FFN_HALO, D_MODEL), lambda i: (jnp.maximum(i * hb - 1, 0), 0)),
            pl.BlockSpec((FFN_HALO, D_MODEL), lambda i: (jnp.minimum((i + 1) * hb, nhb - 1), 0)),
            pl.BlockSpec((tm, D_MODEL), lambda i: (i, 0)),
            pl.BlockSpec((None, 1, 6 * D_MODEL), lambda i: (i // tiles_per_b, 0, 0)),
            _layer_spec((D_MODEL, 2 * D_FF), l),
            _layer_spec((3, 2 * D_FF), l),
            _layer_spec((1, 2 * D_FF), l),
            _layer_spec((D_FF, D_MODEL), l),
            _layer_spec((1, D_MODEL), l),
        ],
        out_specs=pl.BlockSpec((tm, D_MODEL), lambda i: (i, 0)),
        scratch_shapes=[pltpu.VMEM((tm, D_FF), BF16)],
        compiler_params=pltpu.CompilerParams(
            dimension_semantics=("arbitrary",), vmem_limit_bytes=VMEM_LIMIT),
        name="ffn",
    )(h2, h2, h2, x1, mod, P["ffn_up"], P["ffn_dw"], P["ffn_dw_b"], P["ffn_down"], P["g_post_ffn"])


def _pool_tables(tq, halo, seq_len):
    t = np.arange(tq)[:, None]
    jp = np.arange(tq + 2 * halo)[None, :]
    band = np.zeros((len(POOL_WINDOWS), tq, tq + 2 * halo), np.float32)
    inv = np.zeros((seq_len, 256), np.float32)
    pos = np.arange(seq_len)
    for g, w in enumerate(POOL_WINDOWS):
        lo = t - w // 2 + halo
        band[g] = ((jp >= lo) & (jp < lo + w)).astype(np.float32)
        cnt = np.clip(pos - w // 2 + w, 0, seq_len) - np.clip(pos - w // 2, 0, seq_len)
        inv[:, g * 64:(g + 1) * 64] = (1.0 / cnt.astype(np.float64))[:, None].astype(np.float32)
    return jnp.asarray(band, BF16), jnp.asarray(inv, F32)


def _rope_tables(seq_len, dim):
    rows = seq_len // GRID_W
    r = np.repeat(np.arange(rows), GRID_W).astype(np.float32)
    col = np.tile(np.arange(GRID_W), rows).astype(np.float32)
    n = dim // 4
    inv = (ROPE_BASE ** (-np.arange(n) / n)).astype(np.float32)
    ang = np.concatenate([r[:, None] * inv[None], col[:, None] * inv[None]], axis=-1)
    cos, sin = np.cos(ang), np.sin(ang)
    reps = LANES // dim
    cos_t = np.tile(np.concatenate([cos, cos], axis=-1), (1, reps))
    sin_t = np.tile(np.concatenate([-sin, sin], axis=-1), (1, reps))
    return jnp.asarray(cos_t, F32), jnp.asarray(sin_t, F32)


def kernel(x_prompt, x_sample, c, cache_diff_k, cache_diff_v, cache_win_k, cache_win_v, c_ctx, w_ada, b_ada, g_pre_mix, g_post_mix, g_pre_ffn, g_post_ffn, w_in, w_out, pool_w, pool_scale, diff_lq1, diff_lk1, diff_lq2, diff_lk2, diff_subln, win_sink, conv_dw, conv_dw_b, conv_ln_g, conv_ln_b, conv_pw, ffn_up, ffn_dw, ffn_dw_b, ffn_down):
    d = D_MODEL
    xp = x_prompt.reshape(BATCH * SEQ, d)
    xs = x_sample.reshape(DEC_BATCH * DEC_SEQ, d)

    cond8 = jnp.concatenate([c_ctx[None, :], c, jnp.zeros((8 - 1 - DEC_BATCH, d), F32)], axis=0)
    mod = _ada(cond8, w_ada, b_ada.reshape(DEPTH, 1, 6 * d))

    band_ctx, invcnt_ctx = _pool_tables(SEQ, 0, SEQ)
    band_dec, invcnt_dec = _pool_tables(TQ_DEC, HALO, DEC_SEQ)
    eye4 = jnp.eye(4, dtype=F32)
    P = dict(
        band_ctx=band_ctx, invcnt_ctx=invcnt_ctx, band_dec=band_dec, invcnt_dec=invcnt_dec,
        pool_wbd=jnp.einsum("lgcd,gh->lgchd", pool_w, eye4).reshape(DEPTH, 256, 256).astype(BF16),
        pool_scale=pool_scale.reshape(DEPTH, 1, 256),
        lam_p=jnp.stack([diff_lq1, diff_lk1, diff_lq2, diff_lk2], axis=1),
        subln=jnp.tile(diff_subln, (1, 4)).reshape(DEPTH, 1, 256),
        sink=win_sink.reshape(DEPTH * 4),
        conv_dw=conv_dw, conv_dw_b=conv_dw_b.reshape(DEPTH, 1, 256),
        conv_ln_g=conv_ln_g.reshape(DEPTH, 1, 256), conv_ln_b=conv_ln_b.reshape(DEPTH, 1, 256),
        conv_pw=conv_pw.astype(BF16), w_out=w_out.astype(BF16),
        g_post_mix=g_post_mix.reshape(DEPTH, 1, d), g_pre_ffn=g_pre_ffn.reshape(DEPTH, 1, d),
        ffn_up=ffn_up.astype(BF16), ffn_dw=ffn_dw, ffn_dw_b=ffn_dw_b.reshape(DEPTH, 1, 2 * D_FF),
        ffn_down=ffn_down.astype(BF16), g_post_ffn=g_post_ffn.reshape(DEPTH, 1, d),
    )
    w_in_b = w_in.astype(BF16)
    g_pre = g_pre_mix.reshape(DEPTH, 1, d)
    rope_tabs = _rope_tables(DEC_SEQ, DIFF_QK) + _rope_tables(DEC_SEQ, WIN_HD)
    caches = (cache_diff_k.reshape(DEC_BATCH, DEPTH, PAST_LEN, 256).astype(BF16),
              cache_diff_v.reshape(DEC_BATCH, DEPTH, PAST_LEN, 256).astype(BF16),
              cache_win_k.reshape(DEC_BATCH, DEPTH, PAST_LEN, 128).astype(BF16),
              cache_win_v.reshape(DEC_BATCH, DEPTH, PAST_LEN, 128).astype(BF16))

    kd, vd, kw, vw = [], [], [], []
    for l in range(DEPTH):
        mod_ctx = mod[l, 0:1].reshape(1, 1, 6 * d)
        mod_dec = mod[l, 1:1 + DEC_BATCH].reshape(DEC_BATCH, 1, 6 * d)

        (proj_c,) = _inproj(xp, mod_ctx, g_pre, w_in_b, l, None)
        kd.append(proj_c[:, 512:768].reshape(BATCH, SEQ, 4, 64))
        vd.append(proj_c[:, 768:1024].reshape(BATCH, SEQ, 4, 64))
        kw.append(proj_c[:, 1280:1408].reshape(BATCH, SEQ, 2, 64))
        vw.append(proj_c[:, 1408:1536].reshape(BATCH, SEQ, 2, 64))
        xp1, hp2 = _mixer(proj_c, xp, mod_ctx, P, l, dec=False)
        xp = _ffn(hp2, xp1, mod_ctx, P, l, SEQ)

        proj_d, kvb = _inproj(xs, mod_dec, g_pre, w_in_b, l, rope_tabs)
        xs1, hs2 = _mixer(proj_d, xs, mod_dec, P, l, dec=True, kvb=kvb, caches=caches)
        xs = _ffn(hs2, xs1, mod_dec, P, l, DEC_SEQ)

    return (xp.reshape(BATCH, SEQ, d), xs.reshape(DEC_BATCH, DEC_SEQ, d),
            jnp.stack(kd, axis=1), jnp.stack(vd, axis=1), jnp.stack(kw, axis=1), jnp.stack(vw, axis=1))
```

```python
import functools
import math

import jax
import jax.numpy as jnp
import numpy as np
from jax import lax
from jax.experimental import pallas as pl
from jax.experimental.pallas import tpu as pltpu

F32 = jnp.float32
BF16 = jnp.bfloat16

D_MODEL = 1024
BATCH = 16
SEQ = 256
DEPTH = 2
DEC_BATCH = 2
DEC_SEQ = 2048
PAST_LEN = 256
GRID_W = 64
POOL_WINDOWS = (2, 4, 8, 16)
DIFF_QK = 32
WIN_HD = 64
WINDOW = 128
CONV_K = 31
IN_WIDTH = 2048
D_FF = 2816
ROPE_BASE = 10000.0
EPS = 1e-6
LOG2E = 1.4426950408889634

LANES = 128
HALO = 128
CONV_PAD = 16
FFN_HALO = 16
FFN_CHUNK = 256
KV_WIDTH = 768
VMEM_LIMIT = 56 * 1024 * 1024

TM_IN = 512
TQ_DEC = 256
TM_FFN = 512


def _rms(x, g):
    return x * lax.rsqrt(jnp.mean(x * x, axis=-1, keepdims=True) + EPS) * g


def _dot(a, b):
    return jnp.dot(a, b, preferred_element_type=F32)


def _dot_nt(a, b):
    return lax.dot_general(a, b, (((1,), (1,)), ((), ())), preferred_element_type=F32)


def _lane_range(shape, lo, hi):
    lane = lax.broadcasted_iota(jnp.int32, shape, 1)
    return (lane >= lo) & (lane < hi)


def _swap_middle_heads(t0, t1):
    lo = _lane_range(t0.shape, 0, 64)
    return (jnp.where(lo, t0, pltpu.roll(t1, 64, 1)),
            jnp.where(lo, pltpu.roll(t0, 64, 1), t1))


def _ada_kernel(c_ref, w_ref, b_ref, o_ref):
    c = c_ref[...]
    s = (c * jax.nn.sigmoid(c)).astype(BF16)
    o_ref[...] = _dot(s, w_ref[...].astype(BF16)) + b_ref[...]


def _ada(cond8, w_ada, b_ada):
    tn = 1024
    return pl.pallas_call(
        _ada_kernel,
        out_shape=jax.ShapeDtypeStruct((DEPTH, 8, 6 * D_MODEL), F32),
        grid=(DEPTH, 6 * D_MODEL // tn),
        in_specs=[
            pl.BlockSpec((8, D_MODEL), lambda l, j: (0, 0)),
            pl.BlockSpec((None, D_MODEL, tn), lambda l, j: (l, 0, j)),
            pl.BlockSpec((None, 1, tn), lambda l, j: (l, 0, j)),
        ],
        out_specs=pl.BlockSpec((None, 8, tn), lambda l, j: (l, 0, j)),
        compiler_params=pltpu.CompilerParams(
            dimension_semantics=("arbitrary", "arbitrary"), vmem_limit_bytes=VMEM_LIMIT),
        name="ada_mod",
    )(cond8, w_ada, b_ada)


def _rope_tile(v, cos, sin_signed, half):
    is_a = (lax.broadcasted_iota(jnp.int32, v.shape, 1) % (2 * half)) < half
    partner = jnp.where(is_a, pltpu.roll(v, LANES - half, 1), pltpu.roll(v, half, 1))
    return v * cos + partner * sin_signed


def _inproj_kernel(*refs, rope):
    if rope:
        (x_ref, mod_ref, g_ref, w_ref, cd_ref, sd_ref, cw_ref, sw_ref, proj_ref, kv_ref) = refs
    else:
        (x_ref, mod_ref, g_ref, w_ref, proj_ref) = refs
    d = D_MODEL
    mod = mod_ref[...]
    h = _rms(x_ref[...], g_ref[...]) * (1.0 + mod[:, d:2 * d]) + mod[:, 0:d]
    proj = _dot(h.astype(BF16), w_ref[...])
    tiles = [proj[:, i * LANES:(i + 1) * LANES] for i in range(IN_WIDTH // LANES)]
    if rope:
        cd, sd, cw, sw = cd_ref[...], sd_ref[...], cw_ref[...], sw_ref[...]
        for i in (2, 3, 4, 5):
            tiles[i] = _rope_tile(tiles[i], cd, sd, DIFF_QK // 2)
        for i in (8, 9, 10):
            tiles[i] = _rope_tile(tiles[i], cw, sw, WIN_HD // 2)
    tiles[8], tiles[9] = _swap_middle_heads(tiles[8], tiles[9])
    for i, t in enumerate(tiles):
        proj_ref[:, i * LANES:(i + 1) * LANES] = t
    if rope:
        for j, i in enumerate((4, 5, 6, 7, 10, 11)):
            kv_ref[:, j * LANES:(j + 1) * LANES] = tiles[i].astype(BF16)


def _inproj(x, mod, g, w_in_b, l, rope_tabs):
    rows = x.shape[0]
    tm = TM_IN
    nb = mod.shape[0]
    tiles_per_b = rows // nb // tm
    rope = rope_tabs is not None
    in_specs = [
        pl.BlockSpec((tm, D_MODEL), lambda i: (i, 0)),
        pl.BlockSpec((None, 1, 6 * D_MODEL), lambda i: (i // tiles_per_b, 0, 0)),
        pl.BlockSpec((None, 1, D_MODEL), lambda i: (l, 0, 0)),
        pl.BlockSpec((None, D_MODEL, IN_WIDTH), lambda i: (l, 0, 0), pipeline_mode=pl.Buffered(1)),
    ]
    args = [x, mod, g, w_in_b]
    out_shape = [jax.ShapeDtypeStruct((rows, IN_WIDTH), F32)]
    out_specs = [pl.BlockSpec((tm, IN_WIDTH), lambda i: (i, 0))]
    if rope:
        for t in rope_tabs:
            in_specs.append(pl.BlockSpec((tm, LANES), lambda i: (i % tiles_per_b, 0)))
            args.append(t)
        out_shape.append(jax.ShapeDtypeStruct((rows, KV_WIDTH), BF16))
        out_specs.append(pl.BlockSpec((tm, KV_WIDTH), lambda i: (i, 0)))
    return pl.pallas_call(
        functools.partial(_inproj_kernel, rope=rope),
        out_shape=out_shape,
        grid=(rows // tm,),
        in_specs=in_specs,
        out_specs=out_specs,
        compiler_params=pltpu.CompilerParams(
            dimension_semantics=("arbitrary",), vmem_limit_bytes=VMEM_LIMIT),
        name="inproj_dec" if rope else "inproj_ctx",
    )(*args)


def _pool(xpad, u, band_ref, invcnt, wbd, scale):
    xhi = xpad.astype(BF16)
    xlo = (xpad - xhi.astype(F32)).astype(BF16)
    sums = []
    for g in range(len(POOL_WINDOWS)):
        b = band_ref[g]
        sums.append(_dot(b, xhi) + _dot(b, xlo))
    shape = u.shape
    s = jnp.where(_lane_range(shape, 0, 64), sums[0],
                  jnp.where(_lane_range(shape, 64, 128), sums[1],
                            jnp.where(_lane_range(shape, 128, 192), sums[2], sums[3])))
    pooled = s * invcnt - u
    return _dot(pooled.astype(BF16), wbd) * scale


def _conv_module(apad, gpad, dw_ref, dwb, lng, lnb, pw, tq):
    u = apad * jax.nn.sigmoid(gpad)
    rows = u.shape[0]
    acc = jnp.broadcast_to(dwb, (tq, u.shape[1]))
    for r in range(8):
        ur = u if r == 0 else pltpu.roll(u, rows - r, 0)
        for a8 in range(4):
            k = 8 * a8 + r - 1
            if 0 <= k < CONV_K:
                acc = acc + ur[8 * a8:8 * a8 + tq] * dw_ref[k:k + 1, :]
    mu = jnp.mean(acc, axis=-1, keepdims=True)
    xc = acc - mu
    var = jnp.mean(xc * xc, axis=-1, keepdims=True)
    y = xc * lax.rsqrt(var + EPS) * lng + lnb
    y = y * jax.nn.sigmoid(y)
    return _dot(y.astype(BF16), pw)


def _ones_outside(v, lo, hi):
    m = jnp.where(_lane_range((1, v.shape[1]), lo, hi), 1.0, 0.0).astype(BF16)
    return v * m + (1.0 - m)


def _diff_attn(dq, ks, vps, lam, gain, tq, group):
    dqs = dq * (DIFF_QK ** -0.5 * LOG2E)
    ytiles = [jnp.zeros((tq, LANES), F32), jnp.zeros((tq, LANES), F32)]
    for h0 in range(0, 4, group):
        qs = []
        for h in range(h0, h0 + group):
            for lo in (h * 64, h * 64 + DIFF_QK):
                qs.append(jnp.where(_lane_range(dqs.shape, lo, lo + DIFF_QK), dqs, 0.0))
        qg = jnp.concatenate(qs, axis=0).astype(BF16)
        ss = [_dot_nt(qg, k) for k in ks]
        mx = functools.reduce(jnp.maximum, [jnp.max(s, axis=-1, keepdims=True) for s in ss])
        es = [jnp.exp2(s - mx).astype(BF16) for s in ss]
        for h in range(h0, h0 + group):
            r0 = (h - h0) * 2 * tq
            o = None
            for e, vp in zip(es, vps(h)):
                t = _dot(e[r0:r0 + 2 * tq], vp)
                o = t if o is None else o + t
            tile = h // 2
            ot = o[:, tile * LANES:(tile + 1) * LANES]
            hlo = (h % 2) * 64
            den = jnp.max(jnp.where(_lane_range(ot.shape, 64 - hlo, 128 - hlo), ot, 0.0), axis=-1, keepdims=True)
            r = ot / den
            od = jnp.where(_lane_range((tq, LANES), hlo, hlo + 64), r[:tq] - lam * r[tq:], 0.0)
            ms = jnp.sum(od * od, axis=-1, keepdims=True) * (1.0 / 64.0)
            ytiles[tile] = ytiles[tile] + od * lax.rsqrt(ms + EPS)
    return jnp.concatenate(ytiles, axis=1) * gain


def _win_attn(wq, ks, masks, vs, sink_ref, sink_base, tq, group):
    wqs = wq * (WIN_HD ** -0.5 * LOG2E)
    t0, t1 = wqs[:, :LANES], wqs[:, LANES:]
    y0 = jnp.zeros((tq, LANES), F32)
    y1 = jnp.zeros((tq, LANES), F32)
    rows = 2 * tq * group
    row = lax.broadcasted_iota(jnp.int32, (rows, 1), 0)
    for j0 in range(0, 2, group):
        qs = []
        sk = jnp.zeros((rows, 1), F32)
        for j in range(j0, j0 + group):
            lm = _lane_range(t0.shape, j * 64, j * 64 + 64)
            qs += [jnp.where(lm, t0, 0.0), jnp.where(lm, t1, 0.0)]
            for g in range(2):
                r0 = ((j - j0) * 2 + g) * tq
                sk = jnp.where((row >= r0) & (row < r0 + tq), sink_ref[sink_base + 2 * j + g], sk)
        sk = sk * LOG2E
        qg = jnp.concatenate(qs, axis=0).astype(BF16)
        ss = []
        for k, m in zip(ks, masks):
            s = _dot_nt(qg, k)
            if m is not None:
                s = jnp.where(m, s, -1e30)
            ss.append(s)
        mx = functools.reduce(jnp.maximum, [jnp.max(s, axis=-1, keepdims=True) for s in ss])
        mx = jnp.maximum(mx, sk)
        es = [jnp.exp2(s - mx).astype(BF16) for s in ss]
        esink = jnp.exp2(sk - mx)
        for j in range(j0, j0 + group):
            r0 = (j - j0) * 2 * tq
            o = None
            for e, v in zip(es, vs):
                t = _dot(e[r0:r0 + 2 * tq], _ones_outside(v, j * 64, j * 64 + 64))
                o = t if o is None else o + t
            other = _lane_range(o.shape, 64 - j * 64, 128 - j * 64)
            den = jnp.max(jnp.where(other, o, 0.0), axis=-1, keepdims=True)
            r = o / (den + esink[r0:r0 + 2 * tq])
            lm = _lane_range((tq, LANES), j * 64, j * 64 + 64)
            y0 = y0 + jnp.where(lm, r[:tq], 0.0)
            y1 = y1 + jnp.where(lm, r[tq:], 0.0)
    return _swap_middle_heads(y0, y1)


def _mixer_kernel(*refs, dec, l, tq):
    if dec:
        (proj_ref, pool_p_ref, pool_n_ref, conv_p_ref, conv_n_ref, kv_ref,
         cdk_ref, cdv_ref, cwk_ref, cwv_ref, *rest) = refs
    else:
        proj_ref, *rest = refs
    (x_ref, mod_ref, band_ref, invcnt_ref, wbd_ref, pscale_ref, lamp_ref, subln_ref, sink_ref,
     dw_ref, dwb_ref, lng_ref, lnb_ref, pw_ref, wout_ref, gpost_ref, gpre_ref,
     x1_ref, h2_ref, *scratch) = rest
    d = D_MODEL

    u_pool = proj_ref[:, 0:256]
    dq = proj_ref[:, 256:512]
    wq = proj_ref[:, 1024:1280]
    ca = proj_ref[:, 1536:1792]
    cg = proj_ref[:, 1792:2048]

    lam_init = 0.8 - 0.6 * math.exp(-0.3 * l)
    lp = lamp_ref[...]
    lam = (jnp.exp(jnp.sum(lp[0:1] * lp[1:2], axis=-1, keepdims=True))
           - jnp.exp(jnp.sum(lp[2:3] * lp[3:4], axis=-1, keepdims=True)) + lam_init)
    gain = subln_ref[...] * (1.0 - lam_init)

    if dec:
        q = pl.program_id(1)
        nq = pl.num_programs(1)
        pv = (q > 0).astype(F32)
        nv = (q < nq - 1).astype(F32)
        pool_pad = jnp.concatenate([pool_p_ref[...] * pv, u_pool, pool_n_ref[...] * nv], axis=0)
        cp = conv_p_ref[HALO - CONV_PAD:, :] * pv
        cn = conv_n_ref[:CONV_PAD, :] * nv
        apad = jnp.concatenate([cp[:, :256], ca, cn[:, :256]], axis=0)
        gpad = jnp.concatenate([cp[:, 256:], cg, cn[:, 256:]], axis=0)
        (vp_ref,) = scratch

        @pl.when(q == 0)
        def _():
            for h in range(4):
                vp_ref[h, 0:PAST_LEN, :] = _ones_outside(cdv_ref[...], h * 64, h * 64 + 64)
                vp_ref[h, PAST_LEN:, :] = _ones_outside(kv_ref[:, 256:512], h * 64, h * 64 + 64)

        dks = [cdk_ref[...], kv_ref[:, 0:256]]
        vps = lambda h: [vp_ref[h, 0:PAST_LEN, :], vp_ref[h, PAST_LEN:, :]]
        start =jnp.clip(q * tq - WINDOW, 0, DEC_SEQ - (tq + 2 * WINDOW))
        start = pl.multiple_of(start, WINDOW)
        band = tq + 2 * WINDOW
        wkb = kv_ref[pl.ds(start, band), 512:640]
        wvb = kv_ref[pl.ds(start, band), 640:768]
        qpos = q * tq + (lax.broadcasted_iota(jnp.int32, (2 * tq, band), 0) & (tq - 1))
        kpos = start + lax.broadcasted_iota(jnp.int32, (2 * tq, band), 1)
        wmask = jnp.abs(qpos - kpos) <= WINDOW
        wks, wvs, wms = [wkb, cwk_ref[...]], [wvb, cwv_ref[...]], [wmask, None]
    else:
        pool_pad = u_pool
        zpad = jnp.zeros((CONV_PAD, 256), F32)
        apad = jnp.concatenate([zpad, ca, zpad], axis=0)
        gpad = jnp.concatenate([zpad, cg, zpad], axis=0)
        dks = [proj_ref[:, 512:768].astype(BF16)]
        dv = proj_ref[:, 768:1024].astype(BF16)
        vps = lambda h: [_ones_outside(dv, h * 64, h * 64 + 64)]
        wks, wvs, wms = [proj_ref[:, 1280:1408].astype(BF16)], [proj_ref[:, 1408:1536].astype(BF16)], [None]

    y_pool = _pool(pool_pad, u_pool, band_ref, invcnt_ref[...], wbd_ref[...], pscale_ref[...])
    y_diff = _diff_attn(dq, dks, vps, lam, gain, tq, group=1 if dec else 4)
    yw0, yw1 = _win_attn(wq, wks, wms, wvs, sink_ref, 4 * l, tq, group=1 if dec else 2)
    y_conv = _conv_module(apad, gpad, dw_ref, dwb_ref[...], lng_ref[...], lnb_ref[...], pw_ref[...], tq)

    o = (_dot(y_pool.astype(BF16), wout_ref[0:256, :])
         + _dot(y_diff.astype(BF16), wout_ref[256:512, :])
         + _dot(yw0.astype(BF16), wout_ref[512:640, :])
         + _dot(yw1.astype(BF16), wout_ref[640:768, :])
         + _dot(y_conv.astype(BF16), wout_ref[768:1024, :]))
    mod = mod_ref[...]
    x1 = x_ref[...] + mod[:, 2 * d:3 * d] * _rms(o, gpost_ref[...])
    x1_ref[...] = x1
    h2 = _rms(x1, gpre_ref[...]) * (1.0 + mod[:, 4 * d:5 * d]) + mod[:, 3 * d:4 * d]
    h2_ref[...] = h2.astype(BF16)


def _layer_spec(shape, l):
    nz = len(shape)
    return pl.BlockSpec((None,) + tuple(shape), lambda *_: (l,) + (0,) * nz,
                        pipeline_mode=pl.Buffered(1))


def _mixer(proj, x, mod, P, l, *, dec, kvb=None, caches=None):
    rows = x.shape[0]
    if dec:
        tq = TQ_DEC
        nq = DEC_SEQ // tq
        grid = (DEC_BATCH, nq)
        hb = tq // HALO
        nhb = rows // HALO

        def row_map(b, q):
            return (b * nq + q, 0)

        def prev_map(col):
            return lambda b, q: (jnp.maximum((b * nq + q) * hb - 1, 0), col)

        def next_map(col):
            return lambda b, q: (jnp.minimum((b * nq + q + 1) * hb, nhb - 1), col)

        in_specs = [
            pl.BlockSpec((tq, IN_WIDTH), row_map),
            pl.BlockSpec((HALO, 256), prev_map(0)),
            pl.BlockSpec((HALO, 256), next_map(0)),
            pl.BlockSpec((HALO, 512), prev_map(3)),
            pl.BlockSpec((HALO, 512), next_map(3)),
            pl.BlockSpec((DEC_SEQ, KV_WIDTH), lambda b, q: (b, 0)),
            pl.BlockSpec((None, None, PAST_LEN, 256), lambda b, q: (b, l, 0, 0)),
            pl.BlockSpec((None, None, PAST_LEN, 256), lambda b, q: (b, l, 0, 0)),
            pl.BlockSpec((None, None, PAST_LEN, 128), lambda b, q: (b, l, 0, 0)),
            pl.BlockSpec((None, None, PAST_LEN, 128), lambda b, q: (b, l, 0, 0)),
        ]
        args = [proj, proj, proj, proj, proj, kvb, *caches]
        mod_spec = pl.BlockSpec((None, 1, 6 * D_MODEL), lambda b, q: (b, 0, 0))
        band_spec = pl.BlockSpec((4, tq, tq + 2 * HALO), lambda b, q: (0, 0, 0), pipeline_mode=pl.Buffered(1))
        invcnt_spec = pl.BlockSpec((tq, 256), lambda b, q: (q, 0))
        band, invcnt = P["band_dec"], P["invcnt_dec"]
        sem = ("arbitrary", "arbitrary")
    else:
        tq = SEQ
        grid = (rows // tq,)

        def row_map(i):
            return (i, 0)

        in_specs = [pl.BlockSpec((tq, IN_WIDTH), row_map)]
        args = [proj]
        mod_spec = pl.BlockSpec((None, 1, 6 * D_MODEL), lambda i: (0, 0, 0))
        band_spec = pl.BlockSpec((4, tq, tq), lambda i: (0, 0, 0), pipeline_mode=pl.Buffered(1))
        invcnt_spec = pl.BlockSpec((tq, 256), lambda i: (0, 0))
        band, invcnt = P["band_ctx"], P["invcnt_ctx"]
        sem = ("arbitrary",)

    in_specs += [
        pl.BlockSpec((tq, D_MODEL), row_map),
        mod_spec,
        band_spec,
        invcnt_spec,
        _layer_spec((256, 256), l),
        _layer_spec((1, 256), l),
        _layer_spec((4, DIFF_QK), l),
        _layer_spec((1, 256), l),
        pl.BlockSpec(memory_space=pltpu.SMEM),
        _layer_spec((CONV_K, 256), l),
        _layer_spec((1, 256), l),
        _layer_spec((1, 256), l),
        _layer_spec((1, 256), l),
        _layer_spec((256, 256), l),
        _layer_spec((D_MODEL, D_MODEL), l),
        _layer_spec((1, D_MODEL), l),
        _layer_spec((1, D_MODEL), l),
    ]
    args += [x, mod, band, invcnt, P["pool_wbd"], P["pool_scale"], P["lam_p"], P["subln"], P["sink"],
             P["conv_dw"], P["conv_dw_b"], P["conv_ln_g"], P["conv_ln_b"], P["conv_pw"], P["w_out"],
             P["g_post_mix"], P["g_pre_ffn"]]
    return pl.pallas_call(
        functools.partial(_mixer_kernel, dec=dec, l=l, tq=tq),
        out_shape=[jax.ShapeDtypeStruct((rows, D_MODEL), F32),
                   jax.ShapeDtypeStruct((rows, D_MODEL), BF16)],
        grid=grid,
        in_specs=in_specs,
        out_specs=[pl.BlockSpec((tq, D_MODEL), row_map), pl.BlockSpec((tq, D_MODEL), row_map)],
        scratch_shapes=[pltpu.VMEM((4, PAST_LEN + DEC_SEQ, 256), BF16)] if dec else [],
        compiler_params=pltpu.CompilerParams(dimension_semantics=sem, vmem_limit_bytes=VMEM_LIMIT),
        name="mixer_dec" if dec else "mixer_ctx",
    )(*args)


def _ffn_kernel(h_ref, hp_ref, hn_ref, x_ref, mod_ref, wup_ref, dw_ref, dwb_ref, wd_ref, gpost_ref,
                o_ref, act_ref, *, tm, seq_len):
    i = pl.program_id(0)
    hx = jnp.concatenate([hp_ref[...], h_ref[...], hn_ref[...]], axis=0)
    pos = (i * tm + lax.broadcasted_iota(jnp.int32, (tm, 1), 0)) % seq_len
    has_left = pos != 0
    has_right = pos != seq_len - 1
    for j in range(D_FF // FFN_CHUNK):
        ys = []
        for part in range(2):
            c0 = part * D_FF + j * FFN_CHUNK
            u = _dot(hx, wup_ref[:, c0:c0 + FFN_CHUNK])
            w = dw_ref[:, c0:c0 + FFN_CHUNK]
            ys.append(jnp.where(has_left, u[FFN_HALO - 1:FFN_HALO - 1 + tm], 0.0) * w[0:1]
                      + u[FFN_HALO:FFN_HALO + tm] * w[1:2]
                      + jnp.where(has_right, u[FFN_HALO + 1:FFN_HALO + 1 + tm], 0.0) * w[2:3]
                      + dwb_ref[:, c0:c0 + FFN_CHUNK])
        gate, up = ys
        act_ref[:, j * FFN_CHUNK:(j + 1) * FFN_CHUNK] = (gate * jax.nn.sigmoid(gate) * up).astype(BF16)
    o = _dot(act_ref[...], wd_ref[...])
    g2 = mod_ref[:, 5 * D_MODEL:6 * D_MODEL]
    o_ref[...] = x_ref[...] + g2 * _rms(o, gpost_ref[...])


def _ffn(h2, x1, mod, P, l, seq_len):
    rows = x1.shape[0]
    tm = TM_FFN
    nb = mod.shape[0]
    tiles_per_b = rows // nb // tm
    hb = tm // FFN_HALO
    nhb = rows // FFN_HALO
    return pl.pallas_call(
        functools.partial(_ffn_kernel, tm=tm, seq_len=seq_len),
        out_shape=jax.ShapeDtypeStruct((rows, D_MODEL), F32),
        grid=(rows // tm,),
        in_specs=[
            pl.BlockSpec((tm, D_MODEL), lambda i: (i, 0)),
            pl.BlockSpec((FFN_HALO, D_MODEL), lambda i: (jnp.maximum(i * hb - 1, 0), 0)),
            pl.BlockSpec((FFN_HALO, D_MODEL), lambda i: (jnp.minimum((i + 1) * hb, nhb - 1), 0)),
            pl.BlockSpec((tm, D_MODEL), lambda i: (i, 0)),
            pl.BlockSpec((None, 1, 6 * D_MODEL), lambda i: (i // tiles_per_b, 0, 0)),
            _layer_spec((D_MODEL, 2 * D_FF), l),
            _layer_spec((3, 2 * D_FF), l),
            _layer_spec((1, 2 * D_FF), l),
            _layer_spec((D_FF, D_MODEL), l),
            _layer_spec((1, D_MODEL), l),
        ],
        out_specs=pl.BlockSpec((tm, D_MODEL), lambda i: (i, 0)),
        scratch_shapes=[pltpu.VMEM((tm, D_FF), BF16)],
        compiler_params=pltpu.CompilerParams(
            dimension_semantics=("arbitrary",), vmem_limit_bytes=VMEM_LIMIT),
        name="ffn",
    )(h2, h2, h2, x1, mod, P["ffn_up"], P["ffn_dw"], P["ffn_dw_b"], P["ffn_down"], P["g_post_ffn"])


def _pool_tables(tq, halo, seq_len):
    t = np.arange(tq)[:, None]
    jp = np.arange(tq + 2 * halo)[None, :]
    band = np.zeros((len(POOL_WINDOWS), tq, tq + 2 * halo), np.float32)
    inv = np.zeros((seq_len, 256), np.float32)
    pos = np.arange(seq_len)
    for g, w in enumerate(POOL_WINDOWS):
        lo = t - w // 2 + halo
        band[g] = ((jp >= lo) & (jp < lo + w)).astype(np.float32)
        cnt = np.clip(pos - w // 2 + w, 0, seq_len) - np.clip(pos - w // 2, 0, seq_len)
        inv[:, g * 64:(g + 1) * 64] = (1.0 / cnt.astype(np.float64))[:, None].astype(np.float32)
    return jnp.asarray(band, BF16), jnp.asarray(inv, F32)


def _rope_tables(seq_len, dim):
    rows = seq_len // GRID_W
    r = np.repeat(np.arange(rows), GRID_W).astype(np.float32)
    col = np.tile(np.arange(GRID_W), rows).astype(np.float32)
    n = dim // 4
    inv = (ROPE_BASE ** (-np.arange(n) / n)).astype(np.float32)
    ang = np.concatenate([r[:, None] * inv[None], col[:, None] * inv[None]], axis=-1)
    cos, sin = np.cos(ang), np.sin(ang)
    reps = LANES // dim
    cos_t = np.tile(np.concatenate([cos, cos], axis=-1), (1, reps))
    sin_t = np.tile(np.concatenate([-sin, sin], axis=-1), (1, reps))
    return jnp.asarray(cos_t, F32), jnp.asarray(sin_t, F32)


def kernel(x_prompt, x_sample, c, cache_diff_k, cache_diff_v, cache_win_k, cache_win_v, c_ctx, w_ada, b_ada, g_pre_mix, g_post_mix, g_pre_ffn, g_post_ffn, w_in, w_out, pool_w, pool_scale, diff_lq1, diff_lk1, diff_lq2, diff_lk2, diff_subln, win_sink, conv_dw, conv_dw_b, conv_ln_g, conv_ln_b, conv_pw, ffn_up, ffn_dw, ffn_dw_b, ffn_down):
    d = D_MODEL
    xp = x_prompt.reshape(BATCH * SEQ, d)
    xs = x_sample.reshape(DEC_BATCH * DEC_SEQ, d)

    cond8 = jnp.concatenate([c_ctx[None, :], c, jnp.zeros((8 - 1 - DEC_BATCH, d), F32)], axis=0)
    mod = _ada(cond8, w_ada, b_ada.reshape(DEPTH, 1, 6 * d))

    band_ctx, invcnt_ctx = _pool_tables(SEQ, 0, SEQ)
    band_dec, invcnt_dec = _pool_tables(TQ_DEC, HALO, DEC_SEQ)
    eye4 = jnp.eye(4, dtype=F32)
    P = dict(
        band_ctx=band_ctx, invcnt_ctx=invcnt_ctx, band_dec=band_dec, invcnt_dec=invcnt_dec,
        pool_wbd=jnp.einsum("lgcd,gh->lgchd", pool_w, eye4).reshape(DEPTH, 256, 256).astype(BF16),
        pool_scale=pool_scale.reshape(DEPTH, 1, 256),
        lam_p=jnp.stack([diff_lq1, diff_lk1, diff_lq2, diff_lk2], axis=1),
        subln=jnp.tile(diff_subln, (1, 4)).reshape(DEPTH, 1, 256),
        sink=win_sink.reshape(DEPTH * 4),
        conv_dw=conv_dw, conv_dw_b=conv_dw_b.reshape(DEPTH, 1, 256),
        conv_ln_g=conv_ln_g.reshape(DEPTH, 1, 256), conv_ln_b=conv_ln_b.reshape(DEPTH, 1, 256),
        conv_pw=conv_pw.astype(BF16), w_out=w_out.astype(BF16),
        g_post_mix=g_post_mix.reshape(DEPTH, 1, d), g_pre_ffn=g_pre_ffn.reshape(DEPTH, 1, d),
        ffn_up=ffn_up.astype(BF16), ffn_dw=ffn_dw, ffn_dw_b=ffn_dw_b.reshape(DEPTH, 1, 2 * D_FF),
        ffn_down=ffn_down.astype(BF16), g_post_ffn=g_post_ffn.reshape(DEPTH, 1, d),
    )
    w_in_b = w_in.astype(BF16)
    g_pre = g_pre_mix.reshape(DEPTH, 1, d)
    rope_tabs = _rope_tables(DEC_SEQ, DIFF_QK) + _rope_tables(DEC_SEQ, WIN_HD)
    caches = (cache_diff_k.reshape(DEC_BATCH, DEPTH, PAST_LEN, 256).astype(BF16),
              cache_diff_v.reshape(DEC_BATCH, DEPTH, PAST_LEN, 256).astype(BF16),
              cache_win_k.reshape(DEC_BATCH, DEPTH, PAST_LEN, 128).astype(BF16),
              cache_win_v.reshape(DEC_BATCH, DEPTH, PAST_LEN, 128).astype(BF16))

    kd, vd, kw, vw = [], [], [], []
    for l in range(DEPTH):
        mod_ctx = mod[l, 0:1].reshape(1, 1, 6 * d)
        mod_dec = mod[l, 1:1 + DEC_BATCH].reshape(DEC_BATCH, 1, 6 * d)

        (proj_c,) = _inproj(xp, mod_ctx, g_pre, w_in_b, l, None)
        kd.append(proj_c[:, 512:768].reshape(BATCH, SEQ, 4, 64))
        vd.append(proj_c[:, 768:1024].reshape(BATCH, SEQ, 4, 64))
        kw.append(proj_c[:, 1280:1408].reshape(BATCH, SEQ, 2, 64))
        vw.append(proj_c[:, 1408:1536].reshape(BATCH, SEQ, 2, 64))
        xp1, hp2 = _mixer(proj_c, xp, mod_ctx, P, l, dec=False)
        xp = _ffn(hp2, xp1, mod_ctx, P, l, SEQ)

        proj_d, kvb = _inproj(xs, mod_dec, g_pre, w_in_b, l, rope_tabs)
        xs1, hs2 = _mixer(proj_d, xs, mod_dec, P, l, dec=True, kvb=kvb, caches=caches)
        xs = _ffn(hs2, xs1, mod_dec, P, l, DEC_SEQ)

    return (xp.reshape(BATCH, SEQ, d), xs.reshape(DEC_BATCH, DEC_SEQ, d),
            jnp.stack(kd, axis=1), jnp.stack(vd, axis=1), jnp.stack(kw, axis=1), jnp.stack(vw, axis=1))
```

```python
import functools
import math

import jax
import jax.numpy as jnp
import numpy as np
from jax import lax
from jax.experimental import pallas as pl
from jax.experimental.pallas import tpu as pltpu

F32 = jnp.float32
BF16 = jnp.bfloat16

D_MODEL = 1024
BATCH = 16
SEQ = 256
DEPTH = 2
DEC_BATCH = 2
DEC_SEQ = 2048
PAST_LEN = 256
GRID_W = 64
POOL_WINDOWS = (2, 4, 8, 16)
DIFF_QK = 32
WIN_HD = 64
WINDOW = 128
CONV_K = 31
IN_WIDTH = 2048
D_FF = 2816
ROPE_BASE = 10000.0
EPS = 1e-6
LOG2E = 1.4426950408889634

LANES = 128
SUBLANES = 8
HALO = 16
FFN_HALO = 16
FFN_CHUNK = 256
FFN_DOWN_GROUP = 11
KV_WIDTH = 768
QL_WIDTH = 1280
VMEM_LIMIT = 56 * 1024 * 1024

TM_IN = 512
TQ_DEC = 256
TM_FFN = 1024


def _rms(x, g):
    return x * lax.rsqrt(jnp.mean(x * x, axis=-1, keepdims=True) + EPS) * g


def _dot(a, b):
    return jnp.dot(a, b, preferred_element_type=F32)


def _dot_nt(a, b):
    return lax.dot_general(a, b, (((1,), (1,)), ((), ())), preferred_element_type=F32)


def _lane_range(shape, lo, hi):
    lane = lax.broadcasted_iota(jnp.int32, shape, 1)
    return (lane >= lo) & (lane < hi)


def _swap_middle_heads(t0, t1):
    lo = _lane_range(t0.shape, 0, 64)
    return (jnp.where(lo, t0, pltpu.roll(t1, 64, 1)),
            jnp.where(lo, pltpu.roll(t0, 64, 1), t1))


def _modulated_norm(x, g, mod, shift_col, scale_col):
    d = D_MODEL
    return (_rms(x, g) * (1.0 + mod[:, scale_col * d:(scale_col + 1) * d])
            + mod[:, shift_col * d:(shift_col + 1) * d])


def _ada_kernel(c_ref, w_ref, b_ref, o_ref):
    c = c_ref[...]
    s = (c * jax.nn.sigmoid(c)).astype(BF16)
    o_ref[...] = _dot(s, w_ref[...].astype(BF16)) + b_ref[...]


def _ada(cond8, w_ada, b_ada):
    tn = 1024
    return pl.pallas_call(
        _ada_kernel,
        out_shape=jax.ShapeDtypeStruct((DEPTH, 8, 6 * D_MODEL), F32),
        grid=(DEPTH, 6 * D_MODEL // tn),
        in_specs=[
            pl.BlockSpec((8, D_MODEL), lambda l, j: (0, 0)),
            pl.BlockSpec((None, D_MODEL, tn), lambda l, j: (l, 0, j)),
            pl.BlockSpec((None, 1, tn), lambda l, j: (l, 0, j)),
        ],
        out_specs=pl.BlockSpec((None, 8, tn), lambda l, j: (l, 0, j)),
        compiler_params=pltpu.CompilerParams(
            dimension_semantics=("arbitrary", "arbitrary"), vmem_limit_bytes=VMEM_LIMIT),
        name="ada_mod",
    )(cond8, w_ada, b_ada)


def _rope_tile(v, cos, sin_signed, half):
    is_a = (lax.broadcasted_iota(jnp.int32, v.shape, 1) % (2 * half)) < half
    partner = jnp.where(is_a, pltpu.roll(v, LANES - half, 1), pltpu.roll(v, half, 1))
    return v * cos + partner * sin_signed


def _inproj_dec_kernel(x_ref, mod_ref, g_ref, w_ref, cd_ref, sd_ref, cw_ref, sw_ref, ql_ref, kv_ref):
    h = _modulated_norm(x_ref[...], g_ref[...], mod_ref[...], 0, 1)
    proj = _dot(h.astype(BF16), w_ref[...])
    tiles = [proj[:, i * LANES:(i + 1) * LANES] for i in range(IN_WIDTH // LANES)]
    cd, sd, cw, sw = cd_ref[...], sd_ref[...], cw_ref[...], sw_ref[...]
    for i in (2, 3, 4, 5):
        tiles[i] = _rope_tile(tiles[i], cd, sd, DIFF_QK // 2)
    for i in (8, 9, 10):
        tiles[i] = _rope_tile(tiles[i], cw, sw, WIN_HD // 2)
    tiles[8], tiles[9] = _swap_middle_heads(tiles[8], tiles[9])
    for j, i in enumerate((12, 13, 14, 15, 0, 1, 2, 3, 8, 9)):
        ql_ref[:, j * LANES:(j + 1) * LANES] = tiles[i]
    for j, i in enumerate((4, 5, 6, 7, 10, 11)):
        kv_ref[:, j * LANES:(j + 1) * LANES] = tiles[i].astype(BF16)


def _inproj_dec(x, mod, g, w_in_b, l, rope_tabs):
    rows = x.shape[0]
    tm = TM_IN
    tiles_per_b = DEC_SEQ // tm
    in_specs = [
        pl.BlockSpec((tm, D_MODEL), lambda i: (i, 0)),
        pl.BlockSpec((None, 1, 6 * D_MODEL), lambda i: (i // tiles_per_b, 0, 0)),
        _layer_spec((1, D_MODEL), l),
        _layer_spec((D_MODEL, IN_WIDTH), l),
    ] + [pl.BlockSpec((tm, LANES), lambda i: (i % tiles_per_b, 0)) for _ in rope_tabs]
    return pl.pallas_call(
        _inproj_dec_kernel,
        out_shape=[jax.ShapeDtypeStruct((rows, QL_WIDTH), F32),
                   jax.ShapeDtypeStruct((rows, KV_WIDTH), BF16)],
        grid=(rows // tm,),
        in_specs=in_specs,
        out_specs=[pl.BlockSpec((tm, QL_WIDTH), lambda i: (i, 0)),
                   pl.BlockSpec((tm, KV_WIDTH), lambda i: (i, 0))],
        compiler_params=pltpu.CompilerParams(
            dimension_semantics=("arbitrary",), vmem_limit_bytes=VMEM_LIMIT),
        name="inproj_dec",
    )(x, mod, g, w_in_b, *rope_tabs)


def _pool(xpad, u, invcnt, wbd, scale, tq):
    rows = xpad.shape[0]
    p = xpad
    sums = []
    for step, w in zip((1, 2, 4, 8), POOL_WINDOWS):
        p = p + pltpu.roll(p, step, 0)
        off = HALO + w // 2 - 1
        r = off % SUBLANES
        sh = p if r == 0 else pltpu.roll(p, rows - r, 0)
        sums.append(sh[off - r:off - r + tq])
    shape = u.shape
    s = jnp.where(_lane_range(shape, 0, 64), sums[0],
                  jnp.where(_lane_range(shape, 64, 128), sums[1],
                            jnp.where(_lane_range(shape, 128, 192), sums[2], sums[3])))
    pooled = s * invcnt - u
    return _dot(pooled.astype(BF16), wbd) * scale


def _conv_module(apad, gpad, dw_ref, dwb, lng, lnb, pw, tq):
    u = apad * jax.nn.sigmoid(gpad)
    rows = u.shape[0]
    acc = jnp.broadcast_to(dwb, (tq, u.shape[1]))
    for r in range(SUBLANES):
        ur = u if r == 0 else pltpu.roll(u, rows - r, 0)
        for a8 in range(4):
            k = SUBLANES * a8 + r - 1
            if 0 <= k < CONV_K:
                acc = acc + ur[SUBLANES * a8:SUBLANES * a8 + tq] * dw_ref[k:k + 1, :]
    mu = jnp.mean(acc, axis=-1, keepdims=True)
    xc = acc - mu
    var = jnp.mean(xc * xc, axis=-1, keepdims=True)
    y = xc * lax.rsqrt(var + EPS) * lng + lnb
    y = y * jax.nn.sigmoid(y)
    return _dot(y.astype(BF16), pw)


def _ones_outside(v, lo, hi):
    m = jnp.where(_lane_range((1, v.shape[1]), lo, hi), 1.0, 0.0).astype(BF16)
    return v * m + (1.0 - m)


def _diff_attn(dq, ks, vps, lam, gain, tq, group):
    dqs = dq * (DIFF_QK ** -0.5 * LOG2E)
    ytiles = [jnp.zeros((tq, LANES), F32), jnp.zeros((tq, LANES), F32)]
    for h0 in range(0, 4, group):
        qs = []
        for h in range(h0, h0 + group):
            for lo in (h * 64, h * 64 + DIFF_QK):
                qs.append(jnp.where(_lane_range(dqs.shape, lo, lo + DIFF_QK), dqs, 0.0))
        qg = jnp.concatenate(qs, axis=0).astype(BF16)
        ss = [_dot_nt(qg, k) for k in ks]
        mx = functools.reduce(jnp.maximum, [jnp.max(s, axis=-1, keepdims=True) for s in ss])
        es = [jnp.exp2(s - mx).astype(BF16) for s in ss]
        for h in range(h0, h0 + group):
            r0 = (h - h0) * 2 * tq
            o = None
            for e, vp in zip(es, vps(h)):
                t = _dot(e[r0:r0 + 2 * tq], vp)
                o = t if o is None else o + t
            tile = h // 2
            ot = o[:, tile * LANES:(tile + 1) * LANES]
            hlo = (h % 2) * 64
            den = jnp.max(jnp.where(_lane_range(ot.shape, 64 - hlo, 128 - hlo), ot, 0.0), axis=-1, keepdims=True)
            r = ot / den
            od = jnp.where(_lane_range((tq, LANES), hlo, hlo + 64), r[:tq] - lam * r[tq:], 0.0)
            ms = jnp.sum(od * od, axis=-1, keepdims=True) * (1.0 / 64.0)
            ytiles[tile] = ytiles[tile] + od * lax.rsqrt(ms + EPS)
    return jnp.concatenate(ytiles, axis=1) * gain


def _win_attn(wq, ks, masks, vs, sink_ref, sink_base, tq, group):
    wqs = wq * (WIN_HD ** -0.5 * LOG2E)
    t0, t1 = wqs[:, :LANES], wqs[:, LANES:]
    y0 = jnp.zeros((tq, LANES), F32)
    y1 = jnp.zeros((tq, LANES), F32)
    rows = 2 * tq * group
    row = lax.broadcasted_iota(jnp.int32, (rows, 1), 0)
    for j0 in range(0, 2, group):
        qs = []
        sk = jnp.zeros((rows, 1), F32)
        for j in range(j0, j0 + group):
            lm = _lane_range(t0.shape, j * 64, j * 64 + 64)
            qs += [jnp.where(lm, t0, 0.0), jnp.where(lm, t1, 0.0)]
            for g in range(2):
                r0 = ((j - j0) * 2 + g) * tq
                sk = jnp.where((row >= r0) & (row < r0 + tq), sink_ref[sink_base + 2 * j + g], sk)
        sk = sk * LOG2E
        qg = jnp.concatenate(qs, axis=0).astype(BF16)
        ss = []
        for k, m in zip(ks, masks):
            s = _dot_nt(qg, k)
            if m is not None:
                s = jnp.where(m, s, -1e30)
            ss.append(s)
        mx = functools.reduce(jnp.maximum, [jnp.max(s, axis=-1, keepdims=True) for s in ss])
        mx = jnp.maximum(mx, sk)
        es = [jnp.exp2(s - mx).astype(BF16) for s in ss]
        esink = jnp.exp2(sk - mx)
        for j in range(j0, j0 + group):
            r0 = (j - j0) * 2 * tq
            o = None
            for e, v in zip(es, vs):
                t = _dot(e[r0:r0 + 2 * tq], _ones_outside(v, j * 64, j * 64 + 64))
                o = t if o is None else o + t
            other = _lane_range(o.shape, 64 - j * 64, 128 - j * 64)
            den = jnp.max(jnp.where(other, o, 0.0), axis=-1, keepdims=True)
            r = o / (den + esink[r0:r0 + 2 * tq])
            lm = _lane_range((tq, LANES), j * 64, j * 64 + 64)
            y0 = y0 + jnp.where(lm, r[:tq], 0.0)
            y1 = y1 + jnp.where(lm, r[tq:], 0.0)
    return _swap_middle_heads(y0, y1)


def _mixer_kernel(*refs, dec, l, tq):
    if dec:
        (ql_ref, conv_p_ref, conv_n_ref, pool_p_ref, pool_n_ref, kv_ref,
         cdk_ref, cdv_ref, cwk_ref, cwv_ref, x_ref, mod_ref, *rest) = refs
    else:
        (x_ref, mod_ref, gin_ref, win_ref, *rest) = refs
    (invcnt_ref, wbd_ref, pscale_ref, lamp_ref, subln_ref, sink_ref,
     dw_ref, dwb_ref, lng_ref, lnb_ref, pw_ref, wout_ref, gpost_ref, gpre_ref, *outs) = rest
    d = D_MODEL
    mod = mod_ref[...]
    x = x_ref[...]

    lam_init = 0.8 - 0.6 * math.exp(-0.3 * l)
    lp = lamp_ref[...]
    lam = (jnp.exp(jnp.sum(lp[0:1] * lp[1:2], axis=-1, keepdims=True))
           - jnp.exp(jnp.sum(lp[2:3] * lp[3:4], axis=-1, keepdims=True)) + lam_init)
    gain = subln_ref[...] * (1.0 - lam_init)

    if dec:
        x1_ref, h2_ref, vp_ref = outs
        ca, cg = ql_ref[:, 0:256], ql_ref[:, 256:512]
        u_pool, dq, wq = ql_ref[:, 512:768], ql_ref[:, 768:1024], ql_ref[:, 1024:1280]
        q = pl.program_id(1)
        nq = pl.num_programs(1)
        pv = (q > 0).astype(F32)
        nv = (q < nq - 1).astype(F32)
        pool_pad = jnp.concatenate([pool_p_ref[...] * pv, u_pool, pool_n_ref[...] * nv], axis=0)
        cp = conv_p_ref[...] * pv
        cn = conv_n_ref[...] * nv
        apad = jnp.concatenate([cp[:, :256], ca, cn[:, :256]], axis=0)
        gpad = jnp.concatenate([cp[:, 256:], cg, cn[:, 256:]], axis=0)

        @pl.when(q == 0)
        def _():
            for h in range(4):
                vp_ref[h, 0:PAST_LEN, :] = _ones_outside(cdv_ref[...], h * 64, h * 64 + 64)
                vp_ref[h, PAST_LEN:, :] = _ones_outside(kv_ref[:, 256:512], h * 64, h * 64 + 64)

        dks = [cdk_ref[...], kv_ref[:, 0:256]]
        vps = lambda h: [vp_ref[h, 0:PAST_LEN, :], vp_ref[h, PAST_LEN:, :]]
        band = tq + 2 * WINDOW
        start = pl.multiple_of(jnp.clip(q * tq - WINDOW, 0, DEC_SEQ - band), WINDOW)
        wkb = kv_ref[pl.ds(start, band), 512:640]
        wvb = kv_ref[pl.ds(start, band), 640:768]
        qpos = q * tq + (lax.broadcasted_iota(jnp.int32, (2 * tq, band), 0) & (tq - 1))
        kpos = start + lax.broadcasted_iota(jnp.int32, (2 * tq, band), 1)
        wmask = jnp.abs(qpos - kpos) <= WINDOW
        wks, wvs, wms = [wkb, cwk_ref[...]], [wvb, cwv_ref[...]], [wmask, None]
    else:
        x1_ref, h2_ref, kd_ref, vd_ref, kw_ref, vw_ref = outs
        h = _modulated_norm(x, gin_ref[...], mod, 0, 1)
        proj = _dot(h.astype(BF16), win_ref[...])
        u_pool, dq = proj[:, 0:256], proj[:, 256:512]
        dk, dv = proj[:, 512:768], proj[:, 768:1024]
        wk, wv = proj[:, 1280:1408], proj[:, 1408:1536]
        ca, cg = proj[:, 1536:1792], proj[:, 1792:2048]
        wq = jnp.concatenate(_swap_middle_heads(proj[:, 1024:1152], proj[:, 1152:1280]), axis=1)
        kd_ref[...] = dk.T
        vd_ref[...] = dv.T
        kw_ref[...] = wk.T
        vw_ref[...] = wv.T
        zpad = jnp.zeros((HALO, 256), F32)
        pool_pad = jnp.concatenate([zpad, u_pool, zpad], axis=0)
        apad = jnp.concatenate([zpad, ca, zpad], axis=0)
        gpad = jnp.concatenate([zpad, cg, zpad], axis=0)
        dks = [dk.astype(BF16)]
        dvb = dv.astype(BF16)
        vps = lambda h: [_ones_outside(dvb, h * 64, h * 64 + 64)]
        wks, wvs, wms = [wk.astype(BF16)], [wv.astype(BF16)], [None]

    y_pool = _pool(pool_pad, u_pool, invcnt_ref[...], wbd_ref[...], pscale_ref[...], tq)
    y_diff = _diff_attn(dq, dks, vps, lam, gain, tq, group=1 if dec else 4)
    yw0, yw1 = _win_attn(wq, wks, wms, wvs, sink_ref, 4 * l, tq, group=1 if dec else 2)
    y_conv = _conv_module(apad, gpad, dw_ref, dwb_ref[...], lng_ref[...], lnb_ref[...], pw_ref[...], tq)

    o = (_dot(y_pool.astype(BF16), wout_ref[0:256, :])
         + _dot(y_diff.astype(BF16), wout_ref[256:512, :])
         + _dot(yw0.astype(BF16), wout_ref[512:640, :])
         + _dot(yw1.astype(BF16), wout_ref[640:768, :])
         + _dot(y_conv.astype(BF16), wout_ref[768:1024, :]))
    x1 = x + mod[:, 2 * d:3 * d] * _rms(o, gpost_ref[...])
    x1_ref[...] = x1
    h2_ref[...] = _modulated_norm(x1, gpre_ref[...], mod, 3, 4).astype(BF16)


def _layer_spec(shape, l):
    nz = len(shape)
    return pl.BlockSpec((None,) + tuple(shape), lambda *_: (l,) + (0,) * nz,
                        pipeline_mode=pl.Buffered(1))


def _mixer_param_specs(l):
    return [
        _layer_spec((256, 256), l),
        _layer_spec((1, 256), l),
        _layer_spec((4, DIFF_QK), l),
        _layer_spec((1, 256), l),
        pl.BlockSpec(memory_space=pltpu.SMEM),
        _layer_spec((CONV_K, 256), l),
        _layer_spec((1, 256), l),
        _layer_spec((1, 256), l),
        _layer_spec((1, 256), l),
        _layer_spec((256, 256), l),
        _layer_spec((D_MODEL, D_MODEL), l),
        _layer_spec((1, D_MODEL), l),
        _layer_spec((1, D_MODEL), l),
    ]


def _mixer_param_args(P):
    return [P["pool_wbd"], P["pool_scale"], P["lam_p"], P["subln"], P["sink"],
            P["conv_dw"], P["conv_dw_b"], P["conv_ln_g"], P["conv_ln_b"], P["conv_pw"], P["w_out"],
            P["g_post_mix"], P["g_pre_ffn"]]


def _mixer_ctx(x, mod, P, l):
    rows = x.shape[0]
    tq = SEQ
    row_map = lambda i: (i, 0)
    col_map = lambda i: (0, i)
    in_specs = [
        pl.BlockSpec((tq, D_MODEL), row_map),
        pl.BlockSpec((None, 1, 6 * D_MODEL), lambda i: (0, 0, 0)),
        _layer_spec((1, D_MODEL), l),
        _layer_spec((D_MODEL, IN_WIDTH), l),
        pl.BlockSpec((tq, 256), lambda i: (0, 0)),
    ] + _mixer_param_specs(l)
    args = [x, mod, P["g_pre_mix"], P["w_in"], P["invcnt_ctx"]] + _mixer_param_args(P)
    return pl.pallas_call(
        functools.partial(_mixer_kernel, dec=False, l=l, tq=tq),
        out_shape=[jax.ShapeDtypeStruct((rows, D_MODEL), F32),
                   jax.ShapeDtypeStruct((rows, D_MODEL), BF16),
                   jax.ShapeDtypeStruct((256, rows), F32),
                   jax.ShapeDtypeStruct((256, rows), F32),
                   jax.ShapeDtypeStruct((128, rows), F32),
                   jax.ShapeDtypeStruct((128, rows), F32)],
        grid=(rows // tq,),
        in_specs=in_specs,
        out_specs=[pl.BlockSpec((tq, D_MODEL), row_map), pl.BlockSpec((tq, D_MODEL), row_map),
                   pl.BlockSpec((256, tq), col_map), pl.BlockSpec((256, tq), col_map),
                   pl.BlockSpec((128, tq), col_map), pl.BlockSpec((128, tq), col_map)],
        compiler_params=pltpu.CompilerParams(
            dimension_semantics=("arbitrary",), vmem_limit_bytes=VMEM_LIMIT),
        name="mixer_ctx",
    )(*args)


def _mixer_dec(ql, kvb, caches, x, mod, P, l):
    rows = x.shape[0]
    tq = TQ_DEC
    nq = DEC_SEQ // tq
    hb = tq // HALO
    nhb = rows // HALO
    row_map = lambda b, q: (b * nq + q, 0)

    def prev_map(col):
        return lambda b, q: (jnp.maximum((b * nq + q) * hb - 1, 0), col)

    def next_map(col):
        return lambda b, q: (jnp.minimum((b * nq + q + 1) * hb, nhb - 1), col)

    cache_map = lambda b, q: (b, l, 0, 0)
    in_specs = [
        pl.BlockSpec((tq, QL_WIDTH), row_map),
        pl.BlockSpec((HALO, 512), prev_map(0)),
        pl.BlockSpec((HALO, 512), next_map(0)),
        pl.BlockSpec((HALO, 256), prev_map(2)),
        pl.BlockSpec((HALO, 256), next_map(2)),
        pl.BlockSpec((DEC_SEQ, KV_WIDTH), lambda b, q: (b, 0)),
        pl.BlockSpec((None, None, PAST_LEN, 256), cache_map),
        pl.BlockSpec((None, None, PAST_LEN, 256), cache_map),
        pl.BlockSpec((None, None, PAST_LEN, 128), cache_map),
        pl.BlockSpec((None, None, PAST_LEN, 128), cache_map),
        pl.BlockSpec((tq, D_MODEL), row_map),
        pl.BlockSpec((None, 1, 6 * D_MODEL), lambda b, q: (b, 0, 0)),
        pl.BlockSpec((tq, 256), lambda b, q: (q, 0)),
    ] + _mixer_param_specs(l)
    args = [ql, ql, ql, ql, ql, kvb, *caches, x, mod, P["invcnt_dec"]] + _mixer_param_args(P)
    return pl.pallas_call(
        functools.partial(_mixer_kernel, dec=True, l=l, tq=tq),
        out_shape=[jax.ShapeDtypeStruct((rows, D_MODEL), F32),
                   jax.ShapeDtypeStruct((rows, D_MODEL), BF16)],
        grid=(DEC_BATCH, nq),
        in_specs=in_specs,
        out_specs=[pl.BlockSpec((tq, D_MODEL), row_map), pl.BlockSpec((tq, D_MODEL), row_map)],
        scratch_shapes=[pltpu.VMEM((4, PAST_LEN + DEC_SEQ, 256), BF16)],
        compiler_params=pltpu.CompilerParams(
            dimension_semantics=("arbitrary", "arbitrary"), vmem_limit_bytes=VMEM_LIMIT),
        name="mixer_dec",
    )(*args)


def _ffn_kernel(h_ref, hp_ref, hn_ref, x_ref, mod_ref, wup_ref, dw_ref, dwb_ref, wd_ref, gpost_ref,
                o_ref, act_ref, *, tm, seq_len):
    i = pl.program_id(0)
    hx = jnp.concatenate([hp_ref[...], h_ref[...], hn_ref[...]], axis=0)
    pos = (i * tm + lax.broadcasted_iota(jnp.int32, (tm, 1), 0)) % seq_len
    has_left = pos != 0
    has_right = pos != seq_len - 1
    n_chunks = D_FF // FFN_CHUNK
    o = None
    done = 0
    for j in range(n_chunks):
        ys = []
        for part in range(2):
            c0 = part * D_FF + j * FFN_CHUNK
            u = _dot(hx, wup_ref[:, c0:c0 + FFN_CHUNK])
            w = dw_ref[:, c0:c0 + FFN_CHUNK]
            ys.append(jnp.where(has_left, u[FFN_HALO - 1:FFN_HALO - 1 + tm], 0.0) * w[0:1]
                      + u[FFN_HALO:FFN_HALO + tm] * w[1:2]
                      + jnp.where(has_right, u[FFN_HALO + 1:FFN_HALO + 1 + tm], 0.0) * w[2:3]
                      + dwb_ref[:, c0:c0 + FFN_CHUNK])
        gate, up = ys
        act_ref[:, j * FFN_CHUNK:(j + 1) * FFN_CHUNK] = (gate * jax.nn.sigmoid(gate) * up).astype(BF16)
        if (j + 1) % FFN_DOWN_GROUP == 0 or j == n_chunks - 1:
            k0, k1 = done * FFN_CHUNK, (j + 1) * FFN_CHUNK
            t = _dot(act_ref[:, k0:k1], wd_ref[k0:k1, :])
            o = t if o is None else o + t
            done = j + 1
    g2 = mod_ref[:, 5 * D_MODEL:6 * D_MODEL]
    o_ref[...] = x_ref[...] + g2 * _rms(o, gpost_ref[...])


def _ffn(h2, x1, mod, P, l, seq_len):
    rows = x1.shape[0]
    tm = TM_FFN
    nb = mod.shape[0]
    tiles_per_b = rows // nb // tm
    hb = tm // FFN_HALO
    nhb = rows // FFN_HALO
    return pl.pallas_call(
        functools.partial(_ffn_kernel, tm=tm, seq_len=seq_len),
        out_shape=jax.ShapeDtypeStruct((rows, D_MODEL), F32),
        grid=(rows // tm,),
        in_specs=[
            pl.BlockSpec((tm, D_MODEL), lambda i: (i, 0)),
            pl.BlockSpec((FFN_HALO, D_MODEL), lambda i: (jnp.maximum(i * hb - 1, 0), 0)),
            pl.BlockSpec((FFN_HALO, D_MODEL), lambda i: (jnp.minimum((i + 1) * hb, nhb - 1), 0)),
            pl.BlockSpec((tm, D_MODEL), lambda i: (i, 0)),
            pl.BlockSpec((None, 1, 6 * D_MODEL), lambda i: (i // tiles_per_b, 0, 0)),
            _layer_spec((D_MODEL, 2 * D_FF), l),
            _layer_spec((3, 2 * D_FF), l),
            _layer_spec((1, 2 * D_FF), l),
            _layer_spec((D_FF, D_MODEL), l),
            _layer_spec((1, D_MODEL), l),
        ],
        out_specs=pl.BlockSpec((tm, D_MODEL), lambda i: (i, 0)),
        scratch_shapes=[pltpu.VMEM((tm, D_FF), BF16)],
        compiler_params=pltpu.CompilerParams(
            dimension_semantics=("arbitrary",), vmem_limit_bytes=VMEM_LIMIT),
        name="ffn",
    )(h2, h2, h2, x1, mod, P["ffn_up"], P["ffn_dw"], P["ffn_dw_b"], P["ffn_down"], P["g_post_ffn"])


def _pool_inv_counts(seq_len):
    inv = np.zeros((seq_len, 256), np.float32)
    pos = np.arange(seq_len)
    for g, w in enumerate(POOL_WINDOWS):
        cnt = np.clip(pos - w // 2 + w, 0, seq_len) - np.clip(pos - w // 2, 0, seq_len)
        inv[:, g * 64:(g + 1) * 64] = (1.0 / cnt.astype(np.float64))[:, None].astype(np.float32)
    return jnp.asarray(inv, F32)


def _rope_tables(seq_len, dim):
    rows = seq_len // GRID_W
    r = np.repeat(np.arange(rows), GRID_W).astype(np.float32)
    col = np.tile(np.arange(GRID_W), rows).astype(np.float32)
    n = dim // 4
    inv = (ROPE_BASE ** (-np.arange(n) / n)).astype(np.float32)
    ang = np.concatenate([r[:, None] * inv[None], col[:, None] * inv[None]], axis=-1)
    cos, sin = np.cos(ang), np.sin(ang)
    reps = LANES // dim
    cos_t = np.tile(np.concatenate([cos, cos], axis=-1), (1, reps))
    sin_t = np.tile(np.concatenate([-sin, sin], axis=-1), (1, reps))
    return jnp.asarray(cos_t, F32), jnp.asarray(sin_t, F32)


def _cache_from_feature_major(per_layer, heads):
    t = jnp.stack([a.reshape(heads, 64, BATCH, SEQ) for a in per_layer], axis=0)
    return jnp.transpose(t, (3, 0, 4, 1, 2))


def kernel(x_prompt, x_sample, c, cache_diff_k, cache_diff_v, cache_win_k, cache_win_v, c_ctx, w_ada, b_ada, g_pre_mix, g_post_mix, g_pre_ffn, g_post_ffn, w_in, w_out, pool_w, pool_scale, diff_lq1, diff_lk1, diff_lq2, diff_lk2, diff_subln, win_sink, conv_dw, conv_dw_b, conv_ln_g, conv_ln_b, conv_pw, ffn_up, ffn_dw, ffn_dw_b, ffn_down):
    d = D_MODEL
    xp = x_prompt.reshape(BATCH * SEQ, d)
    xs = x_sample.reshape(DEC_BATCH * DEC_SEQ, d)

    cond8 = jnp.concatenate([c_ctx[None, :], c, jnp.zeros((8 - 1 - DEC_BATCH, d), F32)], axis=0)
    mod = _ada(cond8, w_ada, b_ada.reshape(DEPTH, 1, 6 * d))

    eye4 = jnp.eye(4, dtype=F32)
    P = dict(
        invcnt_ctx=_pool_inv_counts(SEQ), invcnt_dec=_pool_inv_counts(DEC_SEQ),
        g_pre_mix=g_pre_mix.reshape(DEPTH, 1, d), w_in=w_in.astype(BF16),
        pool_wbd=jnp.einsum("lgcd,gh->lgchd", pool_w, eye4).reshape(DEPTH, 256, 256).astype(BF16),
        pool_scale=pool_scale.reshape(DEPTH, 1, 256),
        lam_p=jnp.stack([diff_lq1, diff_lk1, diff_lq2, diff_lk2], axis=1),
        subln=jnp.tile(diff_subln, (1, 4)).reshape(DEPTH, 1, 256),
        sink=win_sink.reshape(DEPTH * 4),
        conv_dw=conv_dw, conv_dw_b=conv_dw_b.reshape(DEPTH, 1, 256),
        conv_ln_g=conv_ln_g.reshape(DEPTH, 1, 256), conv_ln_b=conv_ln_b.reshape(DEPTH, 1, 256),
        conv_pw=conv_pw.astype(BF16), w_out=w_out.astype(BF16),
        g_post_mix=g_post_mix.reshape(DEPTH, 1, d), g_pre_ffn=g_pre_ffn.reshape(DEPTH, 1, d),
        ffn_up=ffn_up.astype(BF16), ffn_dw=ffn_dw, ffn_dw_b=ffn_dw_b.reshape(DEPTH, 1, 2 * D_FF),
        ffn_down=ffn_down.astype(BF16), g_post_ffn=g_post_ffn.reshape(DEPTH, 1, d),
    )
    rope_tabs = _rope_tables(DEC_SEQ, DIFF_QK) + _rope_tables(DEC_SEQ, WIN_HD)
    caches = (cache_diff_k.reshape(DEC_BATCH, DEPTH, PAST_LEN, 256).astype(BF16),
              cache_diff_v.reshape(DEC_BATCH, DEPTH, PAST_LEN, 256).astype(BF16),
              cache_win_k.reshape(DEC_BATCH, DEPTH, PAST_LEN, 128).astype(BF16),
              cache_win_v.reshape(DEC_BATCH, DEPTH, PAST_LEN, 128).astype(BF16))

    kd, vd, kw, vw = [], [], [], []
    for l in range(DEPTH):
        mod_ctx = mod[l, 0:1].reshape(1, 1, 6 * d)
        mod_dec = mod[l, 1:1 + DEC_BATCH].reshape(DEC_BATCH, 1, 6 * d)

        xp1, hp2, kd_l, vd_l, kw_l, vw_l = _mixer_ctx(xp, mod_ctx, P, l)
        kd.append(kd_l)
        vd.append(vd_l)
        kw.append(kw_l)
        vw.append(vw_l)
        xp = _ffn(hp2, xp1, mod_ctx, P, l, SEQ)

        ql, kvb = _inproj_dec(xs, mod_dec, P["g_pre_mix"], P["w_in"], l, rope_tabs)
        xs1, hs2 = _mixer_dec(ql, kvb, caches, xs, mod_dec, P, l)
        xs = _ffn(hs2, xs1, mod_dec, P, l, DEC_SEQ)

    return (xp.reshape(BATCH, SEQ, d), xs.reshape(DEC_BATCH, DEC_SEQ, d),
            _cache_from_feature_major(kd, 4), _cache_from_feature_major(vd, 4),
            _cache_from_feature_major(kw, 2), _cache_from_feature_major(vw, 2))
```

```python
import functools
import math

import jax
import jax.numpy as jnp
import numpy as np
from jax import lax
from jax.experimental import pallas as pl
from jax.experimental.pallas import tpu as pltpu

F32 = jnp.float32
BF16 = jnp.bfloat16

D_MODEL = 1024
BATCH = 16
SEQ = 256
DEPTH = 2
DEC_BATCH = 2
DEC_SEQ = 2048
PAST_LEN = 256
GRID_W = 64
POOL_WINDOWS = (2, 4, 8, 16)
DIFF_QK = 32
WIN_HD = 64
WINDOW = 128
CONV_K = 31
IN_WIDTH = 2048
D_FF = 2816
ROPE_BASE = 10000.0
EPS = 1e-6
LOG2E = 1.4426950408889634

LANES = 128
SUBLANES = 8
HALO = 16
FFN_HALO = 16
FFN_CHUNK = 256
KV_WIDTH = 768
QL_WIDTH = 1280
VMEM_LIMIT = 56 * 1024 * 1024

TM_IN = 512
CTX_SEQS_PER_STEP = 2
TQ_DEC = 256
TM_FFN = 1024


def _rms(x, g):
    return x * lax.rsqrt(jnp.mean(x * x, axis=-1, keepdims=True) + EPS) * g


def _dot(a, b):
    return jnp.dot(a, b, preferred_element_type=F32)


def _dot_nt(a, b):
    return lax.dot_general(a, b, (((1,), (1,)), ((), ())), preferred_element_type=F32)


def _lane_range(shape, lo, hi):
    lane = lax.broadcasted_iota(jnp.int32, shape, 1)
    return (lane >= lo) & (lane < hi)


def _swap_middle_heads(t0, t1):
    lo = _lane_range(t0.shape, 0, 64)
    return (jnp.where(lo, t0, pltpu.roll(t1, 64, 1)),
            jnp.where(lo, pltpu.roll(t0, 64, 1), t1))


def _modulated_norm(x, g, mod, shift_col, scale_col):
    d = D_MODEL
    return (_rms(x, g) * (1.0 + mod[:, scale_col * d:(scale_col + 1) * d])
            + mod[:, shift_col * d:(shift_col + 1) * d])


def _layer_spec(shape, l):
    nz = len(shape)
    return pl.BlockSpec((None,) + tuple(shape), lambda *_: (l,) + (0,) * nz,
                        pipeline_mode=pl.Buffered(1))


def _rows_spec(width):
    return pl.BlockSpec((DEPTH, width), lambda *_: (0, 0), pipeline_mode=pl.Buffered(1))


def _mod_spec(l):
    return pl.BlockSpec((None, 8, 6 * D_MODEL), lambda *_: (l, 0, 0), pipeline_mode=pl.Buffered(1))


def _ada_kernel(c_ref, w_ref, b_ref, o_ref):
    c = c_ref[...]
    s = (c * jax.nn.sigmoid(c)).astype(BF16)
    o_ref[...] = _dot(s, w_ref[...].astype(BF16)) + b_ref[pl.ds(pl.program_id(0), 1), :]


def _ada(cond8, w_ada, b_ada):
    tn = 1024
    return pl.pallas_call(
        _ada_kernel,
        out_shape=jax.ShapeDtypeStruct((DEPTH, 8, 6 * D_MODEL), F32),
        grid=(DEPTH, 6 * D_MODEL // tn),
        in_specs=[
            pl.BlockSpec((8, D_MODEL), lambda l, j: (0, 0)),
            pl.BlockSpec((None, D_MODEL, tn), lambda l, j: (l, 0, j)),
            pl.BlockSpec((DEPTH, tn), lambda l, j: (0, j)),
        ],
        out_specs=pl.BlockSpec((None, 8, tn), lambda l, j: (l, 0, j)),
        compiler_params=pltpu.CompilerParams(
            dimension_semantics=("arbitrary", "arbitrary"), vmem_limit_bytes=VMEM_LIMIT),
        name="ada_mod",
    )(cond8, w_ada, b_ada)


def _rope_tile(v, cos, sin_signed, half):
    is_a = (lax.broadcasted_iota(jnp.int32, v.shape, 1) % (2 * half)) < half
    partner = jnp.where(is_a, pltpu.roll(v, LANES - half, 1), pltpu.roll(v, half, 1))
    return v * cos + partner * sin_signed


def _inproj_dec_kernel(x_ref, mod_ref, g_ref, w_ref, cd_ref, sd_ref, cw_ref, sw_ref, ql_ref, kv_ref,
                       *, l, tiles_per_b):
    mod = mod_ref[pl.ds(1 + pl.program_id(0) // tiles_per_b, 1), :]
    h = _modulated_norm(x_ref[...], g_ref[l:l + 1, :], mod, 0, 1)
    proj = _dot(h.astype(BF16), w_ref[...])
    tiles = [proj[:, i * LANES:(i + 1) * LANES] for i in range(IN_WIDTH // LANES)]
    cd, sd, cw, sw = cd_ref[...], sd_ref[...], cw_ref[...], sw_ref[...]
    for i in (2, 3, 4, 5):
        tiles[i] = _rope_tile(tiles[i], cd, sd, DIFF_QK // 2)
    for i in (8, 9, 10):
        tiles[i] = _rope_tile(tiles[i], cw, sw, WIN_HD // 2)
    tiles[8], tiles[9] = _swap_middle_heads(tiles[8], tiles[9])
    for j, i in enumerate((12, 13, 14, 15, 0, 1, 2, 3, 8, 9)):
        ql_ref[:, j * LANES:(j + 1) * LANES] = tiles[i]
    for j, i in enumerate((4, 5, 6, 7, 10, 11)):
        kv_ref[:, j * LANES:(j + 1) * LANES] = tiles[i].astype(BF16)


def _inproj_dec(x, mod, g, w_in_b, l, rope_tabs):
    rows = x.shape[0]
    tm = TM_IN
    tiles_per_b = DEC_SEQ // tm
    in_specs = [
        pl.BlockSpec((tm, D_MODEL), lambda i: (i, 0)),
        _mod_spec(l),
        _rows_spec(D_MODEL),
        _layer_spec((D_MODEL, IN_WIDTH), l),
    ] + [pl.BlockSpec((tm, LANES), lambda i: (i % tiles_per_b, 0)) for _ in rope_tabs]
    return pl.pallas_call(
        functools.partial(_inproj_dec_kernel, l=l, tiles_per_b=tiles_per_b),
        out_shape=[jax.ShapeDtypeStruct((rows, QL_WIDTH), F32),
                   jax.ShapeDtypeStruct((rows, KV_WIDTH), BF16)],
        grid=(rows // tm,),
        in_specs=in_specs,
        out_specs=[pl.BlockSpec((tm, QL_WIDTH), lambda i: (i, 0)),
                   pl.BlockSpec((tm, KV_WIDTH), lambda i: (i, 0))],
        compiler_params=pltpu.CompilerParams(
            dimension_semantics=("arbitrary",), vmem_limit_bytes=VMEM_LIMIT),
        name="inproj_dec",
    )(x, mod, g, w_in_b, *rope_tabs)


def _pool(xpad, u, invcnt, tq):
    rows = xpad.shape[0]
    p = xpad
    sums = []
    for step, w in zip((1, 2, 4, 8), POOL_WINDOWS):
        p = p + pltpu.roll(p, step, 0)
        off = HALO + w // 2 - 1
        r = off % SUBLANES
        sh = p if r == 0 else pltpu.roll(p, rows - r, 0)
        sums.append(sh[off - r:off - r + tq])
    shape = u.shape
    s = jnp.where(_lane_range(shape, 0, 64), sums[0],
                  jnp.where(_lane_range(shape, 64, 128), sums[1],
                            jnp.where(_lane_range(shape, 128, 192), sums[2], sums[3])))
    return s * invcnt - u


def _conv_module(apad, gpad, dw_ref, dwb, lng, lnb, tq):
    u = apad * jax.nn.sigmoid(gpad)
    rows = u.shape[0]
    acc = jnp.broadcast_to(dwb, (tq, u.shape[1]))
    for r in range(SUBLANES):
        ur = u if r == 0 else pltpu.roll(u, rows - r, 0)
        for a8 in range(4):
            k = SUBLANES * a8 + r - 1
            if 0 <= k < CONV_K:
                acc = acc + ur[SUBLANES * a8:SUBLANES * a8 + tq] * dw_ref[k:k + 1, :]
    mu = jnp.mean(acc, axis=-1, keepdims=True)
    xc = acc - mu
    var = jnp.mean(xc * xc, axis=-1, keepdims=True)
    y = xc * lax.rsqrt(var + EPS) * lng + lnb
    return y * jax.nn.sigmoid(y)


def _ones_outside(v, lo, hi):
    m = jnp.where(_lane_range((1, v.shape[1]), lo, hi), 1.0, 0.0).astype(BF16)
    return v * m + (1.0 - m)


def _diff_attn(dq, ks, vps, lam, gain, tq, group):
    dqs = dq * (DIFF_QK ** -0.5 * LOG2E)
    ytiles = [jnp.zeros((tq, LANES), F32), jnp.zeros((tq, LANES), F32)]
    for h0 in range(0, 4, group):
        qs = []
        for h in range(h0, h0 + group):
            for lo in (h * 64, h * 64 + DIFF_QK):
                qs.append(jnp.where(_lane_range(dqs.shape, lo, lo + DIFF_QK), dqs, 0.0))
        qg = jnp.concatenate(qs, axis=0).astype(BF16)
        ss = [_dot_nt(qg, k) for k in ks]
        mx = functools.reduce(jnp.maximum, [jnp.max(s, axis=-1, keepdims=True) for s in ss])
        es = [jnp.exp2(s - mx).astype(BF16) for s in ss]
        for h in range(h0, h0 + group):
            r0 = (h - h0) * 2 * tq
            o = None
            for e, vp in zip(es, vps(h)):
                t = _dot(e[r0:r0 + 2 * tq], vp)
                o = t if o is None else o + t
            tile = h // 2
            ot = o[:, tile * LANES:(tile + 1) * LANES]
            hlo = (h % 2) * 64
            den = jnp.max(jnp.where(_lane_range(ot.shape, 64 - hlo, 128 - hlo), ot, 0.0), axis=-1, keepdims=True)
            r = ot / den
            od = jnp.where(_lane_range((tq, LANES), hlo, hlo + 64), r[:tq] - lam * r[tq:], 0.0)
            ms = jnp.sum(od * od, axis=-1, keepdims=True) * (1.0 / 64.0)
            ytiles[tile] = ytiles[tile] + od * lax.rsqrt(ms + EPS)
    return jnp.concatenate(ytiles, axis=1) * gain


def _win_attn(wq, ks, masks, vs, sink_ref, sink_base, tq, group):
    wqs = wq * (WIN_HD ** -0.5 * LOG2E)
    t0, t1 = wqs[:, :LANES], wqs[:, LANES:]
    y0 = jnp.zeros((tq, LANES), F32)
    y1 = jnp.zeros((tq, LANES), F32)
    rows = 2 * tq * group
    row = lax.broadcasted_iota(jnp.int32, (rows, 1), 0)
    for j0 in range(0, 2, group):
        qs = []
        sk = jnp.zeros((rows, 1), F32)
        for j in range(j0, j0 + group):
            lm = _lane_range(t0.shape, j * 64, j * 64 + 64)
            qs += [jnp.where(lm, t0, 0.0), jnp.where(lm, t1, 0.0)]
            for g in range(2):
                r0 = ((j - j0) * 2 + g) * tq
                sk = jnp.where((row >= r0) & (row < r0 + tq), sink_ref[sink_base + 2 * j + g], sk)
        sk = sk * LOG2E
        qg = jnp.concatenate(qs, axis=0).astype(BF16)
        ss = []
        for k, m in zip(ks, masks):
            s = _dot_nt(qg, k)
            if m is not None:
                s = jnp.where(m, s, -1e30)
            ss.append(s)
        mx = functools.reduce(jnp.maximum, [jnp.max(s, axis=-1, keepdims=True) for s in ss])
        mx = jnp.maximum(mx, sk)
        es = [jnp.exp2(s - mx).astype(BF16) for s in ss]
        esink = jnp.exp2(sk - mx)
        for j in range(j0, j0 + group):
            r0 = (j - j0) * 2 * tq
            o = None
            for e, v in zip(es, vs):
                t = _dot(e[r0:r0 + 2 * tq], _ones_outside(v, j * 64, j * 64 + 64))
                o = t if o is None else o + t
            other = _lane_range(o.shape, 64 - j * 64, 128 - j * 64)
            den = jnp.max(jnp.where(other, o, 0.0), axis=-1, keepdims=True)
            r = o / (den + esink[r0:r0 + 2 * tq])
            lm = _lane_range((tq, LANES), j * 64, j * 64 + 64)
            y0 = y0 + jnp.where(lm, r[:tq], 0.0)
            y1 = y1 + jnp.where(lm, r[tq:], 0.0)
    return _swap_middle_heads(y0, y1)


def _mixer_kernel(*refs, dec, l, tq):
    if dec:
        (ql_ref, conv_p_ref, conv_n_ref, pool_p_ref, pool_n_ref, kv_ref,
         cdk_ref, cdv_ref, cwk_ref, cwv_ref, x_ref, mod_ref, *rest) = refs
    else:
        (x_ref, mod_ref, gin_ref, win_ref, *rest) = refs
    (invcnt_ref, wbd_ref, pscale_ref, lamp_ref, subln_ref, sink_ref,
     dw_ref, dwb_ref, lng_ref, lnb_ref, pw_ref, wout_ref, gpost_ref, gpre_ref, *outs) = rest
    d = D_MODEL
    mod = mod_ref[pl.ds(1 + pl.program_id(0), 1), :] if dec else mod_ref[0:1, :]
    x = x_ref[...]
    row = lambda ref: ref[l:l + 1, :]

    lam_init = 0.8 - 0.6 * math.exp(-0.3 * l)
    lp = lamp_ref[...]
    lam = (jnp.exp(jnp.sum(lp[0:1] * lp[1:2], axis=-1, keepdims=True))
           - jnp.exp(jnp.sum(lp[2:3] * lp[3:4], axis=-1, keepdims=True)) + lam_init)
    gain = row(subln_ref) * (1.0 - lam_init)

    zpad = jnp.zeros((HALO, 256), F32)
    pieces = []
    if dec:
        x1_ref, h2_ref, vp_ref = outs
        ca, cg = ql_ref[:, 0:256], ql_ref[:, 256:512]
        u_pool, dq, wq = ql_ref[:, 512:768], ql_ref[:, 768:1024], ql_ref[:, 1024:1280]
        q = pl.program_id(1)
        nq = pl.num_programs(1)
        pv = (q > 0).astype(F32)
        nv = (q < nq - 1).astype(F32)
        pool_pad = jnp.concatenate([pool_p_ref[...] * pv, u_pool, pool_n_ref[...] * nv], axis=0)
        cp = conv_p_ref[...] * pv
        cn = conv_n_ref[...] * nv
        apad = jnp.concatenate([cp[:, :256], ca, cn[:, :256]], axis=0)
        gpad = jnp.concatenate([cp[:, 256:], cg, cn[:, 256:]], axis=0)

        @pl.when(q == 0)
        def _():
            for h in range(4):
                vp_ref[h, 0:PAST_LEN, :] = _ones_outside(cdv_ref[...], h * 64, h * 64 + 64)
                vp_ref[h, PAST_LEN:, :] = _ones_outside(kv_ref[:, 256:512], h * 64, h * 64 + 64)

        dks = [cdk_ref[...], kv_ref[:, 0:256]]
        vps = lambda h: [vp_ref[h, 0:PAST_LEN, :], vp_ref[h, PAST_LEN:, :]]
        band = tq + 2 * WINDOW
        start = pl.multiple_of(jnp.clip(q * tq - WINDOW, 0, DEC_SEQ - band), WINDOW)
        wkb = kv_ref[pl.ds(start, band), 512:640]
        wvb = kv_ref[pl.ds(start, band), 640:768]
        qpos = q * tq + (lax.broadcasted_iota(jnp.int32, (2 * tq, band), 0) & (tq - 1))
        kpos = start + lax.broadcasted_iota(jnp.int32, (2 * tq, band), 1)
        wmask = jnp.abs(qpos - kpos) <= WINDOW
        yw = _win_attn(wq, [wkb, cwk_ref[...]], [wmask, None], [wvb, cwv_ref[...]], sink_ref, 4 * l, tq, group=1)
        pieces.append((_pool(pool_pad, u_pool, invcnt_ref[...], tq),
                       _diff_attn(dq, dks, vps, lam, gain, tq, group=1), yw[0], yw[1],
                       _conv_module(apad, gpad, dw_ref, row(dwb_ref), row(lng_ref), row(lnb_ref), tq)))
    else:
        x1_ref, h2_ref, kd_ref, vd_ref, kw_ref, vw_ref = outs
        h = _modulated_norm(x, row(gin_ref), mod, 0, 1)
        proj = _dot(h.astype(BF16), win_ref[...])
        for s in range(x.shape[0] // tq):
            ps = proj[s * tq:(s + 1) * tq]
            u_pool, dq = ps[:, 0:256], ps[:, 256:512]
            dk, dv = ps[:, 512:768], ps[:, 768:1024]
            wk, wv = ps[:, 1280:1408], ps[:, 1408:1536]
            ca, cg = ps[:, 1536:1792], ps[:, 1792:2048]
            wq = jnp.concatenate(_swap_middle_heads(ps[:, 1024:1152], ps[:, 1152:1280]), axis=1)
            kd_ref[s] = dk.T
            vd_ref[s] = dv.T
            kw_ref[s] = wk.T
            vw_ref[s] = wv.T
            pool_pad = jnp.concatenate([zpad, u_pool, zpad], axis=0)
            apad = jnp.concatenate([zpad, ca, zpad], axis=0)
            gpad = jnp.concatenate([zpad, cg, zpad], axis=0)
            dvb = dv.astype(BF16)
            vps = lambda h, dvb=dvb: [_ones_outside(dvb, h * 64, h * 64 + 64)]
            yw = _win_attn(wq, [wk.astype(BF16)], [None], [wv.astype(BF16)], sink_ref, 4 * l, tq, group=2)
            pieces.append((_pool(pool_pad, u_pool, invcnt_ref[...], tq),
                           _diff_attn(dq, [dk.astype(BF16)], vps, lam, gain, tq, group=4), yw[0], yw[1],
                           _conv_module(apad, gpad, dw_ref, row(dwb_ref), row(lng_ref), row(lnb_ref), tq)))

    cat = lambda k: jnp.concatenate([t[k] for t in pieces], axis=0).astype(BF16)
    y_pool = _dot(cat(0), wbd_ref[...]) * row(pscale_ref)
    y_conv = _dot(cat(4), pw_ref[...])
    o = (_dot(y_pool.astype(BF16), wout_ref[0:256, :])
         + _dot(cat(1), wout_ref[256:512, :])
         + _dot(cat(2), wout_ref[512:640, :])
         + _dot(cat(3), wout_ref[640:768, :])
         + _dot(y_conv.astype(BF16), wout_ref[768:1024, :]))
    x1 = x + mod[:, 2 * d:3 * d] * _rms(o, row(gpost_ref))
    x1_ref[...] = x1
    h2_ref[...] = _modulated_norm(x1, row(gpre_ref), mod, 3, 4).astype(BF16)


def _mixer_param_specs(l):
    return [
        _layer_spec((256, 256), l),
        _rows_spec(256),
        _layer_spec((4, DIFF_QK), l),
        _rows_spec(256),
        pl.BlockSpec(memory_space=pltpu.SMEM),
        _layer_spec((CONV_K, 256), l),
        _rows_spec(256),
        _rows_spec(256),
        _rows_spec(256),
        _layer_spec((256, 256), l),
        _layer_spec((D_MODEL, D_MODEL), l),
        _rows_spec(D_MODEL),
        _rows_spec(D_MODEL),
    ]


def _mixer_param_args(P):
    return [P["pool_wbd"], P["pool_scale"], P["lam_p"], P["subln"], P["sink"],
            P["conv_dw"], P["conv_dw_b"], P["conv_ln_g"], P["conv_ln_b"], P["conv_pw"], P["w_out"],
            P["g_post_mix"], P["g_pre_ffn"]]


def _mixer_ctx(x, mod, P, l):
    rows = x.shape[0]
    tq = SEQ
    ns = CTX_SEQS_PER_STEP
    row_map = lambda i: (i, 0)
    seq_map = lambda i: (i, 0, 0)
    in_specs = [
        pl.BlockSpec((ns * tq, D_MODEL), row_map),
        _mod_spec(l),
        _rows_spec(D_MODEL),
        _layer_spec((D_MODEL, IN_WIDTH), l),
        pl.BlockSpec((tq, 256), lambda i: (0, 0)),
    ] + _mixer_param_specs(l)
    args = [x, mod, P["g_pre_mix"], P["w_in"], P["invcnt_ctx"]] + _mixer_param_args(P)
    return pl.pallas_call(
        functools.partial(_mixer_kernel, dec=False, l=l, tq=tq),
        out_shape=[jax.ShapeDtypeStruct((rows, D_MODEL), F32),
                   jax.ShapeDtypeStruct((rows, D_MODEL), BF16),
                   jax.ShapeDtypeStruct((BATCH, 256, tq), F32),
                   jax.ShapeDtypeStruct((BATCH, 256, tq), F32),
                   jax.ShapeDtypeStruct((BATCH, 128, tq), F32),
                   jax.ShapeDtypeStruct((BATCH, 128, tq), F32)],
        grid=(rows // (ns * tq),),
        in_specs=in_specs,
        out_specs=[pl.BlockSpec((ns * tq, D_MODEL), row_map), pl.BlockSpec((ns * tq, D_MODEL), row_map),
                   pl.BlockSpec((ns, 256, tq), seq_map), pl.BlockSpec((ns, 256, tq), seq_map),
                   pl.BlockSpec((ns, 128, tq), seq_map), pl.BlockSpec((ns, 128, tq), seq_map)],
        compiler_params=pltpu.CompilerParams(
            dimension_semantics=("arbitrary",), vmem_limit_bytes=VMEM_LIMIT),
        name="mixer_ctx",
    )(*args)


def _mixer_dec(ql, kvb, caches, x, mod, P, l):
    rows = x.shape[0]
    tq = TQ_DEC
    nq = DEC_SEQ // tq
    hb = tq // HALO
    nhb = rows // HALO
    row_map = lambda b, q: (b * nq + q, 0)

    def prev_map(col):
        return lambda b, q: (jnp.maximum((b * nq + q) * hb - 1, 0), col)

    def next_map(col):
        return lambda b, q: (jnp.minimum((b * nq + q + 1) * hb, nhb - 1), col)

    cache_map = lambda b, q: (b, l, 0, 0)
    in_specs = [
        pl.BlockSpec((tq, QL_WIDTH), row_map),
        pl.BlockSpec((HALO, 512), prev_map(0)),
        pl.BlockSpec((HALO, 512), next_map(0)),
        pl.BlockSpec((HALO, 256), prev_map(2)),
        pl.BlockSpec((HALO, 256), next_map(2)),
        pl.BlockSpec((DEC_SEQ, KV_WIDTH), lambda b, q: (b, 0)),
        pl.BlockSpec((None, None, PAST_LEN, 256), cache_map),
        pl.BlockSpec((None, None, PAST_LEN, 256), cache_map),
        pl.BlockSpec((None, None, PAST_LEN, 128), cache_map),
        pl.BlockSpec((None, None, PAST_LEN, 128), cache_map),
        pl.BlockSpec((tq, D_MODEL), row_map),
        _mod_spec(l),
        pl.BlockSpec((tq, 256), lambda b, q: (q, 0)),
    ] + _mixer_param_specs(l)
    args = [ql, ql, ql, ql, ql, kvb, *caches, x, mod, P["invcnt_dec"]] + _mixer_param_args(P)
    return pl.pallas_call(
        functools.partial(_mixer_kernel, dec=True, l=l, tq=tq),
        out_shape=[jax.ShapeDtypeStruct((rows, D_MODEL), F32),
                   jax.ShapeDtypeStruct((rows, D_MODEL), BF16)],
        grid=(DEC_BATCH, nq),
        in_specs=in_specs,
        out_specs=[pl.BlockSpec((tq, D_MODEL), row_map), pl.BlockSpec((tq, D_MODEL), row_map)],
        scratch_shapes=[pltpu.VMEM((4, PAST_LEN + DEC_SEQ, 256), BF16)],
        compiler_params=pltpu.CompilerParams(
            dimension_semantics=("arbitrary", "arbitrary"), vmem_limit_bytes=VMEM_LIMIT),
        name="mixer_dec",
    )(*args)


def _ffn_kernel(h_ref, hp_ref, hn_ref, x_ref, mod_ref, wup_ref, dw_ref, dwb_ref, wd_ref, gpost_ref,
                o_ref, act_ref, *, tm, seq_len, l, mod_row0, tiles_per_b):
    i = pl.program_id(0)
    hx = jnp.concatenate([hp_ref[...], h_ref[...], hn_ref[...]], axis=0)
    pos = (i * tm + lax.broadcasted_iota(jnp.int32, (tm, 1), 0)) % seq_len
    has_left = pos != 0
    has_right = pos != seq_len - 1
    for j in range(D_FF // FFN_CHUNK):
        ys = []
        for part in range(2):
            c0 = part * D_FF + j * FFN_CHUNK
            u = _dot(hx, wup_ref[:, c0:c0 + FFN_CHUNK])
            w = dw_ref[:, c0:c0 + FFN_CHUNK]
            ys.append(jnp.where(has_left, u[FFN_HALO - 1:FFN_HALO - 1 + tm], 0.0) * w[0:1]
                      + u[FFN_HALO:FFN_HALO + tm] * w[1:2]
                      + jnp.where(has_right, u[FFN_HALO + 1:FFN_HALO + 1 + tm], 0.0) * w[2:3]
                      + dwb_ref[l:l + 1, c0:c0 + FFN_CHUNK])
        gate, up = ys
        act_ref[:, j * FFN_CHUNK:(j + 1) * FFN_CHUNK] = (gate * jax.nn.sigmoid(gate) * up).astype(BF16)
    o = _dot(act_ref[...], wd_ref[...])
    g2 = mod_ref[pl.ds(mod_row0 + i // tiles_per_b, 1), 5 * D_MODEL:6 * D_MODEL]
    o_ref[...] = x_ref[...] + g2 * _rms(o, gpost_ref[l:l + 1, :])


def _ffn(h2, x1, mod, P, l, seq_len, mod_row0, nb):
    rows = x1.shape[0]
    tm = TM_FFN
    tiles_per_b = rows // nb // tm
    hb = tm // FFN_HALO
    nhb = rows // FFN_HALO
    return pl.pallas_call(
        functools.partial(_ffn_kernel, tm=tm, seq_len=seq_len, l=l, mod_row0=mod_row0,
                          tiles_per_b=tiles_per_b),
        out_shape=jax.ShapeDtypeStruct((rows, D_MODEL), F32),
        grid=(rows // tm,),
        in_specs=[
            pl.BlockSpec((tm, D_MODEL), lambda i: (i, 0)),
            pl.BlockSpec((FFN_HALO, D_MODEL), lambda i: (jnp.maximum(i * hb - 1, 0), 0)),
            pl.BlockSpec((FFN_HALO, D_MODEL), lambda i: (jnp.minimum((i + 1) * hb, nhb - 1), 0)),
            pl.BlockSpec((tm, D_MODEL), lambda i: (i, 0)),
            _mod_spec(l),
            _layer_spec((D_MODEL, 2 * D_FF), l),
            _layer_spec((3, 2 * D_FF), l),
            _rows_spec(2 * D_FF),
            _layer_spec((D_FF, D_MODEL), l),
            _rows_spec(D_MODEL),
        ],
        out_specs=pl.BlockSpec((tm, D_MODEL), lambda i: (i, 0)),
        scratch_shapes=[pltpu.VMEM((tm, D_FF), BF16)],
        compiler_params=pltpu.CompilerParams(
            dimension_semantics=("arbitrary",), vmem_limit_bytes=VMEM_LIMIT),
        name="ffn",
    )(h2, h2, h2, x1, mod, P["ffn_up"], P["ffn_dw"], P["ffn_dw_b"], P["ffn_down"], P["g_post_ffn"])


def _pool_inv_counts(seq_len):
    inv = np.zeros((seq_len, 256), np.float32)
    pos = np.arange(seq_len)
    for g, w in enumerate(POOL_WINDOWS):
        cnt = np.clip(pos - w // 2 + w, 0, seq_len) - np.clip(pos - w // 2, 0, seq_len)
        inv[:, g * 64:(g + 1) * 64] = (1.0 / cnt.astype(np.float64))[:, None].astype(np.float32)
    return jnp.asarray(inv, F32)


def _rope_tables(seq_len, dim):
    rows = seq_len // GRID_W
    r = np.repeat(np.arange(rows), GRID_W).astype(np.float32)
    col = np.tile(np.arange(GRID_W), rows).astype(np.float32)
    n = dim // 4
    inv = (ROPE_BASE ** (-np.arange(n) / n)).astype(np.float32)
    ang = np.concatenate([r[:, None] * inv[None], col[:, None] * inv[None]], axis=-1)
    cos, sin = np.cos(ang), np.sin(ang)
    reps = LANES // dim
    cos_t = np.tile(np.concatenate([cos, cos], axis=-1), (1, reps))
    sin_t = np.tile(np.concatenate([-sin, sin], axis=-1), (1, reps))
    return jnp.asarray(cos_t, F32), jnp.asarray(sin_t, F32)


def _cache_from_feature_major(per_layer, heads):
    t = jnp.stack(per_layer, axis=1).reshape(BATCH, DEPTH, heads, 64, SEQ)
    return jnp.transpose(t, (0, 1, 4, 2, 3))


def kernel(x_prompt, x_sample, c, cache_diff_k, cache_diff_v, cache_win_k, cache_win_v, c_ctx, w_ada, b_ada, g_pre_mix, g_post_mix, g_pre_ffn, g_post_ffn, w_in, w_out, pool_w, pool_scale, diff_lq1, diff_lk1, diff_lq2, diff_lk2, diff_subln, win_sink, conv_dw, conv_dw_b, conv_ln_g, conv_ln_b, conv_pw, ffn_up, ffn_dw, ffn_dw_b, ffn_down):
    d = D_MODEL
    xp = x_prompt.reshape(BATCH * SEQ, d)
    xs = x_sample.reshape(DEC_BATCH * DEC_SEQ, d)

    cond8 = jnp.concatenate([c_ctx[None, :], c, jnp.zeros((8 - 1 - DEC_BATCH, d), F32)], axis=0)
    mod = _ada(cond8, w_ada, b_ada)

    eye4 = jnp.eye(4, dtype=F32)
    P = dict(
        invcnt_ctx=_pool_inv_counts(SEQ), invcnt_dec=_pool_inv_counts(DEC_SEQ),
        g_pre_mix=g_pre_mix, w_in=w_in.astype(BF16),
        pool_wbd=jnp.einsum("lgcd,gh->lgchd", pool_w, eye4).reshape(DEPTH, 256, 256).astype(BF16),
        pool_scale=pool_scale,
        lam_p=jnp.stack([diff_lq1, diff_lk1, diff_lq2, diff_lk2], axis=1),
        subln=jnp.tile(diff_subln, (1, 4)),
        sink=win_sink.reshape(DEPTH * 4),
        conv_dw=conv_dw, conv_dw_b=conv_dw_b, conv_ln_g=conv_ln_g, conv_ln_b=conv_ln_b,
        conv_pw=conv_pw.astype(BF16), w_out=w_out.astype(BF16),
        g_post_mix=g_post_mix, g_pre_ffn=g_pre_ffn,
        ffn_up=ffn_up.astype(BF16), ffn_dw=ffn_dw, ffn_dw_b=ffn_dw_b,
        ffn_down=ffn_down.astype(BF16), g_post_ffn=g_post_ffn,
    )
    rope_tabs = _rope_tables(DEC_SEQ, DIFF_QK) + _rope_tables(DEC_SEQ, WIN_HD)
    caches = (cache_diff_k.reshape(DEC_BATCH, DEPTH, PAST_LEN, 256).astype(BF16),
              cache_diff_v.reshape(DEC_BATCH, DEPTH, PAST_LEN, 256).astype(BF16),
              cache_win_k.reshape(DEC_BATCH, DEPTH, PAST_LEN, 128).astype(BF16),
              cache_win_v.reshape(DEC_BATCH, DEPTH, PAST_LEN, 128).astype(BF16))

    kd, vd, kw, vw = [], [], [], []
    for l in range(DEPTH):
        xp1, hp2, kd_l, vd_l, kw_l, vw_l = _mixer_ctx(xp, mod, P, l)
        kd.append(kd_l)
        vd.append(vd_l)
        kw.append(kw_l)
        vw.append(vw_l)
        xp = _ffn(hp2, xp1, mod, P, l, SEQ, 0, 1)

        ql, kvb = _inproj_dec(xs, mod, P["g_pre_mix"], P["w_in"], l, rope_tabs)
        xs1, hs2 = _mixer_dec(ql, kvb, caches, xs, mod, P, l)
        xs = _ffn(hs2, xs1, mod, P, l, DEC_SEQ, 1, DEC_BATCH)

    return (xp.reshape(BATCH, SEQ, d), xs.reshape(DEC_BATCH, DEC_SEQ, d),
            _cache_from_feature_major(kd, 4), _cache_from_feature_major(vd, 4),
            _cache_from_feature_major(kw, 2), _cache_from_feature_major(vw, 2))
```

```python
import functools
import math

import jax
import jax.numpy as jnp
import numpy as np
from jax import lax
from jax.experimental import pallas as pl
from jax.experimental.pallas import tpu as pltpu

F32 = jnp.float32
BF16 = jnp.bfloat16

D_MODEL = 1024
BATCH = 16
SEQ = 256
DEPTH = 2
DEC_BATCH = 2
DEC_SEQ = 2048
PAST_LEN = 256
GRID_W = 64
POOL_WINDOWS = (2, 4, 8, 16)
DIFF_QK = 32
WIN_HD = 64
WINDOW = 128
CONV_K = 31
IN_WIDTH = 2048
D_FF = 2816
ROPE_BASE = 10000.0
EPS = 1e-6
LOG2E = 1.4426950408889634

LANES = 128
SUBLANES = 8
HALO = 16
FFN_HALO = 16
FFN_CHUNK = 256
KV_WIDTH = 768
QL_WIDTH = 1280
VMEM_LIMIT = 56 * 1024 * 1024

TM_IN = 512
CTX_SEQS_PER_STEP = 2
TQ_DEC = 256
TM_FFN = 1024


def _rms(x, g):
    return x * lax.rsqrt(jnp.mean(x * x, axis=-1, keepdims=True) + EPS) * g


def _dot(a, b):
    return jnp.dot(a, b, preferred_element_type=F32)


def _dot_nt(a, b):
    return lax.dot_general(a, b, (((1,), (1,)), ((), ())), preferred_element_type=F32)


def _lane_range(shape, lo, hi):
    lane = lax.broadcasted_iota(jnp.int32, shape, 1)
    return (lane >= lo) & (lane < hi)


def _swap_middle_heads(t0, t1):
    lo = _lane_range(t0.shape, 0, 64)
    return (jnp.where(lo, t0, pltpu.roll(t1, 64, 1)),
            jnp.where(lo, pltpu.roll(t0, 64, 1), t1))


def _modulated_norm(x, g, mod, shift_col, scale_col):
    d = D_MODEL
    return (_rms(x, g) * (1.0 + mod[:, scale_col * d:(scale_col + 1) * d])
            + mod[:, shift_col * d:(shift_col + 1) * d])


def _layer_spec(shape, l):
    nz = len(shape)
    return pl.BlockSpec((None,) + tuple(shape), lambda *_: (l,) + (0,) * nz,
                        pipeline_mode=pl.Buffered(1))


def _whole_spec(shape):
    return pl.BlockSpec(tuple(shape), lambda *_: (0,) * len(shape), pipeline_mode=pl.Buffered(1))


def _rows_spec(width):
    return pl.BlockSpec((DEPTH, width), lambda *_: (0, 0), pipeline_mode=pl.Buffered(1))


def _mod_spec(l):
    return pl.BlockSpec((None, 8, 6 * D_MODEL), lambda *_: (l, 0, 0), pipeline_mode=pl.Buffered(1))


def _ada_kernel(c_ref, w_ref, b_ref, o_ref):
    c = c_ref[...]
    s = (c * jax.nn.sigmoid(c)).astype(BF16)
    o_ref[...] = _dot(s, w_ref[...].astype(BF16)) + b_ref[pl.ds(pl.program_id(0), 1), :]


def _ada(cond8, w_ada, b_ada):
    tn = 1024
    return pl.pallas_call(
        _ada_kernel,
        out_shape=jax.ShapeDtypeStruct((DEPTH, 8, 6 * D_MODEL), F32),
        grid=(DEPTH, 6 * D_MODEL // tn),
        in_specs=[
            pl.BlockSpec((8, D_MODEL), lambda l, j: (0, 0)),
            pl.BlockSpec((None, D_MODEL, tn), lambda l, j: (l, 0, j)),
            pl.BlockSpec((DEPTH, tn), lambda l, j: (0, j)),
        ],
        out_specs=pl.BlockSpec((None, 8, tn), lambda l, j: (l, 0, j)),
        compiler_params=pltpu.CompilerParams(
            dimension_semantics=("arbitrary", "arbitrary"), vmem_limit_bytes=VMEM_LIMIT),
        name="ada_mod",
    )(cond8, w_ada, b_ada)


def _rope_tile(v, cos, sin_signed, half):
    is_a = (lax.broadcasted_iota(jnp.int32, v.shape, 1) % (2 * half)) < half
    partner = jnp.where(is_a, pltpu.roll(v, LANES - half, 1), pltpu.roll(v, half, 1))
    return v * cos + partner * sin_signed


def _inproj_dec_kernel(x_ref, mod_ref, g_ref, w_ref, cd_ref, sd_ref, cw_ref, sw_ref, ql_ref, kv_ref,
                       *, l, tiles_per_b):
    mod = mod_ref[pl.ds(1 + pl.program_id(0) // tiles_per_b, 1), :]
    h = _modulated_norm(x_ref[...], g_ref[l:l + 1, :], mod, 0, 1)
    proj = _dot(h.astype(BF16), w_ref[...])
    tiles = [proj[:, i * LANES:(i + 1) * LANES] for i in range(IN_WIDTH // LANES)]
    cd, sd, cw, sw = cd_ref[...], sd_ref[...], cw_ref[...], sw_ref[...]
    for i in (2, 3, 4, 5):
        tiles[i] = _rope_tile(tiles[i], cd, sd, DIFF_QK // 2)
    for i in (8, 9, 10):
        tiles[i] = _rope_tile(tiles[i], cw, sw, WIN_HD // 2)
    tiles[8], tiles[9] = _swap_middle_heads(tiles[8], tiles[9])
    for j, i in enumerate((12, 13, 14, 15, 0, 1, 2, 3, 8, 9)):
        ql_ref[:, j * LANES:(j + 1) * LANES] = tiles[i]
    for j, i in enumerate((4, 5, 6, 7, 10, 11)):
        kv_ref[:, j * LANES:(j + 1) * LANES] = tiles[i].astype(BF16)


def _inproj_dec(x, mod, g, w_in_b, l, rope_tabs):
    rows = x.shape[0]
    tm = TM_IN
    tiles_per_b = DEC_SEQ // tm
    in_specs = [
        pl.BlockSpec((tm, D_MODEL), lambda i: (i, 0)),
        _mod_spec(l),
        _rows_spec(D_MODEL),
        _whole_spec((D_MODEL, IN_WIDTH)),
    ] + [pl.BlockSpec((tm, LANES), lambda i: (i % tiles_per_b, 0)) for _ in rope_tabs]
    return pl.pallas_call(
        functools.partial(_inproj_dec_kernel, l=l, tiles_per_b=tiles_per_b),
        out_shape=[jax.ShapeDtypeStruct((rows, QL_WIDTH), F32),
                   jax.ShapeDtypeStruct((rows, KV_WIDTH), BF16)],
        grid=(rows // tm,),
        in_specs=in_specs,
        out_specs=[pl.BlockSpec((tm, QL_WIDTH), lambda i: (i, 0)),
                   pl.BlockSpec((tm, KV_WIDTH), lambda i: (i, 0))],
        compiler_params=pltpu.CompilerParams(
            dimension_semantics=("arbitrary",), vmem_limit_bytes=VMEM_LIMIT),
        name="inproj_dec",
    )(x, mod, g, w_in_b, *rope_tabs)


def _pool(xpad, u, invcnt, tq):
    rows = xpad.shape[0]
    p = xpad
    sums = []
    for step, w in zip((1, 2, 4, 8), POOL_WINDOWS):
        p = p + pltpu.roll(p, step, 0)
        off = HALO + w // 2 - 1
        r = off % SUBLANES
        sh = p if r == 0 else pltpu.roll(p, rows - r, 0)
        sums.append(sh[off - r:off - r + tq])
    shape = u.shape
    s = jnp.where(_lane_range(shape, 0, 64), sums[0],
                  jnp.where(_lane_range(shape, 64, 128), sums[1],
                            jnp.where(_lane_range(shape, 128, 192), sums[2], sums[3])))
    return s * invcnt - u


def _conv_module(apad, gpad, dw_ref, dwb, lng, lnb, tq):
    u = apad * jax.nn.sigmoid(gpad)
    rows = u.shape[0]
    acc = jnp.broadcast_to(dwb, (tq, u.shape[1]))
    for r in range(SUBLANES):
        ur = u if r == 0 else pltpu.roll(u, rows - r, 0)
        for a8 in range(4):
            k = SUBLANES * a8 + r - 1
            if 0 <= k < CONV_K:
                acc = acc + ur[SUBLANES * a8:SUBLANES * a8 + tq] * dw_ref[k:k + 1, :]
    mu = jnp.mean(acc, axis=-1, keepdims=True)
    xc = acc - mu
    var = jnp.mean(xc * xc, axis=-1, keepdims=True)
    y = xc * lax.rsqrt(var + EPS) * lng + lnb
    return y * jax.nn.sigmoid(y)


def _ones_outside(v, lo, hi):
    m = jnp.where(_lane_range((1, v.shape[1]), lo, hi), 1.0, 0.0).astype(BF16)
    return v * m + (1.0 - m)


def _diff_attn(dq, ks, vps, lam, gain, tq, group):
    dqs = dq * (DIFF_QK ** -0.5 * LOG2E)
    ytiles = [jnp.zeros((tq, LANES), F32), jnp.zeros((tq, LANES), F32)]
    for h0 in range(0, 4, group):
        qs = []
        for h in range(h0, h0 + group):
            for lo in (h * 64, h * 64 + DIFF_QK):
                qs.append(jnp.where(_lane_range(dqs.shape, lo, lo + DIFF_QK), dqs, 0.0))
        qg = jnp.concatenate(qs, axis=0).astype(BF16)
        ss = [_dot_nt(qg, k) for k in ks]
        mx = functools.reduce(jnp.maximum, [jnp.max(s, axis=-1, keepdims=True) for s in ss])
        es = [jnp.exp2(s - mx).astype(BF16) for s in ss]
        for h in range(h0, h0 + group):
            r0 = (h - h0) * 2 * tq
            o = None
            for e, vp in zip(es, vps(h)):
                t = _dot(e[r0:r0 + 2 * tq], vp)
                o = t if o is None else o + t
            tile = h // 2
            ot = o[:, tile * LANES:(tile + 1) * LANES]
            hlo = (h % 2) * 64
            den = jnp.max(jnp.where(_lane_range(ot.shape, 64 - hlo, 128 - hlo), ot, 0.0), axis=-1, keepdims=True)
            r = ot / den
            od = jnp.where(_lane_range((tq, LANES), hlo, hlo + 64), r[:tq] - lam * r[tq:], 0.0)
            ms = jnp.sum(od * od, axis=-1, keepdims=True) * (1.0 / 64.0)
            ytiles[tile] = ytiles[tile] + od * lax.rsqrt(ms + EPS)
    return jnp.concatenate(ytiles, axis=1) * gain


def _win_attn(wq, ks, masks, vs, sink_ref, sink_base, tq, group):
    wqs = wq * (WIN_HD ** -0.5 * LOG2E)
    t0, t1 = wqs[:, :LANES], wqs[:, LANES:]
    y0 = jnp.zeros((tq, LANES), F32)
    y1 = jnp.zeros((tq, LANES), F32)
    rows = 2 * tq * group
    row = lax.broadcasted_iota(jnp.int32, (rows, 1), 0)
    for j0 in range(0, 2, group):
        qs = []
        sk = jnp.zeros((rows, 1), F32)
        for j in range(j0, j0 + group):
            lm = _lane_range(t0.shape, j * 64, j * 64 + 64)
            qs += [jnp.where(lm, t0, 0.0), jnp.where(lm, t1, 0.0)]
            for g in range(2):
                r0 = ((j - j0) * 2 + g) * tq
                sk = jnp.where((row >= r0) & (row < r0 + tq), sink_ref[sink_base + 2 * j + g], sk)
        sk = sk * LOG2E
        qg = jnp.concatenate(qs, axis=0).astype(BF16)
        ss = []
        for k, m in zip(ks, masks):
            s = _dot_nt(qg, k)
            if m is not None:
                s = jnp.where(m, s, -1e30)
            ss.append(s)
        mx = functools.reduce(jnp.maximum, [jnp.max(s, axis=-1, keepdims=True) for s in ss])
        mx = jnp.maximum(mx, sk)
        es = [jnp.exp2(s - mx).astype(BF16) for s in ss]
        esink = jnp.exp2(sk - mx)
        for j in range(j0, j0 + group):
            r0 = (j - j0) * 2 * tq
            o = None
            for e, v in zip(es, vs):
                t = _dot(e[r0:r0 + 2 * tq], _ones_outside(v, j * 64, j * 64 + 64))
                o = t if o is None else o + t
            other = _lane_range(o.shape, 64 - j * 64, 128 - j * 64)
            den = jnp.max(jnp.where(other, o, 0.0), axis=-1, keepdims=True)
            r = o / (den + esink[r0:r0 + 2 * tq])
            lm = _lane_range((tq, LANES), j * 64, j * 64 + 64)
            y0 = y0 + jnp.where(lm, r[:tq], 0.0)
            y1 = y1 + jnp.where(lm, r[tq:], 0.0)
    return _swap_middle_heads(y0, y1)


def _mixer_kernel(*refs, dec, l, tq, n_cast=0):
    if dec:
        (ql_ref, conv_p_ref, conv_n_ref, pool_p_ref, pool_n_ref, kv_ref,
         cdk_ref, cdv_ref, cwk_ref, cwv_ref, x_ref, mod_ref, *rest) = refs
    else:
        (x_ref, mod_ref, gin_ref, win_ref, *rest) = refs
    (invcnt_ref, wbd_ref, pscale_ref, lamp_ref, subln_ref, sink_ref,
     dw_ref, dwb_ref, lng_ref, lnb_ref, pw_ref, wout_ref, gpost_ref, gpre_ref, *tail) = rest
    cast_in, outs = tail[:n_cast], tail[n_cast:]
    d = D_MODEL
    mod = mod_ref[pl.ds(1 + pl.program_id(0), 1), :] if dec else mod_ref[0:1, :]
    x = x_ref[...]
    row = lambda ref: ref[l:l + 1, :]

    lam_init = 0.8 - 0.6 * math.exp(-0.3 * l)
    lp = lamp_ref[...]
    lam = (jnp.exp(jnp.sum(lp[0:1] * lp[1:2], axis=-1, keepdims=True))
           - jnp.exp(jnp.sum(lp[2:3] * lp[3:4], axis=-1, keepdims=True)) + lam_init)
    gain = row(subln_ref) * (1.0 - lam_init)

    zpad = jnp.zeros((HALO, 256), F32)
    pieces = []
    if dec:
        x1_ref, h2_ref, *cast_out, vp_ref = outs
        for src, dst in zip(cast_in, cast_out):
            dst[...] = src[...].astype(BF16)
        ca, cg = ql_ref[:, 0:256], ql_ref[:, 256:512]
        u_pool, dq, wq = ql_ref[:, 512:768], ql_ref[:, 768:1024], ql_ref[:, 1024:1280]
        q = pl.program_id(1)
        nq = pl.num_programs(1)
        pv = (q > 0).astype(F32)
        nv = (q < nq - 1).astype(F32)
        pool_pad = jnp.concatenate([pool_p_ref[...] * pv, u_pool, pool_n_ref[...] * nv], axis=0)
        cp = conv_p_ref[...] * pv
        cn = conv_n_ref[...] * nv
        apad = jnp.concatenate([cp[:, :256], ca, cn[:, :256]], axis=0)
        gpad = jnp.concatenate([cp[:, 256:], cg, cn[:, 256:]], axis=0)

        @pl.when(q == 0)
        def _():
            for h in range(4):
                vp_ref[h, 0:PAST_LEN, :] = _ones_outside(cdv_ref[...], h * 64, h * 64 + 64)
                vp_ref[h, PAST_LEN:, :] = _ones_outside(kv_ref[:, 256:512], h * 64, h * 64 + 64)

        dks = [cdk_ref[...], kv_ref[:, 0:256]]
        vps = lambda h: [vp_ref[h, 0:PAST_LEN, :], vp_ref[h, PAST_LEN:, :]]
        band = tq + 2 * WINDOW
        start = pl.multiple_of(jnp.clip(q * tq - WINDOW, 0, DEC_SEQ - band), WINDOW)
        wkb = kv_ref[pl.ds(start, band), 512:640]
        wvb = kv_ref[pl.ds(start, band), 640:768]
        qpos = q * tq + (lax.broadcasted_iota(jnp.int32, (2 * tq, band), 0) & (tq - 1))
        kpos = start + lax.broadcasted_iota(jnp.int32, (2 * tq, band), 1)
        wmask = jnp.abs(qpos - kpos) <= WINDOW
        yw = _win_attn(wq, [wkb, cwk_ref[...]], [wmask, None], [wvb, cwv_ref[...]], sink_ref, 4 * l, tq, group=1)
        pieces.append((_pool(pool_pad, u_pool, invcnt_ref[...], tq),
                       _diff_attn(dq, dks, vps, lam, gain, tq, group=1), yw[0], yw[1],
                       _conv_module(apad, gpad, dw_ref, row(dwb_ref), row(lng_ref), row(lnb_ref), tq)))
    else:
        x1_ref, h2_ref, kd_ref, vd_ref, kw_ref, vw_ref = outs
        h = _modulated_norm(x, row(gin_ref), mod, 0, 1)
        proj = _dot(h.astype(BF16), win_ref[...])
        for s in range(x.shape[0] // tq):
            ps = proj[s * tq:(s + 1) * tq]
            u_pool, dq = ps[:, 0:256], ps[:, 256:512]
            dk, dv = ps[:, 512:768], ps[:, 768:1024]
            wk, wv = ps[:, 1280:1408], ps[:, 1408:1536]
            ca, cg = ps[:, 1536:1792], ps[:, 1792:2048]
            wq = jnp.concatenate(_swap_middle_heads(ps[:, 1024:1152], ps[:, 1152:1280]), axis=1)
            kd_ref[s] = dk.T
            vd_ref[s] = dv.T
            kw_ref[s] = wk.T
            vw_ref[s] = wv.T
            pool_pad = jnp.concatenate([zpad, u_pool, zpad], axis=0)
            apad = jnp.concatenate([zpad, ca, zpad], axis=0)
            gpad = jnp.concatenate([zpad, cg, zpad], axis=0)
            dvb = dv.astype(BF16)
            vps = lambda h, dvb=dvb: [_ones_outside(dvb, h * 64, h * 64 + 64)]
            yw = _win_attn(wq, [wk.astype(BF16)], [None], [wv.astype(BF16)], sink_ref, 4 * l, tq, group=2)
            pieces.append((_pool(pool_pad, u_pool, invcnt_ref[...], tq),
                           _diff_attn(dq, [dk.astype(BF16)], vps, lam, gain, tq, group=4), yw[0], yw[1],
                           _conv_module(apad, gpad, dw_ref, row(dwb_ref), row(lng_ref), row(lnb_ref), tq)))

    cat = lambda k: jnp.concatenate([t[k] for t in pieces], axis=0).astype(BF16)
    y_pool = _dot(cat(0), wbd_ref[...]) * row(pscale_ref)
    y_conv = _dot(cat(4), pw_ref[...])
    o = (_dot(y_pool.astype(BF16), wout_ref[0:256, :])
         + _dot(cat(1), wout_ref[256:512, :])
         + _dot(cat(2), wout_ref[512:640, :])
         + _dot(cat(3), wout_ref[640:768, :])
         + _dot(y_conv.astype(BF16), wout_ref[768:1024, :]))
    x1 = x + mod[:, 2 * d:3 * d] * _rms(o, row(gpost_ref))
    x1_ref[...] = x1
    h2_ref[...] = _modulated_norm(x1, row(gpre_ref), mod, 3, 4).astype(BF16)


def _mixer_param_specs(l):
    return [
        _layer_spec((256, 256), l),
        _rows_spec(256),
        _layer_spec((4, DIFF_QK), l),
        _rows_spec(256),
        pl.BlockSpec(memory_space=pltpu.SMEM),
        _layer_spec((CONV_K, 256), l),
        _rows_spec(256),
        _rows_spec(256),
        _rows_spec(256),
        _layer_spec((256, 256), l),
        _whole_spec((D_MODEL, D_MODEL)),
        _rows_spec(D_MODEL),
        _rows_spec(D_MODEL),
    ]


def _mixer_param_args(P, l):
    return [P["pool_wbd"], P["pool_scale"], P["lam_p"], P["subln"], P["sink"],
            P["conv_dw"], P["conv_dw_b"], P["conv_ln_g"], P["conv_ln_b"], P["conv_pw"], P["w_out"][l],
            P["g_post_mix"], P["g_pre_ffn"]]


def _mixer_ctx(x, mod, P, l):
    rows = x.shape[0]
    tq = SEQ
    ns = CTX_SEQS_PER_STEP
    row_map = lambda i: (i, 0)
    seq_map = lambda i: (i, 0, 0)
    in_specs = [
        pl.BlockSpec((ns * tq, D_MODEL), row_map),
        _mod_spec(l),
        _rows_spec(D_MODEL),
        _whole_spec((D_MODEL, IN_WIDTH)),
        pl.BlockSpec((tq, 256), lambda i: (0, 0)),
    ] + _mixer_param_specs(l)
    args = [x, mod, P["g_pre_mix"], P["w_in"][l], P["invcnt_ctx"]] + _mixer_param_args(P, l)
    return pl.pallas_call(
        functools.partial(_mixer_kernel, dec=False, l=l, tq=tq),
        out_shape=[jax.ShapeDtypeStruct((rows, D_MODEL), F32),
                   jax.ShapeDtypeStruct((rows, D_MODEL), BF16),
                   jax.ShapeDtypeStruct((BATCH, 256, tq), F32),
                   jax.ShapeDtypeStruct((BATCH, 256, tq), F32),
                   jax.ShapeDtypeStruct((BATCH, 128, tq), F32),
                   jax.ShapeDtypeStruct((BATCH, 128, tq), F32)],
        grid=(rows // (ns * tq),),
        in_specs=in_specs,
        out_specs=[pl.BlockSpec((ns * tq, D_MODEL), row_map), pl.BlockSpec((ns * tq, D_MODEL), row_map),
                   pl.BlockSpec((ns, 256, tq), seq_map), pl.BlockSpec((ns, 256, tq), seq_map),
                   pl.BlockSpec((ns, 128, tq), seq_map), pl.BlockSpec((ns, 128, tq), seq_map)],
        compiler_params=pltpu.CompilerParams(
            dimension_semantics=("arbitrary",), vmem_limit_bytes=VMEM_LIMIT),
        name="mixer_ctx",
    )(*args)


def _mixer_dec(ql, kvb, caches, x, mod, P, l, to_cast=(), cast_layer=0):
    rows = x.shape[0]
    tq = TQ_DEC
    nq = DEC_SEQ // tq
    steps = DEC_BATCH * nq
    cast_in_specs = [pl.BlockSpec((None, w.shape[1] // steps, w.shape[2]),
                                  lambda b, q: (cast_layer, b * nq + q, 0)) for w in to_cast]
    cast_out_specs = [pl.BlockSpec((w.shape[1] // steps, w.shape[2]), lambda b, q: (b * nq + q, 0))
                      for w in to_cast]
    hb = tq // HALO
    nhb = rows // HALO
    row_map = lambda b, q: (b * nq + q, 0)

    def prev_map(col):
        return lambda b, q: (jnp.maximum((b * nq + q) * hb - 1, 0), col)

    def next_map(col):
        return lambda b, q: (jnp.minimum((b * nq + q + 1) * hb, nhb - 1), col)

    cache_map = lambda b, q: (b, l, 0, 0)
    in_specs = [
        pl.BlockSpec((tq, QL_WIDTH), row_map),
        pl.BlockSpec((HALO, 512), prev_map(0)),
        pl.BlockSpec((HALO, 512), next_map(0)),
        pl.BlockSpec((HALO, 256), prev_map(2)),
        pl.BlockSpec((HALO, 256), next_map(2)),
        pl.BlockSpec((DEC_SEQ, KV_WIDTH), lambda b, q: (b, 0)),
        pl.BlockSpec((None, None, PAST_LEN, 256), cache_map),
        pl.BlockSpec((None, None, PAST_LEN, 256), cache_map),
        pl.BlockSpec((None, None, PAST_LEN, 128), cache_map),
        pl.BlockSpec((None, None, PAST_LEN, 128), cache_map),
        pl.BlockSpec((tq, D_MODEL), row_map),
        _mod_spec(l),
        pl.BlockSpec((tq, 256), lambda b, q: (q, 0)),
    ] + _mixer_param_specs(l)
    args = [ql, ql, ql, ql, ql, kvb, *caches, x, mod, P["invcnt_dec"]] + _mixer_param_args(P, l)
    return pl.pallas_call(
        functools.partial(_mixer_kernel, dec=True, l=l, tq=tq, n_cast=len(to_cast)),
        out_shape=[jax.ShapeDtypeStruct((rows, D_MODEL), F32),
                   jax.ShapeDtypeStruct((rows, D_MODEL), BF16)]
        + [jax.ShapeDtypeStruct(w.shape[1:], BF16) for w in to_cast],
        grid=(DEC_BATCH, nq),
        in_specs=in_specs + cast_in_specs,
        out_specs=[pl.BlockSpec((tq, D_MODEL), row_map), pl.BlockSpec((tq, D_MODEL), row_map)] + cast_out_specs,
        scratch_shapes=[pltpu.VMEM((4, PAST_LEN + DEC_SEQ, 256), BF16)],
        compiler_params=pltpu.CompilerParams(
            dimension_semantics=("arbitrary", "arbitrary"), vmem_limit_bytes=VMEM_LIMIT),
        name="mixer_dec",
    )(*args, *to_cast)


def _ffn_kernel(*refs, tm, seq_len, l, mod_row0, tiles_per_b, halo):
    if halo:
        h_ref, hp_ref, hn_ref, *rest = refs
        hx = jnp.concatenate([hp_ref[...], h_ref[...], hn_ref[...]], axis=0)
    else:
        h_ref, *rest = refs
        hx = h_ref[...]
    x_ref, mod_ref, wup_ref, dw_ref, dwb_ref, wd_ref, gpost_ref, o_ref, act_ref = rest
    i = pl.program_id(0)
    pos = (i * tm + lax.broadcasted_iota(jnp.int32, (tm, 1), 0)) % seq_len
    has_left = pos != 0
    has_right = pos != seq_len - 1
    for j in range(D_FF // FFN_CHUNK):
        ys = []
        for part in range(2):
            c0 = part * D_FF + j * FFN_CHUNK
            u = _dot(hx, wup_ref[:, c0:c0 + FFN_CHUNK])
            w = dw_ref[:, c0:c0 + FFN_CHUNK]
            if halo:
                below, mid, above = (u[FFN_HALO - 1:FFN_HALO - 1 + tm], u[FFN_HALO:FFN_HALO + tm],
                                     u[FFN_HALO + 1:FFN_HALO + 1 + tm])
            else:
                below, mid, above = pltpu.roll(u, 1, 0), u, pltpu.roll(u, tm - 1, 0)
            ys.append(jnp.where(has_left, below, 0.0) * w[0:1] + mid * w[1:2]
                      + jnp.where(has_right, above, 0.0) * w[2:3]
                      + dwb_ref[l:l + 1, c0:c0 + FFN_CHUNK])
        gate, up = ys
        act_ref[:, j * FFN_CHUNK:(j + 1) * FFN_CHUNK] = (gate * jax.nn.sigmoid(gate) * up).astype(BF16)
    o = _dot(act_ref[...], wd_ref[...])
    g2 = mod_ref[pl.ds(mod_row0 + i // tiles_per_b, 1), 5 * D_MODEL:6 * D_MODEL]
    o_ref[...] = x_ref[...] + g2 * _rms(o, gpost_ref[l:l + 1, :])


def _ffn(h2, x1, mod, P, l, seq_len, mod_row0, nb):
    rows = x1.shape[0]
    tm = TM_FFN
    tiles_per_b = rows // nb // tm
    hb = tm // FFN_HALO
    nhb = rows // FFN_HALO
    halo = tm % seq_len != 0
    h_specs = [pl.BlockSpec((tm, D_MODEL), lambda i: (i, 0))]
    if halo:
        h_specs += [pl.BlockSpec((FFN_HALO, D_MODEL), lambda i: (jnp.maximum(i * hb - 1, 0), 0)),
                    pl.BlockSpec((FFN_HALO, D_MODEL), lambda i: (jnp.minimum((i + 1) * hb, nhb - 1), 0))]
    return pl.pallas_call(
        functools.partial(_ffn_kernel, tm=tm, seq_len=seq_len, l=l, mod_row0=mod_row0,
                          tiles_per_b=tiles_per_b, halo=halo),
        out_shape=jax.ShapeDtypeStruct((rows, D_MODEL), F32),
        grid=(rows // tm,),
        in_specs=h_specs + [
            pl.BlockSpec((tm, D_MODEL), lambda i: (i, 0)),
            _mod_spec(l),
            _whole_spec((D_MODEL, 2 * D_FF)),
            _layer_spec((3, 2 * D_FF), l),
            _rows_spec(2 * D_FF),
            _whole_spec((D_FF, D_MODEL)),
            _rows_spec(D_MODEL),
        ],
        out_specs=pl.BlockSpec((tm, D_MODEL), lambda i: (i, 0)),
        scratch_shapes=[pltpu.VMEM((tm, D_FF), BF16)],
        compiler_params=pltpu.CompilerParams(
            dimension_semantics=("arbitrary",), vmem_limit_bytes=VMEM_LIMIT),
        name="ffn",
    )(*([h2] * len(h_specs)), x1, mod, P["ffn_up"][l], P["ffn_dw"], P["ffn_dw_b"], P["ffn_down"][l],
      P["g_post_ffn"])


def _pool_inv_counts(seq_len):
    inv = np.zeros((seq_len, 256), np.float32)
    pos = np.arange(seq_len)
    for g, w in enumerate(POOL_WINDOWS):
        cnt = np.clip(pos - w // 2 + w, 0, seq_len) - np.clip(pos - w // 2, 0, seq_len)
        inv[:, g * 64:(g + 1) * 64] = (1.0 / cnt.astype(np.float64))[:, None].astype(np.float32)
    return jnp.asarray(inv, F32)


def _rope_tables(seq_len, dim):
    rows = seq_len // GRID_W
    r = np.repeat(np.arange(rows), GRID_W).astype(np.float32)
    col = np.tile(np.arange(GRID_W), rows).astype(np.float32)
    n = dim // 4
    inv = (ROPE_BASE ** (-np.arange(n) / n)).astype(np.float32)
    ang = np.concatenate([r[:, None] * inv[None], col[:, None] * inv[None]], axis=-1)
    cos, sin = np.cos(ang), np.sin(ang)
    reps = LANES // dim
    cos_t = np.tile(np.concatenate([cos, cos], axis=-1), (1, reps))
    sin_t = np.tile(np.concatenate([-sin, sin], axis=-1), (1, reps))
    return jnp.asarray(cos_t, F32), jnp.asarray(sin_t, F32)


def _cache_from_feature_major(per_layer, heads):
    t = jnp.stack(per_layer, axis=1).reshape(BATCH, DEPTH, heads, 64, SEQ)
    return jnp.transpose(t, (0, 1, 4, 2, 3))


def kernel(x_prompt, x_sample, c, cache_diff_k, cache_diff_v, cache_win_k, cache_win_v, c_ctx, w_ada, b_ada, g_pre_mix, g_post_mix, g_pre_ffn, g_post_ffn, w_in, w_out, pool_w, pool_scale, diff_lq1, diff_lk1, diff_lq2, diff_lk2, diff_subln, win_sink, conv_dw, conv_dw_b, conv_ln_g, conv_ln_b, conv_pw, ffn_up, ffn_dw, ffn_dw_b, ffn_down):
    d = D_MODEL
    xp = x_prompt.reshape(BATCH * SEQ, d)
    xs = x_sample.reshape(DEC_BATCH * DEC_SEQ, d)

    cond8 = jnp.concatenate([c_ctx[None, :], c, jnp.zeros((8 - 1 - DEC_BATCH, d), F32)], axis=0)
    mod = _ada(cond8, w_ada, b_ada)

    eye4 = jnp.eye(4, dtype=F32)
    P = dict(
        invcnt_ctx=_pool_inv_counts(SEQ), invcnt_dec=_pool_inv_counts(DEC_SEQ),
        g_pre_mix=g_pre_mix,
        pool_wbd=jnp.einsum("lgcd,gh->lgchd", pool_w, eye4).reshape(DEPTH, 256, 256).astype(BF16),
        pool_scale=pool_scale,
        lam_p=jnp.stack([diff_lq1, diff_lk1, diff_lq2, diff_lk2], axis=1),
        subln=jnp.tile(diff_subln, (1, 4)),
        sink=win_sink.reshape(DEPTH * 4),
        conv_dw=conv_dw, conv_dw_b=conv_dw_b, conv_ln_g=conv_ln_g, conv_ln_b=conv_ln_b,
        conv_pw=conv_pw.astype(BF16),
        g_post_mix=g_post_mix, g_pre_ffn=g_pre_ffn,
        ffn_dw=ffn_dw, ffn_dw_b=ffn_dw_b, g_post_ffn=g_post_ffn,
    )
    big = dict(w_in=w_in, w_out=w_out, ffn_up=ffn_up, ffn_down=ffn_down)
    for name, w in big.items():
        P[name] = [w[0].astype(BF16)]
    rope_tabs = _rope_tables(DEC_SEQ, DIFF_QK) + _rope_tables(DEC_SEQ, WIN_HD)
    caches = (cache_diff_k.reshape(DEC_BATCH, DEPTH, PAST_LEN, 256).astype(BF16),
              cache_diff_v.reshape(DEC_BATCH, DEPTH, PAST_LEN, 256).astype(BF16),
              cache_win_k.reshape(DEC_BATCH, DEPTH, PAST_LEN, 128).astype(BF16),
              cache_win_v.reshape(DEC_BATCH, DEPTH, PAST_LEN, 128).astype(BF16))

    kd, vd, kw, vw = [], [], [], []
    for l in range(DEPTH):
        xp1, hp2, kd_l, vd_l, kw_l, vw_l = _mixer_ctx(xp, mod, P, l)
        kd.append(kd_l)
        vd.append(vd_l)
        kw.append(kw_l)
        vw.append(vw_l)
        xp = _ffn(hp2, xp1, mod, P, l, SEQ, 0, 1)

        ql, kvb = _inproj_dec(xs, mod, P["g_pre_mix"], P["w_in"][l], l, rope_tabs)
        nxt = list(big.values()) if l + 1 < DEPTH else []
        xs1, hs2, *cast = _mixer_dec(ql, kvb, caches, xs, mod, P, l, nxt, l + 1)
        for name, w in zip(big, cast):
            P[name].append(w)
        xs = _ffn(hs2, xs1, mod, P, l, DEC_SEQ, 1, DEC_BATCH)

    return (xp.reshape(BATCH, SEQ, d), xs.reshape(DEC_BATCH, DEC_SEQ, d),
            _cache_from_feature_major(kd, 4), _cache_from_feature_major(vd, 4),
            _cache_from_feature_major(kw, 2), _cache_from_feature_major(vw, 2))
```

```python
import functools
import math

import jax
import jax.numpy as jnp
import numpy as np
from jax import lax
from jax.experimental import pallas as pl
from jax.experimental.pallas import tpu as pltpu

F32 = jnp.float32
BF16 = jnp.bfloat16

D_MODEL = 1024
BATCH = 16
SEQ = 256
DEPTH = 2
DEC_BATCH = 2
DEC_SEQ = 2048
PAST_LEN = 256
GRID_W = 64
POOL_WINDOWS = (2, 4, 8, 16)
DIFF_QK = 32
WIN_HD = 64
WINDOW = 128
CONV_K = 31
IN_WIDTH = 2048
D_FF = 2816
ROPE_BASE = 10000.0
EPS = 1e-6
LOG2E = 1.4426950408889634

LANES = 128
SUBLANES = 8
HALO = 16
FFN_HALO = 16
FFN_CHUNK = 256
KV_WIDTH = 768
QL_WIDTH = 1280
VMEM_LIMIT = 56 * 1024 * 1024

TM_IN = 512
CTX_SEQS_PER_STEP = 2
TQ_DEC = 256
TM_FFN = 1024


def _rms(x, g):
    return x * lax.rsqrt(jnp.mean(x * x, axis=-1, keepdims=True) + EPS) * g


def _dot(a, b):
    return jnp.dot(a, b, preferred_element_type=F32)


def _dot_nt(a, b):
    return lax.dot_general(a, b, (((1,), (1,)), ((), ())), preferred_element_type=F32)


def _lane_range(shape, lo, hi):
    lane = lax.broadcasted_iota(jnp.int32, shape, 1)
    return (lane >= lo) & (lane < hi)


def _swap_middle_heads(t0, t1):
    lo = _lane_range(t0.shape, 0, 64)
    return (jnp.where(lo, t0, pltpu.roll(t1, 64, 1)),
            jnp.where(lo, pltpu.roll(t0, 64, 1), t1))


def _modulated_norm(x, g, mod, shift_col, scale_col):
    d = D_MODEL
    return (_rms(x, g) * (1.0 + mod[:, scale_col * d:(scale_col + 1) * d])
            + mod[:, shift_col * d:(shift_col + 1) * d])


def _layer_spec(shape, l):
    nz = len(shape)
    return pl.BlockSpec((None,) + tuple(shape), lambda *_: (l,) + (0,) * nz,
                        pipeline_mode=pl.Buffered(1))


def _whole_spec(shape):
    return pl.BlockSpec(tuple(shape), lambda *_: (0,) * len(shape), pipeline_mode=pl.Buffered(1))


def _rows_spec(width):
    return pl.BlockSpec((DEPTH, width), lambda *_: (0, 0), pipeline_mode=pl.Buffered(1))


def _mod_spec(l):
    return pl.BlockSpec((None, 8, 6 * D_MODEL), lambda *_: (l, 0, 0), pipeline_mode=pl.Buffered(1))


def _ada_kernel(c_ref, w_ref, b_ref, o_ref):
    c = c_ref[...]
    s = (c * jax.nn.sigmoid(c)).astype(BF16)
    o_ref[...] = _dot(s, w_ref[...].astype(BF16)) + b_ref[pl.ds(pl.program_id(0), 1), :]


def _ada(cond8, w_ada, b_ada):
    tn = 1024
    return pl.pallas_call(
        _ada_kernel,
        out_shape=jax.ShapeDtypeStruct((DEPTH, 8, 6 * D_MODEL), F32),
        grid=(DEPTH, 6 * D_MODEL // tn),
        in_specs=[
            pl.BlockSpec((8, D_MODEL), lambda l, j: (0, 0)),
            pl.BlockSpec((None, D_MODEL, tn), lambda l, j: (l, 0, j)),
            pl.BlockSpec((DEPTH, tn), lambda l, j: (0, j)),
        ],
        out_specs=pl.BlockSpec((None, 8, tn), lambda l, j: (l, 0, j)),
        compiler_params=pltpu.CompilerParams(
            dimension_semantics=("arbitrary", "arbitrary"), vmem_limit_bytes=VMEM_LIMIT),
        name="ada_mod",
    )(cond8, w_ada, b_ada)


def _rope_tile(v, cos, sin_signed, half):
    is_a = (lax.broadcasted_iota(jnp.int32, v.shape, 1) % (2 * half)) < half
    partner = jnp.where(is_a, pltpu.roll(v, LANES - half, 1), pltpu.roll(v, half, 1))
    return v * cos + partner * sin_signed


def _inproj_dec_kernel(x_ref, mod_ref, g_ref, w_ref, cd_ref, sd_ref, cw_ref, sw_ref, ql_ref, kv_ref,
                       *, l, tiles_per_b):
    mod = mod_ref[pl.ds(1 + pl.program_id(0) // tiles_per_b, 1), :]
    h = _modulated_norm(x_ref[...], g_ref[l:l + 1, :], mod, 0, 1)
    proj = _dot(h.astype(BF16), w_ref[...])
    tiles = [proj[:, i * LANES:(i + 1) * LANES] for i in range(IN_WIDTH // LANES)]
    cd, sd, cw, sw = cd_ref[...], sd_ref[...], cw_ref[...], sw_ref[...]
    for i in (2, 3, 4, 5):
        tiles[i] = _rope_tile(tiles[i], cd, sd, DIFF_QK // 2)
    for i in (8, 9, 10):
        tiles[i] = _rope_tile(tiles[i], cw, sw, WIN_HD // 2)
    tiles[8], tiles[9] = _swap_middle_heads(tiles[8], tiles[9])
    for j, i in enumerate((12, 13, 14, 15, 0, 1, 2, 3, 8, 9)):
        ql_ref[:, j * LANES:(j + 1) * LANES] = tiles[i]
    for j, i in enumerate((4, 5, 6, 7, 10, 11)):
        kv_ref[:, j * LANES:(j + 1) * LANES] = tiles[i].astype(BF16)


def _inproj_dec(x, mod, g, w_in_b, l, rope_tabs):
    rows = x.shape[0]
    tm = TM_IN
    tiles_per_b = DEC_SEQ // tm
    in_specs = [
        pl.BlockSpec((tm, D_MODEL), lambda i: (i, 0)),
        _mod_spec(l),
        _rows_spec(D_MODEL),
        _whole_spec((D_MODEL, IN_WIDTH)),
    ] + [pl.BlockSpec((tm, LANES), lambda i: (i % tiles_per_b, 0)) for _ in rope_tabs]
    return pl.pallas_call(
        functools.partial(_inproj_dec_kernel, l=l, tiles_per_b=tiles_per_b),
        out_shape=[jax.ShapeDtypeStruct((rows, QL_WIDTH), F32),
                   jax.ShapeDtypeStruct((rows, KV_WIDTH), BF16)],
        grid=(rows // tm,),
        in_specs=in_specs,
        out_specs=[pl.BlockSpec((tm, QL_WIDTH), lambda i: (i, 0)),
                   pl.BlockSpec((tm, KV_WIDTH), lambda i: (i, 0))],
        compiler_params=pltpu.CompilerParams(
            dimension_semantics=("arbitrary",), vmem_limit_bytes=VMEM_LIMIT),
        name="inproj_dec",
    )(x, mod, g, w_in_b, *rope_tabs)


def _pool(xpad, u, invcnt, tq):
    rows = xpad.shape[0]
    p = xpad
    sums = []
    for step, w in zip((1, 2, 4, 8), POOL_WINDOWS):
        p = p + pltpu.roll(p, step, 0)
        off = HALO + w // 2 - 1
        r = off % SUBLANES
        sh = p if r == 0 else pltpu.roll(p, rows - r, 0)
        sums.append(sh[off - r:off - r + tq])
    shape = u.shape
    s = jnp.where(_lane_range(shape, 0, 64), sums[0],
                  jnp.where(_lane_range(shape, 64, 128), sums[1],
                            jnp.where(_lane_range(shape, 128, 192), sums[2], sums[3])))
    return s * invcnt - u


def _conv_module(apad, gpad, dw_ref, dwb, lng, lnb, tq):
    u = apad * jax.nn.sigmoid(gpad)
    rows = u.shape[0]
    acc = jnp.broadcast_to(dwb, (tq, u.shape[1]))
    for r in range(SUBLANES):
        ur = u if r == 0 else pltpu.roll(u, rows - r, 0)
        for a8 in range(4):
            k = SUBLANES * a8 + r - 1
            if 0 <= k < CONV_K:
                acc = acc + ur[SUBLANES * a8:SUBLANES * a8 + tq] * dw_ref[k:k + 1, :]
    mu = jnp.mean(acc, axis=-1, keepdims=True)
    xc = acc - mu
    var = jnp.mean(xc * xc, axis=-1, keepdims=True)
    y = xc * lax.rsqrt(var + EPS) * lng + lnb
    return y * jax.nn.sigmoid(y)


def _ones_outside(v, lo, hi):
    m = jnp.where(_lane_range((1, v.shape[1]), lo, hi), 1.0, 0.0).astype(BF16)
    return v * m + (1.0 - m)


def _diff_attn(dq, ks, vps, lam, gain, tq, group):
    dqs = dq * (DIFF_QK ** -0.5 * LOG2E)
    ytiles = [jnp.zeros((tq, LANES), F32), jnp.zeros((tq, LANES), F32)]
    for h0 in range(0, 4, group):
        qs = []
        for h in range(h0, h0 + group):
            for lo in (h * 64, h * 64 + DIFF_QK):
                qs.append(jnp.where(_lane_range(dqs.shape, lo, lo + DIFF_QK), dqs, 0.0))
        qg = jnp.concatenate(qs, axis=0).astype(BF16)
        ss = [_dot_nt(qg, k) for k in ks]
        mx = functools.reduce(jnp.maximum, [jnp.max(s, axis=-1, keepdims=True) for s in ss])
        es = [jnp.exp2(s - mx).astype(BF16) for s in ss]
        for h in range(h0, h0 + group):
            r0 = (h - h0) * 2 * tq
            o = None
            for e, vp in zip(es, vps(h)):
                t = _dot(e[r0:r0 + 2 * tq], vp)
                o = t if o is None else o + t
            tile = h // 2
            ot = o[:, tile * LANES:(tile + 1) * LANES]
            hlo = (h % 2) * 64
            den = jnp.max(jnp.where(_lane_range(ot.shape, 64 - hlo, 128 - hlo), ot, 0.0), axis=-1, keepdims=True)
            r = ot / den
            od = jnp.where(_lane_range((tq, LANES), hlo, hlo + 64), r[:tq] - lam * r[tq:], 0.0)
            ms = jnp.sum(od * od, axis=-1, keepdims=True) * (1.0 / 64.0)
            ytiles[tile] = ytiles[tile] + od * lax.rsqrt(ms + EPS)
    return jnp.concatenate(ytiles, axis=1) * gain


def _win_attn(wq, ks, masks, vs, sink_ref, sink_base, tq, group):
    wqs = wq * (WIN_HD ** -0.5 * LOG2E)
    t0, t1 = wqs[:, :LANES], wqs[:, LANES:]
    y0 = jnp.zeros((tq, LANES), F32)
    y1 = jnp.zeros((tq, LANES), F32)
    rows = 2 * tq * group
    row = lax.broadcasted_iota(jnp.int32, (rows, 1), 0)
    for j0 in range(0, 2, group):
        qs = []
        sk = jnp.zeros((rows, 1), F32)
        for j in range(j0, j0 + group):
            lm = _lane_range(t0.shape, j * 64, j * 64 + 64)
            qs += [jnp.where(lm, t0, 0.0), jnp.where(lm, t1, 0.0)]
            for g in range(2):
                r0 = ((j - j0) * 2 + g) * tq
                sk = jnp.where((row >= r0) & (row < r0 + tq), sink_ref[sink_base + 2 * j + g], sk)
        sk = sk * LOG2E
        qg = jnp.concatenate(qs, axis=0).astype(BF16)
        ss = []
        for k, m in zip(ks, masks):
            s = _dot_nt(qg, k)
            if m is not None:
                s = jnp.where(m, s, -1e30)
            ss.append(s)
        mx = functools.reduce(jnp.maximum, [jnp.max(s, axis=-1, keepdims=True) for s in ss])
        mx = jnp.maximum(mx, sk)
        es = [jnp.exp2(s - mx).astype(BF16) for s in ss]
        esink = jnp.exp2(sk - mx)
        for j in range(j0, j0 + group):
            r0 = (j - j0) * 2 * tq
            o = None
            for e, v in zip(es, vs):
                t = _dot(e[r0:r0 + 2 * tq], _ones_outside(v, j * 64, j * 64 + 64))
                o = t if o is None else o + t
            other = _lane_range(o.shape, 64 - j * 64, 128 - j * 64)
            den = jnp.max(jnp.where(other, o, 0.0), axis=-1, keepdims=True)
            r = o / (den + esink[r0:r0 + 2 * tq])
            lm = _lane_range((tq, LANES), j * 64, j * 64 + 64)
            y0 = y0 + jnp.where(lm, r[:tq], 0.0)
            y1 = y1 + jnp.where(lm, r[tq:], 0.0)
    return _swap_middle_heads(y0, y1)


def _mixer_kernel(*refs, dec, l, tq, n_cast=0):
    if dec:
        (ql_ref, conv_p_ref, conv_n_ref, pool_p_ref, pool_n_ref, kv_ref,
         cdk_ref, cdv_ref, cwk_ref, cwv_ref, x_ref, mod_ref, *rest) = refs
    else:
        (x_ref, mod_ref, gin_ref, win_ref, *rest) = refs
    (invcnt_ref, wbd_ref, pscale_ref, lamp_ref, subln_ref, sink_ref,
     dw_ref, dwb_ref, lng_ref, lnb_ref, pw_ref, wout_ref, gpost_ref, gpre_ref, *tail) = rest
    cast_in, outs = tail[:n_cast], tail[n_cast:]
    d = D_MODEL
    mod = mod_ref[pl.ds(1 + pl.program_id(0), 1), :] if dec else mod_ref[0:1, :]
    x = x_ref[...]
    row = lambda ref: ref[l:l + 1, :]

    lam_init = 0.8 - 0.6 * math.exp(-0.3 * l)
    lp = lamp_ref[...]
    lam = (jnp.exp(jnp.sum(lp[0:1] * lp[1:2], axis=-1, keepdims=True))
           - jnp.exp(jnp.sum(lp[2:3] * lp[3:4], axis=-1, keepdims=True)) + lam_init)
    gain = row(subln_ref) * (1.0 - lam_init)

    zpad = jnp.zeros((HALO, 256), F32)
    pieces = []
    if dec:
        x1_ref, h2_ref, *cast_out, vp_ref = outs
        ca, cg = ql_ref[:, 0:256], ql_ref[:, 256:512]
        u_pool, dq, wq = ql_ref[:, 512:768], ql_ref[:, 768:1024], ql_ref[:, 1024:1280]
        q = pl.program_id(1)
        nq = pl.num_programs(1)
        pv = (q > 0).astype(F32)
        nv = (q < nq - 1).astype(F32)
        pool_pad = jnp.concatenate([pool_p_ref[...] * pv, u_pool, pool_n_ref[...] * nv], axis=0)
        cp = conv_p_ref[...] * pv
        cn = conv_n_ref[...] * nv
        apad = jnp.concatenate([cp[:, :256], ca, cn[:, :256]], axis=0)
        gpad = jnp.concatenate([cp[:, 256:], cg, cn[:, 256:]], axis=0)

        @pl.when(q == 0)
        def _():
            for h in range(4):
                vp_ref[h, 0:PAST_LEN, :] = _ones_outside(cdv_ref[...], h * 64, h * 64 + 64)
                vp_ref[h, PAST_LEN:, :] = _ones_outside(kv_ref[:, 256:512], h * 64, h * 64 + 64)

        dks = [cdk_ref[...], kv_ref[:, 0:256]]
        vps = lambda h: [vp_ref[h, 0:PAST_LEN, :], vp_ref[h, PAST_LEN:, :]]
        band = tq + 2 * WINDOW
        start = pl.multiple_of(jnp.clip(q * tq - WINDOW, 0, DEC_SEQ - band), WINDOW)
        wkb = kv_ref[pl.ds(start, band), 512:640]
        wvb = kv_ref[pl.ds(start, band), 640:768]
        qpos = q * tq + (lax.broadcasted_iota(jnp.int32, (2 * tq, band), 0) & (tq - 1))
        kpos = start + lax.broadcasted_iota(jnp.int32, (2 * tq, band), 1)
        wmask = jnp.abs(qpos - kpos) <= WINDOW
        yw = _win_attn(wq, [wkb, cwk_ref[...]], [wmask, None], [wvb, cwv_ref[...]], sink_ref, 4 * l, tq, group=1)
        pieces.append((_pool(pool_pad, u_pool, invcnt_ref[...], tq),
                       _diff_attn(dq, dks, vps, lam, gain, tq, group=1), yw[0], yw[1],
                       _conv_module(apad, gpad, dw_ref, row(dwb_ref), row(lng_ref), row(lnb_ref), tq)))
    else:
        x1_ref, h2_ref, kd_ref, vd_ref, kw_ref, vw_ref, *cast_out = outs
        h = _modulated_norm(x, row(gin_ref), mod, 0, 1)
        proj = _dot(h.astype(BF16), win_ref[...])
        for s in range(x.shape[0] // tq):
            ps = proj[s * tq:(s + 1) * tq]
            u_pool, dq = ps[:, 0:256], ps[:, 256:512]
            dk, dv = ps[:, 512:768], ps[:, 768:1024]
            wk, wv = ps[:, 1280:1408], ps[:, 1408:1536]
            ca, cg = ps[:, 1536:1792], ps[:, 1792:2048]
            wq = jnp.concatenate(_swap_middle_heads(ps[:, 1024:1152], ps[:, 1152:1280]), axis=1)
            kd_ref[s] = dk.T
            vd_ref[s] = dv.T
            kw_ref[s] = wk.T
            vw_ref[s] = wv.T
            pool_pad = jnp.concatenate([zpad, u_pool, zpad], axis=0)
            apad = jnp.concatenate([zpad, ca, zpad], axis=0)
            gpad = jnp.concatenate([zpad, cg, zpad], axis=0)
            dvb = dv.astype(BF16)
            vps = lambda h, dvb=dvb: [_ones_outside(dvb, h * 64, h * 64 + 64)]
            yw = _win_attn(wq, [wk.astype(BF16)], [None], [wv.astype(BF16)], sink_ref, 4 * l, tq, group=2)
            pieces.append((_pool(pool_pad, u_pool, invcnt_ref[...], tq),
                           _diff_attn(dq, [dk.astype(BF16)], vps, lam, gain, tq, group=4), yw[0], yw[1],
                           _conv_module(apad, gpad, dw_ref, row(dwb_ref), row(lng_ref), row(lnb_ref), tq)))

    for src, dst in zip(cast_in, cast_out):
        dst[...] = src[...].astype(BF16)

    cat = lambda k: jnp.concatenate([t[k] for t in pieces], axis=0).astype(BF16)
    y_pool = _dot(cat(0), wbd_ref[...]) * row(pscale_ref)
    y_conv = _dot(cat(4), pw_ref[...])
    o = (_dot(y_pool.astype(BF16), wout_ref[0:256, :])
         + _dot(cat(1), wout_ref[256:512, :])
         + _dot(cat(2), wout_ref[512:640, :])
         + _dot(cat(3), wout_ref[640:768, :])
         + _dot(y_conv.astype(BF16), wout_ref[768:1024, :]))
    x1 = x + mod[:, 2 * d:3 * d] * _rms(o, row(gpost_ref))
    x1_ref[...] = x1
    h2_ref[...] = _modulated_norm(x1, row(gpre_ref), mod, 3, 4).astype(BF16)


def _mixer_param_specs(l):
    return [
        _layer_spec((256, 256), l),
        _rows_spec(256),
        _layer_spec((4, DIFF_QK), l),
        _rows_spec(256),
        pl.BlockSpec(memory_space=pltpu.SMEM),
        _layer_spec((CONV_K, 256), l),
        _rows_spec(256),
        _rows_spec(256),
        _rows_spec(256),
        _layer_spec((256, 256), l),
        _whole_spec((D_MODEL, D_MODEL)),
        _rows_spec(D_MODEL),
        _rows_spec(D_MODEL),
    ]


def _mixer_param_args(P, l):
    return [P["pool_wbd"], P["pool_scale"], P["lam_p"], P["subln"], P["sink"],
            P["conv_dw"], P["conv_dw_b"], P["conv_ln_g"], P["conv_ln_b"], P["conv_pw"], P["w_out"][l],
            P["g_post_mix"], P["g_pre_ffn"]]


def _cast_specs(to_cast, cast_layer, steps, step_of):
    ins = [pl.BlockSpec((None, w.shape[1] // steps, w.shape[2]),
                        lambda *g: (cast_layer, step_of(*g), 0)) for w in to_cast]
    outs = [pl.BlockSpec((w.shape[1] // steps, w.shape[2]), lambda *g: (step_of(*g), 0)) for w in to_cast]
    shapes = [jax.ShapeDtypeStruct(w.shape[1:], BF16) for w in to_cast]
    return ins, outs, shapes


def _mixer_ctx(x, mod, P, l, to_cast=(), cast_layer=0):
    rows = x.shape[0]
    tq = SEQ
    ns = CTX_SEQS_PER_STEP
    row_map = lambda i: (i, 0)
    seq_map = lambda i: (i, 0, 0)
    cast_in, cast_out, cast_shapes = _cast_specs(to_cast, cast_layer, rows // (ns * tq), lambda i: i)
    in_specs = [
        pl.BlockSpec((ns * tq, D_MODEL), row_map),
        _mod_spec(l),
        _rows_spec(D_MODEL),
        _whole_spec((D_MODEL, IN_WIDTH)),
        pl.BlockSpec((tq, 256), lambda i: (0, 0)),
    ] + _mixer_param_specs(l)
    args = [x, mod, P["g_pre_mix"], P["w_in"][l], P["invcnt_ctx"]] + _mixer_param_args(P, l)
    return pl.pallas_call(
        functools.partial(_mixer_kernel, dec=False, l=l, tq=tq, n_cast=len(to_cast)),
        out_shape=[jax.ShapeDtypeStruct((rows, D_MODEL), F32),
                   jax.ShapeDtypeStruct((rows, D_MODEL), BF16),
                   jax.ShapeDtypeStruct((BATCH, 256, tq), F32),
                   jax.ShapeDtypeStruct((BATCH, 256, tq), F32),
                   jax.ShapeDtypeStruct((BATCH, 128, tq), F32),
                   jax.ShapeDtypeStruct((BATCH, 128, tq), F32)] + cast_shapes,
        grid=(rows // (ns * tq),),
        in_specs=in_specs + cast_in,
        out_specs=[pl.BlockSpec((ns * tq, D_MODEL), row_map), pl.BlockSpec((ns * tq, D_MODEL), row_map),
                   pl.BlockSpec((ns, 256, tq), seq_map), pl.BlockSpec((ns, 256, tq), seq_map),
                   pl.BlockSpec((ns, 128, tq), seq_map), pl.BlockSpec((ns, 128, tq), seq_map)] + cast_out,
        compiler_params=pltpu.CompilerParams(
            dimension_semantics=("arbitrary",), vmem_limit_bytes=VMEM_LIMIT),
        name="mixer_ctx",
    )(*args, *to_cast)


def _mixer_dec(ql, kvb, caches, x, mod, P, l, to_cast=(), cast_layer=0):
    rows = x.shape[0]
    tq = TQ_DEC
    nq = DEC_SEQ // tq
    cast_in_specs, cast_out_specs, cast_shapes = _cast_specs(
        to_cast, cast_layer, DEC_BATCH * nq, lambda b, q: b * nq + q)
    hb = tq // HALO
    nhb = rows // HALO
    row_map = lambda b, q: (b * nq + q, 0)

    def prev_map(col):
        return lambda b, q: (jnp.maximum((b * nq + q) * hb - 1, 0), col)

    def next_map(col):
        return lambda b, q: (jnp.minimum((b * nq + q + 1) * hb, nhb - 1), col)

    cache_map = lambda b, q: (b, l, 0, 0)
    in_specs = [
        pl.BlockSpec((tq, QL_WIDTH), row_map),
        pl.BlockSpec((HALO, 512), prev_map(0)),
        pl.BlockSpec((HALO, 512), next_map(0)),
        pl.BlockSpec((HALO, 256), prev_map(2)),
        pl.BlockSpec((HALO, 256), next_map(2)),
        pl.BlockSpec((DEC_SEQ, KV_WIDTH), lambda b, q: (b, 0)),
        pl.BlockSpec((None, None, PAST_LEN, 256), cache_map),
        pl.BlockSpec((None, None, PAST_LEN, 256), cache_map),
        pl.BlockSpec((None, None, PAST_LEN, 128), cache_map),
        pl.BlockSpec((None, None, PAST_LEN, 128), cache_map),
        pl.BlockSpec((tq, D_MODEL), row_map),
        _mod_spec(l),
        pl.BlockSpec((tq, 256), lambda b, q: (q, 0)),
    ] + _mixer_param_specs(l)
    args = [ql, ql, ql, ql, ql, kvb, *caches, x, mod, P["invcnt_dec"]] + _mixer_param_args(P, l)
    return pl.pallas_call(
        functools.partial(_mixer_kernel, dec=True, l=l, tq=tq, n_cast=len(to_cast)),
        out_shape=[jax.ShapeDtypeStruct((rows, D_MODEL), F32),
                   jax.ShapeDtypeStruct((rows, D_MODEL), BF16)]
        + cast_shapes,
        grid=(DEC_BATCH, nq),
        in_specs=in_specs + cast_in_specs,
        out_specs=[pl.BlockSpec((tq, D_MODEL), row_map), pl.BlockSpec((tq, D_MODEL), row_map)] + cast_out_specs,
        scratch_shapes=[pltpu.VMEM((4, PAST_LEN + DEC_SEQ, 256), BF16)],
        compiler_params=pltpu.CompilerParams(
            dimension_semantics=("arbitrary", "arbitrary"), vmem_limit_bytes=VMEM_LIMIT),
        name="mixer_dec",
    )(*args, *to_cast)


def _ffn_kernel(*refs, tm, seq_len, l, mod_row0, tiles_per_b, halo):
    if halo:
        h_ref, hp_ref, hn_ref, *rest = refs
        hx = jnp.concatenate([hp_ref[...], h_ref[...], hn_ref[...]], axis=0)
    else:
        h_ref, *rest = refs
        hx = h_ref[...]
    x_ref, mod_ref, wup_ref, dw_ref, dwb_ref, wd_ref, gpost_ref, o_ref, act_ref = rest
    i = pl.program_id(0)
    pos = (i * tm + lax.broadcasted_iota(jnp.int32, (tm, 1), 0)) % seq_len
    has_left = pos != 0
    has_right = pos != seq_len - 1
    for j in range(D_FF // FFN_CHUNK):
        ys = []
        for part in range(2):
            c0 = part * D_FF + j * FFN_CHUNK
            u = _dot(hx, wup_ref[:, c0:c0 + FFN_CHUNK])
            w = dw_ref[:, c0:c0 + FFN_CHUNK]
            if halo:
                below, mid, above = (u[FFN_HALO - 1:FFN_HALO - 1 + tm], u[FFN_HALO:FFN_HALO + tm],
                                     u[FFN_HALO + 1:FFN_HALO + 1 + tm])
            else:
                below, mid, above = pltpu.roll(u, 1, 0), u, pltpu.roll(u, tm - 1, 0)
            ys.append(jnp.where(has_left, below, 0.0) * w[0:1] + mid * w[1:2]
                      + jnp.where(has_right, above, 0.0) * w[2:3]
                      + dwb_ref[l:l + 1, c0:c0 + FFN_CHUNK])
        gate, up = ys
        act_ref[:, j * FFN_CHUNK:(j + 1) * FFN_CHUNK] = (gate * jax.nn.sigmoid(gate) * up).astype(BF16)
    o = _dot(act_ref[...], wd_ref[...])
    g2 = mod_ref[pl.ds(mod_row0 + i // tiles_per_b, 1), 5 * D_MODEL:6 * D_MODEL]
    o_ref[...] = x_ref[...] + g2 * _rms(o, gpost_ref[l:l + 1, :])


def _ffn(h2, x1, mod, P, l, seq_len, mod_row0, nb):
    rows = x1.shape[0]
    tm = TM_FFN
    tiles_per_b = rows // nb // tm
    hb = tm // FFN_HALO
    nhb = rows // FFN_HALO
    halo = tm % seq_len != 0
    h_specs = [pl.BlockSpec((tm, D_MODEL), lambda i: (i, 0))]
    if halo:
        h_specs += [pl.BlockSpec((FFN_HALO, D_MODEL), lambda i: (jnp.maximum(i * hb - 1, 0), 0)),
                    pl.BlockSpec((FFN_HALO, D_MODEL), lambda i: (jnp.minimum((i + 1) * hb, nhb - 1), 0))]
    return pl.pallas_call(
        functools.partial(_ffn_kernel, tm=tm, seq_len=seq_len, l=l, mod_row0=mod_row0,
                          tiles_per_b=tiles_per_b, halo=halo),
        out_shape=jax.ShapeDtypeStruct((rows, D_MODEL), F32),
        grid=(rows // tm,),
        in_specs=h_specs + [
            pl.BlockSpec((tm, D_MODEL), lambda i: (i, 0)),
            _mod_spec(l),
            _whole_spec((D_MODEL, 2 * D_FF)),
            _layer_spec((3, 2 * D_FF), l),
            _rows_spec(2 * D_FF),
            _whole_spec((D_FF, D_MODEL)),
            _rows_spec(D_MODEL),
        ],
        out_specs=pl.BlockSpec((tm, D_MODEL), lambda i: (i, 0)),
        scratch_shapes=[pltpu.VMEM((tm, D_FF), BF16)],
        compiler_params=pltpu.CompilerParams(
            dimension_semantics=("arbitrary",), vmem_limit_bytes=VMEM_LIMIT),
        name="ffn",
    )(*([h2] * len(h_specs)), x1, mod, P["ffn_up"][l], P["ffn_dw"], P["ffn_dw_b"], P["ffn_down"][l],
      P["g_post_ffn"])


def _pool_inv_counts(seq_len):
    inv = np.zeros((seq_len, 256), np.float32)
    pos = np.arange(seq_len)
    for g, w in enumerate(POOL_WINDOWS):
        cnt = np.clip(pos - w // 2 + w, 0, seq_len) - np.clip(pos - w // 2, 0, seq_len)
        inv[:, g * 64:(g + 1) * 64] = (1.0 / cnt.astype(np.float64))[:, None].astype(np.float32)
    return jnp.asarray(inv, F32)


def _rope_tables(seq_len, dim):
    rows = seq_len // GRID_W
    r = np.repeat(np.arange(rows), GRID_W).astype(np.float32)
    col = np.tile(np.arange(GRID_W), rows).astype(np.float32)
    n = dim // 4
    inv = (ROPE_BASE ** (-np.arange(n) / n)).astype(np.float32)
    ang = np.concatenate([r[:, None] * inv[None], col[:, None] * inv[None]], axis=-1)
    cos, sin = np.cos(ang), np.sin(ang)
    reps = LANES // dim
    cos_t = np.tile(np.concatenate([cos, cos], axis=-1), (1, reps))
    sin_t = np.tile(np.concatenate([-sin, sin], axis=-1), (1, reps))
    return jnp.asarray(cos_t, F32), jnp.asarray(sin_t, F32)


def _cache_from_feature_major(per_layer, heads):
    t = jnp.stack(per_layer, axis=1).reshape(BATCH, DEPTH, heads, 64, SEQ)
    return jnp.transpose(t, (0, 1, 4, 2, 3))


def kernel(x_prompt, x_sample, c, cache_diff_k, cache_diff_v, cache_win_k, cache_win_v, c_ctx, w_ada, b_ada, g_pre_mix, g_post_mix, g_pre_ffn, g_post_ffn, w_in, w_out, pool_w, pool_scale, diff_lq1, diff_lk1, diff_lq2, diff_lk2, diff_subln, win_sink, conv_dw, conv_dw_b, conv_ln_g, conv_ln_b, conv_pw, ffn_up, ffn_dw, ffn_dw_b, ffn_down):
    d = D_MODEL
    xp = x_prompt.reshape(BATCH * SEQ, d)
    xs = x_sample.reshape(DEC_BATCH * DEC_SEQ, d)

    cond8 = jnp.concatenate([c_ctx[None, :], c, jnp.zeros((8 - 1 - DEC_BATCH, d), F32)], axis=0)
    mod = _ada(cond8, w_ada, b_ada)

    eye4 = jnp.eye(4, dtype=F32)
    P = dict(
        invcnt_ctx=_pool_inv_counts(SEQ), invcnt_dec=_pool_inv_counts(DEC_SEQ),
        g_pre_mix=g_pre_mix,
        pool_wbd=jnp.einsum("lgcd,gh->lgchd", pool_w, eye4).reshape(DEPTH, 256, 256).astype(BF16),
        pool_scale=pool_scale,
        lam_p=jnp.stack([diff_lq1, diff_lk1, diff_lq2, diff_lk2], axis=1),
        subln=jnp.tile(diff_subln, (1, 4)),
        sink=win_sink.reshape(DEPTH * 4),
        conv_dw=conv_dw, conv_dw_b=conv_dw_b, conv_ln_g=conv_ln_g, conv_ln_b=conv_ln_b,
        conv_pw=conv_pw.astype(BF16),
        g_post_mix=g_post_mix, g_pre_ffn=g_pre_ffn,
        ffn_dw=ffn_dw, ffn_dw_b=ffn_dw_b, g_post_ffn=g_post_ffn,
    )
    mix_w = dict(w_in=w_in, w_out=w_out)
    ffn_w = dict(ffn_up=ffn_up, ffn_down=ffn_down)
    for name, w in mix_w.items():
        P[name] = [w[0].astype(BF16)]
    for name in ffn_w:
        P[name] = []
    rope_tabs = _rope_tables(DEC_SEQ, DIFF_QK) + _rope_tables(DEC_SEQ, WIN_HD)
    caches = (cache_diff_k.reshape(DEC_BATCH, DEPTH, PAST_LEN, 256).astype(BF16),
              cache_diff_v.reshape(DEC_BATCH, DEPTH, PAST_LEN, 256).astype(BF16),
              cache_win_k.reshape(DEC_BATCH, DEPTH, PAST_LEN, 128).astype(BF16),
              cache_win_v.reshape(DEC_BATCH, DEPTH, PAST_LEN, 128).astype(BF16))

    kd, vd, kw, vw = [], [], [], []
    for l in range(DEPTH):
        first = list(ffn_w.values()) if l == 0 else []
        xp1, hp2, kd_l, vd_l, kw_l, vw_l, *cast = _mixer_ctx(xp, mod, P, l, first, l)
        for name, w in zip(ffn_w, cast):
            P[name].append(w)
        kd.append(kd_l)
        vd.append(vd_l)
        kw.append(kw_l)
        vw.append(vw_l)
        xp = _ffn(hp2, xp1, mod, P, l, SEQ, 0, 1)

        ql, kvb = _inproj_dec(xs, mod, P["g_pre_mix"], P["w_in"][l], l, rope_tabs)
        big = {**mix_w, **ffn_w}
        nxt = list(big.values()) if l + 1 < DEPTH else []
        xs1, hs2, *cast = _mixer_dec(ql, kvb, caches, xs, mod, P, l, nxt, l + 1)
        for name, w in zip(big, cast):
            P[name].append(w)
        xs = _ffn(hs2, xs1, mod, P, l, DEC_SEQ, 1, DEC_BATCH)

    return (xp.reshape(BATCH, SEQ, d), xs.reshape(DEC_BATCH, DEC_SEQ, d),
            _cache_from_feature_major(kd, 4), _cache_from_feature_major(vd, 4),
            _cache_from_feature_major(kw, 2), _cache_from_feature_major(vw, 2))
```

```python
import functools
import math

import jax
import jax.numpy as jnp
import numpy as np
from jax import lax
from jax.experimental import pallas as pl
from jax.experimental.pallas import tpu as pltpu

F32 = jnp.float32
BF16 = jnp.bfloat16

D_MODEL = 1024
BATCH = 16
SEQ = 256
DEPTH = 2
DEC_BATCH = 2
DEC_SEQ = 2048
PAST_LEN = 256
GRID_W = 64
POOL_WINDOWS = (2, 4, 8, 16)
DIFF_QK = 32
WIN_HD = 64
WINDOW = 128
CONV_K = 31
IN_WIDTH = 2048
D_FF = 2816
ROPE_BASE = 10000.0
EPS = 1e-6
LOG2E = 1.4426950408889634

LANES = 128
SUBLANES = 8
HALO = 16
FFN_HALO = 16
FFN_CHUNK = 256
KV_WIDTH = 768
QL_WIDTH = 1280
VMEM_LIMIT = 56 * 1024 * 1024

TM_IN = 512
CTX_SEQS_PER_STEP = 2
TQ_DEC = 256
TM_FFN = 1024


def _rms(x, g):
    return x * lax.rsqrt(jnp.mean(x * x, axis=-1, keepdims=True) + EPS) * g


def _dot(a, b):
    return jnp.dot(a, b, preferred_element_type=F32)


def _dot_nt(a, b):
    return lax.dot_general(a, b, (((1,), (1,)), ((), ())), preferred_element_type=F32)


def _lane_range(shape, lo, hi):
    lane = lax.broadcasted_iota(jnp.int32, shape, 1)
    return (lane >= lo) & (lane < hi)


def _swap_middle_heads(t0, t1):
    lo = _lane_range(t0.shape, 0, 64)
    return (jnp.where(lo, t0, pltpu.roll(t1, 64, 1)),
            jnp.where(lo, pltpu.roll(t0, 64, 1), t1))


def _modulated_norm(x, g, mod, shift_col, scale_col):
    d = D_MODEL
    return (_rms(x, g) * (1.0 + mod[:, scale_col * d:(scale_col + 1) * d])
            + mod[:, shift_col * d:(shift_col + 1) * d])


def _layer_spec(shape, l):
    nz = len(shape)
    return pl.BlockSpec((None,) + tuple(shape), lambda *_: (l,) + (0,) * nz,
                        pipeline_mode=pl.Buffered(1))


def _whole_spec(shape):
    return pl.BlockSpec(tuple(shape), lambda *_: (0,) * len(shape), pipeline_mode=pl.Buffered(1))


def _rows_spec(width):
    return pl.BlockSpec((DEPTH, width), lambda *_: (0, 0), pipeline_mode=pl.Buffered(1))


def _mod_spec(l):
    return pl.BlockSpec((None, 8, 6 * D_MODEL), lambda *_: (l, 0, 0), pipeline_mode=pl.Buffered(1))


def _ada_kernel(c_ref, w_ref, b_ref, o_ref):
    c = c_ref[...]
    s = (c * jax.nn.sigmoid(c)).astype(BF16)
    o_ref[...] = _dot(s, w_ref[...].astype(BF16)) + b_ref[pl.ds(pl.program_id(0), 1), :]


def _ada(cond8, w_ada, b_ada):
    tn = 1024
    return pl.pallas_call(
        _ada_kernel,
        out_shape=jax.ShapeDtypeStruct((DEPTH, 8, 6 * D_MODEL), F32),
        grid=(DEPTH, 6 * D_MODEL // tn),
        in_specs=[
            pl.BlockSpec((8, D_MODEL), lambda l, j: (0, 0)),
            pl.BlockSpec((None, D_MODEL, tn), lambda l, j: (l, 0, j)),
            pl.BlockSpec((DEPTH, tn), lambda l, j: (0, j)),
        ],
        out_specs=pl.BlockSpec((None, 8, tn), lambda l, j: (l, 0, j)),
        compiler_params=pltpu.CompilerParams(
            dimension_semantics=("arbitrary", "arbitrary"), vmem_limit_bytes=VMEM_LIMIT),
        name="ada_mod",
    )(cond8, w_ada, b_ada)


def _rope_tile(v, cos, sin_signed, half):
    is_a = (lax.broadcasted_iota(jnp.int32, v.shape, 1) % (2 * half)) < half
    partner = jnp.where(is_a, pltpu.roll(v, LANES - half, 1), pltpu.roll(v, half, 1))
    return v * cos + partner * sin_signed


def _inproj_dec_kernel(x_ref, mod_ref, g_ref, w_ref, cd_ref, sd_ref, cw_ref, sw_ref, ql_ref, kv_ref,
                       *, l, tiles_per_b):
    mod = mod_ref[pl.ds(1 + pl.program_id(0) // tiles_per_b, 1), :]
    h = _modulated_norm(x_ref[...], g_ref[l:l + 1, :], mod, 0, 1)
    proj = _dot(h.astype(BF16), w_ref[...])
    tiles = [proj[:, i * LANES:(i + 1) * LANES] for i in range(IN_WIDTH // LANES)]
    cd, sd, cw, sw = cd_ref[...], sd_ref[...], cw_ref[...], sw_ref[...]
    for i in (2, 3, 4, 5):
        tiles[i] = _rope_tile(tiles[i], cd, sd, DIFF_QK // 2)
    for i in (8, 9, 10):
        tiles[i] = _rope_tile(tiles[i], cw, sw, WIN_HD // 2)
    tiles[8], tiles[9] = _swap_middle_heads(tiles[8], tiles[9])
    for j, i in enumerate((12, 13, 14, 15, 0, 1, 2, 3, 8, 9)):
        ql_ref[:, j * LANES:(j + 1) * LANES] = tiles[i]
    for j, i in enumerate((4, 5, 6, 7, 10, 11)):
        kv_ref[:, j * LANES:(j + 1) * LANES] = tiles[i].astype(BF16)


def _inproj_dec(x, mod, g, w_in_b, l, rope_tabs):
    rows = x.shape[0]
    tm = TM_IN
    tiles_per_b = DEC_SEQ // tm
    in_specs = [
        pl.BlockSpec((tm, D_MODEL), lambda i: (i, 0)),
        _mod_spec(l),
        _rows_spec(D_MODEL),
        _whole_spec((D_MODEL, IN_WIDTH)),
    ] + [pl.BlockSpec((tm, LANES), lambda i: (i % tiles_per_b, 0)) for _ in rope_tabs]
    return pl.pallas_call(
        functools.partial(_inproj_dec_kernel, l=l, tiles_per_b=tiles_per_b),
        out_shape=[jax.ShapeDtypeStruct((rows, QL_WIDTH), F32),
                   jax.ShapeDtypeStruct((rows, KV_WIDTH), BF16)],
        grid=(rows // tm,),
        in_specs=in_specs,
        out_specs=[pl.BlockSpec((tm, QL_WIDTH), lambda i: (i, 0)),
                   pl.BlockSpec((tm, KV_WIDTH), lambda i: (i, 0))],
        compiler_params=pltpu.CompilerParams(
            dimension_semantics=("arbitrary",), vmem_limit_bytes=VMEM_LIMIT),
        name="inproj_dec",
    )(x, mod, g, w_in_b, *rope_tabs)


def _pool(xpad, u, invcnt, tq):
    rows = xpad.shape[0]
    p = xpad
    sums = []
    for step, w in zip((1, 2, 4, 8), POOL_WINDOWS):
        p = p + pltpu.roll(p, step, 0)
        off = HALO + w // 2 - 1
        r = off % SUBLANES
        sh = p if r == 0 else pltpu.roll(p, rows - r, 0)
        sums.append(sh[off - r:off - r + tq])
    shape = u.shape
    s = jnp.where(_lane_range(shape, 0, 64), sums[0],
                  jnp.where(_lane_range(shape, 64, 128), sums[1],
                            jnp.where(_lane_range(shape, 128, 192), sums[2], sums[3])))
    return s * invcnt - u


def _conv_module(apad, gpad, dw_ref, dwb, lng, lnb, tq):
    u = apad * jax.nn.sigmoid(gpad)
    rows = u.shape[0]
    acc = jnp.broadcast_to(dwb, (tq, u.shape[1]))
    for r in range(SUBLANES):
        ur = u if r == 0 else pltpu.roll(u, rows - r, 0)
        for a8 in range(4):
            k = SUBLANES * a8 + r - 1
            if 0 <= k < CONV_K:
                acc = acc + ur[SUBLANES * a8:SUBLANES * a8 + tq] * dw_ref[k:k + 1, :]
    mu = jnp.mean(acc, axis=-1, keepdims=True)
    xc = acc - mu
    var = jnp.mean(xc * xc, axis=-1, keepdims=True)
    y = xc * lax.rsqrt(var + EPS) * lng + lnb
    return y * jax.nn.sigmoid(y)


def _ones_outside(v, lo, hi):
    m = jnp.where(_lane_range((1, v.shape[1]), lo, hi), 1.0, 0.0).astype(BF16)
    return v * m + (1.0 - m)


def _diff_attn(dq, ks, vps, lam, gain, tq, group):
    dqs = dq * (DIFF_QK ** -0.5 * LOG2E)
    ytiles = [jnp.zeros((tq, LANES), F32), jnp.zeros((tq, LANES), F32)]
    for h0 in range(0, 4, group):
        qs = []
        for h in range(h0, h0 + group):
            for lo in (h * 64, h * 64 + DIFF_QK):
                qs.append(jnp.where(_lane_range(dqs.shape, lo, lo + DIFF_QK), dqs, 0.0))
        qg = jnp.concatenate(qs, axis=0).astype(BF16)
        ss = [_dot_nt(qg, k) for k in ks]
        mx = functools.reduce(jnp.maximum, [jnp.max(s, axis=-1, keepdims=True) for s in ss])
        es = [jnp.exp2(s - mx).astype(BF16) for s in ss]
        for h in range(h0, h0 + group):
            r0 = (h - h0) * 2 * tq
            o = None
            for e, vp in zip(es, vps(h)):
                t = _dot(e[r0:r0 + 2 * tq], vp)
                o = t if o is None else o + t
            tile = h // 2
            ot = o[:, tile * LANES:(tile + 1) * LANES]
            hlo = (h % 2) * 64
            den = jnp.max(jnp.where(_lane_range(ot.shape, 64 - hlo, 128 - hlo), ot, 0.0), axis=-1, keepdims=True)
            r = ot / den
            od = jnp.where(_lane_range((tq, LANES), hlo, hlo + 64), r[:tq] - lam * r[tq:], 0.0)
            ms = jnp.sum(od * od, axis=-1, keepdims=True) * (1.0 / 64.0)
            ytiles[tile] = ytiles[tile] + od * lax.rsqrt(ms + EPS)
    return jnp.concatenate(ytiles, axis=1) * gain


def _win_attn(wq, ks, masks, vs, sink_ref, sink_base, tq, group):
    wqs = wq * (WIN_HD ** -0.5 * LOG2E)
    t0, t1 = wqs[:, :LANES], wqs[:, LANES:]
    y0 = jnp.zeros((tq, LANES), F32)
    y1 = jnp.zeros((tq, LANES), F32)
    rows = 2 * tq * group
    row = lax.broadcasted_iota(jnp.int32, (rows, 1), 0)
    for j0 in range(0, 2, group):
        qs = []
        sk = jnp.zeros((rows, 1), F32)
        for j in range(j0, j0 + group):
            lm = _lane_range(t0.shape, j * 64, j * 64 + 64)
            qs += [jnp.where(lm, t0, 0.0), jnp.where(lm, t1, 0.0)]
            for g in range(2):
                r0 = ((j - j0) * 2 + g) * tq
                sk = jnp.where((row >= r0) & (row < r0 + tq), sink_ref[sink_base + 2 * j + g], sk)
        sk = sk * LOG2E
        qg = jnp.concatenate(qs, axis=0).astype(BF16)
        ss = []
        for k, m in zip(ks, masks):
            s = _dot_nt(qg, k)
            if m is not None:
                s = jnp.where(m, s, -1e30)
            ss.append(s)
        mx = functools.reduce(jnp.maximum, [jnp.max(s, axis=-1, keepdims=True) for s in ss])
        mx = jnp.maximum(mx, sk)
        es = [jnp.exp2(s - mx).astype(BF16) for s in ss]
        esink = jnp.exp2(sk - mx)
        for j in range(j0, j0 + group):
            r0 = (j - j0) * 2 * tq
            o = None
            for e, v in zip(es, vs):
                t = _dot(e[r0:r0 + 2 * tq], _ones_outside(v, j * 64, j * 64 + 64))
                o = t if o is None else o + t
            other = _lane_range(o.shape, 64 - j * 64, 128 - j * 64)
            den = jnp.max(jnp.where(other, o, 0.0), axis=-1, keepdims=True)
            r = o / (den + esink[r0:r0 + 2 * tq])
            lm = _lane_range((tq, LANES), j * 64, j * 64 + 64)
            y0 = y0 + jnp.where(lm, r[:tq], 0.0)
            y1 = y1 + jnp.where(lm, r[tq:], 0.0)
    return _swap_middle_heads(y0, y1)


def _mixer_kernel(*refs, dec, l, tq, n_cast=0, n_alias=0):
    if dec:
        (ql_ref, conv_p_ref, conv_n_ref, pool_p_ref, pool_n_ref, kv_ref,
         cdk_ref, cdv_ref, cwk_ref, cwv_ref, x_ref, mod_ref, *rest) = refs
    else:
        (x_ref, mod_ref, gin_ref, win_ref, *rest) = refs
    (invcnt_ref, wbd_ref, pscale_ref, lamp_ref, subln_ref, sink_ref,
     dw_ref, dwb_ref, lng_ref, lnb_ref, pw_ref, wout_ref, gpost_ref, gpre_ref, *tail) = rest
    cast_in, outs = tail[:n_cast], tail[n_cast + n_alias:]
    d = D_MODEL
    mod = mod_ref[pl.ds(1 + pl.program_id(0), 1), :] if dec else mod_ref[0:1, :]
    x = x_ref[...]
    row = lambda ref: ref[l:l + 1, :]

    lam_init = 0.8 - 0.6 * math.exp(-0.3 * l)
    lp = lamp_ref[...]
    lam = (jnp.exp(jnp.sum(lp[0:1] * lp[1:2], axis=-1, keepdims=True))
           - jnp.exp(jnp.sum(lp[2:3] * lp[3:4], axis=-1, keepdims=True)) + lam_init)
    gain = row(subln_ref) * (1.0 - lam_init)

    zpad = jnp.zeros((HALO, 256), F32)
    pieces = []
    if dec:
        x1_ref, h2_ref, *cast_out, vp_ref = outs
        ca, cg = ql_ref[:, 0:256], ql_ref[:, 256:512]
        u_pool, dq, wq = ql_ref[:, 512:768], ql_ref[:, 768:1024], ql_ref[:, 1024:1280]
        q = pl.program_id(1)
        nq = pl.num_programs(1)
        pv = (q > 0).astype(F32)
        nv = (q < nq - 1).astype(F32)
        pool_pad = jnp.concatenate([pool_p_ref[...] * pv, u_pool, pool_n_ref[...] * nv], axis=0)
        cp = conv_p_ref[...] * pv
        cn = conv_n_ref[...] * nv
        apad = jnp.concatenate([cp[:, :256], ca, cn[:, :256]], axis=0)
        gpad = jnp.concatenate([cp[:, 256:], cg, cn[:, 256:]], axis=0)

        @pl.when(q == 0)
        def _():
            for h in range(4):
                vp_ref[h, 0:PAST_LEN, :] = _ones_outside(cdv_ref[...], h * 64, h * 64 + 64)
                vp_ref[h, PAST_LEN:, :] = _ones_outside(kv_ref[:, 256:512], h * 64, h * 64 + 64)

        dks = [cdk_ref[...], kv_ref[:, 0:256]]
        vps = lambda h: [vp_ref[h, 0:PAST_LEN, :], vp_ref[h, PAST_LEN:, :]]
        band = tq + 2 * WINDOW
        start = pl.multiple_of(jnp.clip(q * tq - WINDOW, 0, DEC_SEQ - band), WINDOW)
        wkb = kv_ref[pl.ds(start, band), 512:640]
        wvb = kv_ref[pl.ds(start, band), 640:768]
        qpos = q * tq + (lax.broadcasted_iota(jnp.int32, (2 * tq, band), 0) & (tq - 1))
        kpos = start + lax.broadcasted_iota(jnp.int32, (2 * tq, band), 1)
        wmask = jnp.abs(qpos - kpos) <= WINDOW
        yw = _win_attn(wq, [wkb, cwk_ref[...]], [wmask, None], [wvb, cwv_ref[...]], sink_ref, 4 * l, tq, group=1)
        pieces.append((_pool(pool_pad, u_pool, invcnt_ref[...], tq),
                       _diff_attn(dq, dks, vps, lam, gain, tq, group=1), yw[0], yw[1],
                       _conv_module(apad, gpad, dw_ref, row(dwb_ref), row(lng_ref), row(lnb_ref), tq)))
    else:
        x1_ref, h2_ref, kd_ref, vd_ref, kw_ref, vw_ref, *cast_out = outs
        h = _modulated_norm(x, row(gin_ref), mod, 0, 1)
        proj = _dot(h.astype(BF16), win_ref[...])
        for s in range(x.shape[0] // tq):
            ps = proj[s * tq:(s + 1) * tq]
            u_pool, dq = ps[:, 0:256], ps[:, 256:512]
            dk, dv = ps[:, 512:768], ps[:, 768:1024]
            wk, wv = ps[:, 1280:1408], ps[:, 1408:1536]
            ca, cg = ps[:, 1536:1792], ps[:, 1792:2048]
            wq = jnp.concatenate(_swap_middle_heads(ps[:, 1024:1152], ps[:, 1152:1280]), axis=1)
            for ref, new in ((kd_ref, dk.T), (vd_ref, dv.T), (kw_ref, wk.T), (vw_ref, wv.T)):
                if n_alias:
                    ref[s] = new
                else:
                    for ll in range(DEPTH):
                        ref[s, ll] = new if ll == l else jnp.zeros_like(new)
            pool_pad = jnp.concatenate([zpad, u_pool, zpad], axis=0)
            apad = jnp.concatenate([zpad, ca, zpad], axis=0)
            gpad = jnp.concatenate([zpad, cg, zpad], axis=0)
            dvb = dv.astype(BF16)
            vps = lambda h, dvb=dvb: [_ones_outside(dvb, h * 64, h * 64 + 64)]
            yw = _win_attn(wq, [wk.astype(BF16)], [None], [wv.astype(BF16)], sink_ref, 4 * l, tq, group=2)
            pieces.append((_pool(pool_pad, u_pool, invcnt_ref[...], tq),
                           _diff_attn(dq, [dk.astype(BF16)], vps, lam, gain, tq, group=4), yw[0], yw[1],
                           _conv_module(apad, gpad, dw_ref, row(dwb_ref), row(lng_ref), row(lnb_ref), tq)))

    for src, dst in zip(cast_in, cast_out):
        dst[...] = src[...].astype(BF16)

    cat = lambda k: jnp.concatenate([t[k] for t in pieces], axis=0).astype(BF16)
    y_pool = _dot(cat(0), wbd_ref[...]) * row(pscale_ref)
    y_conv = _dot(cat(4), pw_ref[...])
    o = (_dot(y_pool.astype(BF16), wout_ref[0:256, :])
         + _dot(cat(1), wout_ref[256:512, :])
         + _dot(cat(2), wout_ref[512:640, :])
         + _dot(cat(3), wout_ref[640:768, :])
         + _dot(y_conv.astype(BF16), wout_ref[768:1024, :]))
    x1 = x + mod[:, 2 * d:3 * d] * _rms(o, row(gpost_ref))
    x1_ref[...] = x1
    h2_ref[...] = _modulated_norm(x1, row(gpre_ref), mod, 3, 4).astype(BF16)


def _mixer_param_specs(l):
    return [
        _layer_spec((256, 256), l),
        _rows_spec(256),
        _layer_spec((4, DIFF_QK), l),
        _rows_spec(256),
        pl.BlockSpec(memory_space=pltpu.SMEM),
        _layer_spec((CONV_K, 256), l),
        _rows_spec(256),
        _rows_spec(256),
        _rows_spec(256),
        _layer_spec((256, 256), l),
        _whole_spec((D_MODEL, D_MODEL)),
        _rows_spec(D_MODEL),
        _rows_spec(D_MODEL),
    ]


def _mixer_param_args(P, l):
    return [P["pool_wbd"], P["pool_scale"], P["lam_p"], P["subln"], P["sink"],
            P["conv_dw"], P["conv_dw_b"], P["conv_ln_g"], P["conv_ln_b"], P["conv_pw"], P["w_out"][l],
            P["g_post_mix"], P["g_pre_ffn"]]


def _cast_specs(to_cast, cast_layer, steps, step_of):
    ins = [pl.BlockSpec((None, w.shape[1] // steps, w.shape[2]),
                        lambda *g: (cast_layer, step_of(*g), 0)) for w in to_cast]
    outs = [pl.BlockSpec((w.shape[1] // steps, w.shape[2]), lambda *g: (step_of(*g), 0)) for w in to_cast]
    shapes = [jax.ShapeDtypeStruct(w.shape[1:], BF16) for w in to_cast]
    return ins, outs, shapes


def _mixer_ctx(x, mod, P, l, to_cast=(), cast_layer=0, caches=()):
    rows = x.shape[0]
    tq = SEQ
    ns = CTX_SEQS_PER_STEP
    row_map = lambda i: (i, 0)
    seq_map = lambda i: (i, l, 0, 0)
    cast_in, cast_out, cast_shapes = _cast_specs(to_cast, cast_layer, rows // (ns * tq), lambda i: i)
    in_specs = [
        pl.BlockSpec((ns * tq, D_MODEL), row_map),
        _mod_spec(l),
        _rows_spec(D_MODEL),
        _whole_spec((D_MODEL, IN_WIDTH)),
        pl.BlockSpec((tq, 256), lambda i: (0, 0)),
    ] + _mixer_param_specs(l)
    args = [x, mod, P["g_pre_mix"], P["w_in"][l], P["invcnt_ctx"]] + _mixer_param_args(P, l) + list(to_cast)
    first_cache_out = 2
    aliases = {len(args) + k: first_cache_out + k for k in range(len(caches))}
    return pl.pallas_call(
        functools.partial(_mixer_kernel, dec=False, l=l, tq=tq, n_cast=len(to_cast), n_alias=len(caches)),
        out_shape=[jax.ShapeDtypeStruct((rows, D_MODEL), F32),
                   jax.ShapeDtypeStruct((rows, D_MODEL), BF16),
                   jax.ShapeDtypeStruct((BATCH, DEPTH, 256, tq), F32),
                   jax.ShapeDtypeStruct((BATCH, DEPTH, 256, tq), F32),
                   jax.ShapeDtypeStruct((BATCH, DEPTH, 128, tq), F32),
                   jax.ShapeDtypeStruct((BATCH, DEPTH, 128, tq), F32)] + cast_shapes,
        grid=(rows // (ns * tq),),
        in_specs=in_specs + cast_in + [pl.BlockSpec(memory_space=pl.ANY)] * len(caches),
        out_specs=[pl.BlockSpec((ns * tq, D_MODEL), row_map), pl.BlockSpec((ns * tq, D_MODEL), row_map)]
        + [pl.BlockSpec((ns, None, f, tq), seq_map) if caches
           else pl.BlockSpec((ns, DEPTH, f, tq), lambda i: (i, 0, 0, 0)) for f in (256, 256, 128, 128)]
        + cast_out,
        input_output_aliases=aliases,
        compiler_params=pltpu.CompilerParams(
            dimension_semantics=("arbitrary",), vmem_limit_bytes=VMEM_LIMIT),
        name="mixer_ctx",
    )(*args, *caches)


def _mixer_dec(ql, kvb, caches, x, mod, P, l, to_cast=(), cast_layer=0):
    rows = x.shape[0]
    tq = TQ_DEC
    nq = DEC_SEQ // tq
    cast_in_specs, cast_out_specs, cast_shapes = _cast_specs(
        to_cast, cast_layer, DEC_BATCH * nq, lambda b, q: b * nq + q)
    hb = tq // HALO
    nhb = rows // HALO
    row_map = lambda b, q: (b * nq + q, 0)

    def prev_map(col):
        return lambda b, q: (jnp.maximum((b * nq + q) * hb - 1, 0), col)

    def next_map(col):
        return lambda b, q: (jnp.minimum((b * nq + q + 1) * hb, nhb - 1), col)

    cache_map = lambda b, q: (b, l, 0, 0)
    in_specs = [
        pl.BlockSpec((tq, QL_WIDTH), row_map),
        pl.BlockSpec((HALO, 512), prev_map(0)),
        pl.BlockSpec((HALO, 512), next_map(0)),
        pl.BlockSpec((HALO, 256), prev_map(2)),
        pl.BlockSpec((HALO, 256), next_map(2)),
        pl.BlockSpec((DEC_SEQ, KV_WIDTH), lambda b, q: (b, 0)),
        pl.BlockSpec((None, None, PAST_LEN, 256), cache_map),
        pl.BlockSpec((None, None, PAST_LEN, 256), cache_map),
        pl.BlockSpec((None, None, PAST_LEN, 128), cache_map),
        pl.BlockSpec((None, None, PAST_LEN, 128), cache_map),
        pl.BlockSpec((tq, D_MODEL), row_map),
        _mod_spec(l),
        pl.BlockSpec((tq, 256), lambda b, q: (q, 0)),
    ] + _mixer_param_specs(l)
    args = [ql, ql, ql, ql, ql, kvb, *caches, x, mod, P["invcnt_dec"]] + _mixer_param_args(P, l)
    return pl.pallas_call(
        functools.partial(_mixer_kernel, dec=True, l=l, tq=tq, n_cast=len(to_cast)),
        out_shape=[jax.ShapeDtypeStruct((rows, D_MODEL), F32),
                   jax.ShapeDtypeStruct((rows, D_MODEL), BF16)]
        + cast_shapes,
        grid=(DEC_BATCH, nq),
        in_specs=in_specs + cast_in_specs,
        out_specs=[pl.BlockSpec((tq, D_MODEL), row_map), pl.BlockSpec((tq, D_MODEL), row_map)] + cast_out_specs,
        scratch_shapes=[pltpu.VMEM((4, PAST_LEN + DEC_SEQ, 256), BF16)],
        compiler_params=pltpu.CompilerParams(
            dimension_semantics=("arbitrary", "arbitrary"), vmem_limit_bytes=VMEM_LIMIT),
        name="mixer_dec",
    )(*args, *to_cast)


def _ffn_kernel(*refs, tm, seq_len, l, mod_row0, tiles_per_b, halo):
    if halo:
        h_ref, hp_ref, hn_ref, *rest = refs
        hx = jnp.concatenate([hp_ref[...], h_ref[...], hn_ref[...]], axis=0)
    else:
        h_ref, *rest = refs
        hx = h_ref[...]
    x_ref, mod_ref, wup_ref, dw_ref, dwb_ref, wd_ref, gpost_ref, o_ref, act_ref = rest
    i = pl.program_id(0)
    pos = (i * tm + lax.broadcasted_iota(jnp.int32, (tm, 1), 0)) % seq_len
    has_left = pos != 0
    has_right = pos != seq_len - 1
    for j in range(D_FF // FFN_CHUNK):
        ys = []
        for part in range(2):
            c0 = part * D_FF + j * FFN_CHUNK
            u = _dot(hx, wup_ref[:, c0:c0 + FFN_CHUNK])
            w = dw_ref[:, c0:c0 + FFN_CHUNK]
            if halo:
                below, mid, above = (u[FFN_HALO - 1:FFN_HALO - 1 + tm], u[FFN_HALO:FFN_HALO + tm],
                                     u[FFN_HALO + 1:FFN_HALO + 1 + tm])
            else:
                below, mid, above = pltpu.roll(u, 1, 0), u, pltpu.roll(u, tm - 1, 0)
            ys.append(jnp.where(has_left, below, 0.0) * w[0:1] + mid * w[1:2]
                      + jnp.where(has_right, above, 0.0) * w[2:3]
                      + dwb_ref[l:l + 1, c0:c0 + FFN_CHUNK])
        gate, up = ys
        act_ref[:, j * FFN_CHUNK:(j + 1) * FFN_CHUNK] = (gate * jax.nn.sigmoid(gate) * up).astype(BF16)
    o = _dot(act_ref[...], wd_ref[...])
    g2 = mod_ref[pl.ds(mod_row0 + i // tiles_per_b, 1), 5 * D_MODEL:6 * D_MODEL]
    o_ref[...] = x_ref[...] + g2 * _rms(o, gpost_ref[l:l + 1, :])


def _ffn(h2, x1, mod, P, l, seq_len, mod_row0, nb):
    rows = x1.shape[0]
    tm = TM_FFN
    tiles_per_b = rows // nb // tm
    hb = tm // FFN_HALO
    nhb = rows // FFN_HALO
    halo = tm % seq_len != 0
    h_specs = [pl.BlockSpec((tm, D_MODEL), lambda i: (i, 0))]
    if halo:
        h_specs += [pl.BlockSpec((FFN_HALO, D_MODEL), lambda i: (jnp.maximum(i * hb - 1, 0), 0)),
                    pl.BlockSpec((FFN_HALO, D_MODEL), lambda i: (jnp.minimum((i + 1) * hb, nhb - 1), 0))]
    return pl.pallas_call(
        functools.partial(_ffn_kernel, tm=tm, seq_len=seq_len, l=l, mod_row0=mod_row0,
                          tiles_per_b=tiles_per_b, halo=halo),
        out_shape=jax.ShapeDtypeStruct((rows, D_MODEL), F32),
        grid=(rows // tm,),
        in_specs=h_specs + [
            pl.BlockSpec((tm, D_MODEL), lambda i: (i, 0)),
            _mod_spec(l),
            _whole_spec((D_MODEL, 2 * D_FF)),
            _layer_spec((3, 2 * D_FF), l),
            _rows_spec(2 * D_FF),
            _whole_spec((D_FF, D_MODEL)),
            _rows_spec(D_MODEL),
        ],
        out_specs=pl.BlockSpec((tm, D_MODEL), lambda i: (i, 0)),
        scratch_shapes=[pltpu.VMEM((tm, D_FF), BF16)],
        compiler_params=pltpu.CompilerParams(
            dimension_semantics=("arbitrary",), vmem_limit_bytes=VMEM_LIMIT),
        name="ffn",
    )(*([h2] * len(h_specs)), x1, mod, P["ffn_up"][l], P["ffn_dw"], P["ffn_dw_b"], P["ffn_down"][l],
      P["g_post_ffn"])


def _pool_inv_counts(seq_len):
    inv = np.zeros((seq_len, 256), np.float32)
    pos = np.arange(seq_len)
    for g, w in enumerate(POOL_WINDOWS):
        cnt = np.clip(pos - w // 2 + w, 0, seq_len) - np.clip(pos - w // 2, 0, seq_len)
        inv[:, g * 64:(g + 1) * 64] = (1.0 / cnt.astype(np.float64))[:, None].astype(np.float32)
    return jnp.asarray(inv, F32)


def _rope_tables(seq_len, dim):
    rows = seq_len // GRID_W
    r = np.repeat(np.arange(rows), GRID_W).astype(np.float32)
    col = np.tile(np.arange(GRID_W), rows).astype(np.float32)
    n = dim // 4
    inv = (ROPE_BASE ** (-np.arange(n) / n)).astype(np.float32)
    ang = np.concatenate([r[:, None] * inv[None], col[:, None] * inv[None]], axis=-1)
    cos, sin = np.cos(ang), np.sin(ang)
    reps = LANES // dim
    cos_t = np.tile(np.concatenate([cos, cos], axis=-1), (1, reps))
    sin_t = np.tile(np.concatenate([-sin, sin], axis=-1), (1, reps))
    return jnp.asarray(cos_t, F32), jnp.asarray(sin_t, F32)


def _cache_from_feature_major(a, heads):
    return jnp.transpose(a.reshape(BATCH, DEPTH, heads, 64, SEQ), (0, 1, 4, 2, 3))


def kernel(x_prompt, x_sample, c, cache_diff_k, cache_diff_v, cache_win_k, cache_win_v, c_ctx, w_ada, b_ada, g_pre_mix, g_post_mix, g_pre_ffn, g_post_ffn, w_in, w_out, pool_w, pool_scale, diff_lq1, diff_lk1, diff_lq2, diff_lk2, diff_subln, win_sink, conv_dw, conv_dw_b, conv_ln_g, conv_ln_b, conv_pw, ffn_up, ffn_dw, ffn_dw_b, ffn_down):
    d = D_MODEL
    xp = x_prompt.reshape(BATCH * SEQ, d)
    xs = x_sample.reshape(DEC_BATCH * DEC_SEQ, d)

    cond8 = jnp.concatenate([c_ctx[None, :], c, jnp.zeros((8 - 1 - DEC_BATCH, d), F32)], axis=0)
    mod = _ada(cond8, w_ada, b_ada)

    eye4 = jnp.eye(4, dtype=F32)
    P = dict(
        invcnt_ctx=_pool_inv_counts(SEQ), invcnt_dec=_pool_inv_counts(DEC_SEQ),
        g_pre_mix=g_pre_mix,
        pool_wbd=jnp.einsum("lgcd,gh->lgchd", pool_w, eye4).reshape(DEPTH, 256, 256).astype(BF16),
        pool_scale=pool_scale,
        lam_p=jnp.stack([diff_lq1, diff_lk1, diff_lq2, diff_lk2], axis=1),
        subln=jnp.tile(diff_subln, (1, 4)),
        sink=win_sink.reshape(DEPTH * 4),
        conv_dw=conv_dw, conv_dw_b=conv_dw_b, conv_ln_g=conv_ln_g, conv_ln_b=conv_ln_b,
        conv_pw=conv_pw.astype(BF16),
        g_post_mix=g_post_mix, g_pre_ffn=g_pre_ffn,
        ffn_dw=ffn_dw, ffn_dw_b=ffn_dw_b, g_post_ffn=g_post_ffn,
    )
    mix_w = dict(w_in=w_in, w_out=w_out)
    ffn_w = dict(ffn_up=ffn_up, ffn_down=ffn_down)
    for name, w in mix_w.items():
        P[name] = [w[0].astype(BF16)]
    for name in ffn_w:
        P[name] = []
    rope_tabs = _rope_tables(DEC_SEQ, DIFF_QK) + _rope_tables(DEC_SEQ, WIN_HD)
    caches = (cache_diff_k.reshape(DEC_BATCH, DEPTH, PAST_LEN, 256).astype(BF16),
              cache_diff_v.reshape(DEC_BATCH, DEPTH, PAST_LEN, 256).astype(BF16),
              cache_win_k.reshape(DEC_BATCH, DEPTH, PAST_LEN, 128).astype(BF16),
              cache_win_v.reshape(DEC_BATCH, DEPTH, PAST_LEN, 128).astype(BF16))

    new_caches = ()
    for l in range(DEPTH):
        first = list(ffn_w.values()) if l == 0 else []
        xp1, hp2, *rest = _mixer_ctx(xp, mod, P, l, first, l, new_caches)
        new_caches, cast = tuple(rest[:4]), rest[4:]
        for name, w in zip(ffn_w, cast):
            P[name].append(w)
        xp = _ffn(hp2, xp1, mod, P, l, SEQ, 0, 1)

        ql, kvb = _inproj_dec(xs, mod, P["g_pre_mix"], P["w_in"][l], l, rope_tabs)
        big = {**mix_w, **ffn_w}
        nxt = list(big.values()) if l + 1 < DEPTH else []
        xs1, hs2, *cast = _mixer_dec(ql, kvb, caches, xs, mod, P, l, nxt, l + 1)
        for name, w in zip(big, cast):
            P[name].append(w)
        xs = _ffn(hs2, xs1, mod, P, l, DEC_SEQ, 1, DEC_BATCH)

    return (xp.reshape(BATCH, SEQ, d), xs.reshape(DEC_BATCH, DEC_SEQ, d),
            _cache_from_feature_major(new_caches[0], 4), _cache_from_feature_major(new_caches[1], 4),
            _cache_from_feature_major(new_caches[2], 2), _cache_from_feature_major(new_caches[3], 2))
```

```python
import functools
import math

import jax
import jax.numpy as jnp
import numpy as np
from jax import lax
from jax.experimental import pallas as pl
from jax.experimental.pallas import tpu as pltpu

F32 = jnp.float32
BF16 = jnp.bfloat16

D_MODEL = 1024
BATCH = 16
SEQ = 256
DEPTH = 2
DEC_BATCH = 2
DEC_SEQ = 2048
PAST_LEN = 256
GRID_W = 64
POOL_WINDOWS = (2, 4, 8, 16)
DIFF_QK = 32
WIN_HD = 64
WINDOW = 128
CONV_K = 31
IN_WIDTH = 2048
D_FF = 2816
ROPE_BASE = 10000.0
EPS = 1e-6
LOG2E = 1.4426950408889634

LANES = 128
SUBLANES = 8
HALO = 16
FFN_HALO = 16
FFN_CHUNK = 256
KV_WIDTH = 768
QL_WIDTH = 1280
VMEM_LIMIT = 56 * 1024 * 1024

TM_IN = 512
CTX_SEQS_PER_STEP = 2
TQ_DEC = 256
TM_FFN = 1024


def _rms(x, g):
    return x * lax.rsqrt(jnp.mean(x * x, axis=-1, keepdims=True) + EPS) * g


def _dot(a, b):
    return jnp.dot(a, b, preferred_element_type=F32)


def _dot_nt(a, b):
    return lax.dot_general(a, b, (((1,), (1,)), ((), ())), preferred_element_type=F32)


def _lane_range(shape, lo, hi):
    lane = lax.broadcasted_iota(jnp.int32, shape, 1)
    return (lane >= lo) & (lane < hi)


def _swap_middle_heads(t0, t1):
    lo = _lane_range(t0.shape, 0, 64)
    return (jnp.where(lo, t0, pltpu.roll(t1, 64, 1)),
            jnp.where(lo, pltpu.roll(t0, 64, 1), t1))


def _modulated_norm(x, g, mod, shift_col, scale_col):
    d = D_MODEL
    return (_rms(x, g) * (1.0 + mod[:, scale_col * d:(scale_col + 1) * d])
            + mod[:, shift_col * d:(shift_col + 1) * d])


def _layer_spec(shape, l):
    nz = len(shape)
    return pl.BlockSpec((None,) + tuple(shape), lambda *_: (l,) + (0,) * nz,
                        pipeline_mode=pl.Buffered(1))


def _whole_spec(shape):
    return pl.BlockSpec(tuple(shape), lambda *_: (0,) * len(shape), pipeline_mode=pl.Buffered(1))


def _rows_spec(width):
    return pl.BlockSpec((DEPTH, width), lambda *_: (0, 0), pipeline_mode=pl.Buffered(1))


def _mod_spec(l):
    return pl.BlockSpec((None, 8, 6 * D_MODEL), lambda *_: (l, 0, 0), pipeline_mode=pl.Buffered(1))


def _ada_kernel(c_ref, w_ref, b_ref, o_ref):
    c = c_ref[...]
    s = (c * jax.nn.sigmoid(c)).astype(BF16)
    o_ref[...] = _dot(s, w_ref[...].astype(BF16)) + b_ref[pl.ds(pl.program_id(0), 1), :]


def _ada(cond8, w_ada, b_ada):
    tn = 1024
    return pl.pallas_call(
        _ada_kernel,
        out_shape=jax.ShapeDtypeStruct((DEPTH, 8, 6 * D_MODEL), F32),
        grid=(DEPTH, 6 * D_MODEL // tn),
        in_specs=[
            pl.BlockSpec((8, D_MODEL), lambda l, j: (0, 0)),
            pl.BlockSpec((None, D_MODEL, tn), lambda l, j: (l, 0, j)),
            pl.BlockSpec((DEPTH, tn), lambda l, j: (0, j)),
        ],
        out_specs=pl.BlockSpec((None, 8, tn), lambda l, j: (l, 0, j)),
        compiler_params=pltpu.CompilerParams(
            dimension_semantics=("arbitrary", "arbitrary"), vmem_limit_bytes=VMEM_LIMIT),
        name="ada_mod",
    )(cond8, w_ada, b_ada)


def _rope_tile(v, cos, sin_signed, half):
    is_a = (lax.broadcasted_iota(jnp.int32, v.shape, 1) % (2 * half)) < half
    partner = jnp.where(is_a, pltpu.roll(v, LANES - half, 1), pltpu.roll(v, half, 1))
    return v * cos + partner * sin_signed


def _inproj_dec_kernel(x_ref, mod_ref, g_ref, w_ref, cd_ref, sd_ref, cw_ref, sw_ref, ql_ref, kv_ref,
                       *, l, tiles_per_b):
    mod = mod_ref[pl.ds(1 + pl.program_id(0) // tiles_per_b, 1), :]
    h = _modulated_norm(x_ref[...], g_ref[l:l + 1, :], mod, 0, 1)
    proj = _dot(h.astype(BF16), w_ref[...])
    tiles = [proj[:, i * LANES:(i + 1) * LANES] for i in range(IN_WIDTH // LANES)]
    cd, sd, cw, sw = cd_ref[...], sd_ref[...], cw_ref[...], sw_ref[...]
    for i in (2, 3, 4, 5):
        tiles[i] = _rope_tile(tiles[i], cd, sd, DIFF_QK // 2)
    for i in (8, 9, 10):
        tiles[i] = _rope_tile(tiles[i], cw, sw, WIN_HD // 2)
    tiles[8], tiles[9] = _swap_middle_heads(tiles[8], tiles[9])
    for j, i in enumerate((12, 13, 14, 15, 0, 1, 2, 3, 8, 9)):
        ql_ref[:, j * LANES:(j + 1) * LANES] = tiles[i]
    for j, i in enumerate((4, 5, 6, 7, 10, 11)):
        kv_ref[:, j * LANES:(j + 1) * LANES] = tiles[i].astype(BF16)


def _inproj_dec(x, mod, g, w_in_b, l, rope_tabs):
    rows = x.shape[0]
    tm = TM_IN
    tiles_per_b = DEC_SEQ // tm
    in_specs = [
        pl.BlockSpec((tm, D_MODEL), lambda i: (i, 0)),
        _mod_spec(l),
        _rows_spec(D_MODEL),
        _whole_spec((D_MODEL, IN_WIDTH)),
    ] + [pl.BlockSpec((tm, LANES), lambda i: (i % tiles_per_b, 0)) for _ in rope_tabs]
    return pl.pallas_call(
        functools.partial(_inproj_dec_kernel, l=l, tiles_per_b=tiles_per_b),
        out_shape=[jax.ShapeDtypeStruct((rows, QL_WIDTH), F32),
                   jax.ShapeDtypeStruct((rows, KV_WIDTH), BF16)],
        grid=(rows // tm,),
        in_specs=in_specs,
        out_specs=[pl.BlockSpec((tm, QL_WIDTH), lambda i: (i, 0)),
                   pl.BlockSpec((tm, KV_WIDTH), lambda i: (i, 0))],
        compiler_params=pltpu.CompilerParams(
            dimension_semantics=("arbitrary",), vmem_limit_bytes=VMEM_LIMIT),
        name="inproj_dec",
    )(x, mod, g, w_in_b, *rope_tabs)


def _pool(xpad, u, invcnt, tq):
    rows = xpad.shape[0]
    p = xpad
    sums = []
    for step, w in zip((1, 2, 4, 8), POOL_WINDOWS):
        p = p + pltpu.roll(p, step, 0)
        off = HALO + w // 2 - 1
        r = off % SUBLANES
        sh = p if r == 0 else pltpu.roll(p, rows - r, 0)
        sums.append(sh[off - r:off - r + tq])
    shape = u.shape
    s = jnp.where(_lane_range(shape, 0, 64), sums[0],
                  jnp.where(_lane_range(shape, 64, 128), sums[1],
                            jnp.where(_lane_range(shape, 128, 192), sums[2], sums[3])))
    return s * invcnt - u


def _conv_module(apad, gpad, dw_ref, dwb, lng, lnb, tq):
    u = apad * jax.nn.sigmoid(gpad)
    rows = u.shape[0]
    acc = jnp.broadcast_to(dwb, (tq, u.shape[1]))
    for r in range(SUBLANES):
        ur = u if r == 0 else pltpu.roll(u, rows - r, 0)
        for a8 in range(4):
            k = SUBLANES * a8 + r - 1
            if 0 <= k < CONV_K:
                acc = acc + ur[SUBLANES * a8:SUBLANES * a8 + tq] * dw_ref[k:k + 1, :]
    mu = jnp.mean(acc, axis=-1, keepdims=True)
    xc = acc - mu
    var = jnp.mean(xc * xc, axis=-1, keepdims=True)
    y = xc * lax.rsqrt(var + EPS) * lng + lnb
    return y * jax.nn.sigmoid(y)


def _ones_outside(v, lo, hi):
    m = jnp.where(_lane_range((1, v.shape[1]), lo, hi), 1.0, 0.0).astype(BF16)
    return v * m + (1.0 - m)


def _diff_attn(dq, ks, vps, lam, gain, tq, group):
    dqs = dq * (DIFF_QK ** -0.5 * LOG2E)
    ytiles = [jnp.zeros((tq, LANES), F32), jnp.zeros((tq, LANES), F32)]
    for h0 in range(0, 4, group):
        qs = []
        for h in range(h0, h0 + group):
            for lo in (h * 64, h * 64 + DIFF_QK):
                qs.append(jnp.where(_lane_range(dqs.shape, lo, lo + DIFF_QK), dqs, 0.0))
        qg = jnp.concatenate(qs, axis=0).astype(BF16)
        ss = [_dot_nt(qg, k) for k in ks]
        mx = functools.reduce(jnp.maximum, [jnp.max(s, axis=-1, keepdims=True) for s in ss])
        es = [jnp.exp2(s - mx).astype(BF16) for s in ss]
        for h in range(h0, h0 + group):
            r0 = (h - h0) * 2 * tq
            o = None
            for e, vp in zip(es, vps(h)):
                t = _dot(e[r0:r0 + 2 * tq], vp)
                o = t if o is None else o + t
            tile = h // 2
            ot = o[:, tile * LANES:(tile + 1) * LANES]
            hlo = (h % 2) * 64
            den = jnp.max(jnp.where(_lane_range(ot.shape, 64 - hlo, 128 - hlo), ot, 0.0), axis=-1, keepdims=True)
            r = ot / den
            od = jnp.where(_lane_range((tq, LANES), hlo, hlo + 64), r[:tq] - lam * r[tq:], 0.0)
            ms = jnp.sum(od * od, axis=-1, keepdims=True) * (1.0 / 64.0)
            ytiles[tile] = ytiles[tile] + od * lax.rsqrt(ms + EPS)
    return jnp.concatenate(ytiles, axis=1) * gain


def _win_attn(wq, ks, masks, vs, sink_ref, sink_base, tq, group):
    wqs = wq * (WIN_HD ** -0.5 * LOG2E)
    t0, t1 = wqs[:, :LANES], wqs[:, LANES:]
    y0 = jnp.zeros((tq, LANES), F32)
    y1 = jnp.zeros((tq, LANES), F32)
    rows = 2 * tq * group
    row = lax.broadcasted_iota(jnp.int32, (rows, 1), 0)
    for j0 in range(0, 2, group):
        qs = []
        sk = jnp.zeros((rows, 1), F32)
        for j in range(j0, j0 + group):
            lm = _lane_range(t0.shape, j * 64, j * 64 + 64)
            qs += [jnp.where(lm, t0, 0.0), jnp.where(lm, t1, 0.0)]
            for g in range(2):
                r0 = ((j - j0) * 2 + g) * tq
                sk = jnp.where((row >= r0) & (row < r0 + tq), sink_ref[sink_base + 2 * j + g], sk)
        sk = sk * LOG2E
        qg = jnp.concatenate(qs, axis=0).astype(BF16)
        ss = []
        for k, m in zip(ks, masks):
            s = _dot_nt(qg, k)
            if m is not None:
                s = jnp.where(m, s, -1e30)
            ss.append(s)
        mx = functools.reduce(jnp.maximum, [jnp.max(s, axis=-1, keepdims=True) for s in ss])
        mx = jnp.maximum(mx, sk)
        es = [jnp.exp2(s - mx).astype(BF16) for s in ss]
        esink = jnp.exp2(sk - mx)
        for j in range(j0, j0 + group):
            r0 = (j - j0) * 2 * tq
            o = None
            for e, v in zip(es, vs):
                t = _dot(e[r0:r0 + 2 * tq], _ones_outside(v, j * 64, j * 64 + 64))
                o = t if o is None else o + t
            other = _lane_range(o.shape, 64 - j * 64, 128 - j * 64)
            den = jnp.max(jnp.where(other, o, 0.0), axis=-1, keepdims=True)
            r = o / (den + esink[r0:r0 + 2 * tq])
            lm = _lane_range((tq, LANES), j * 64, j * 64 + 64)
            y0 = y0 + jnp.where(lm, r[:tq], 0.0)
            y1 = y1 + jnp.where(lm, r[tq:], 0.0)
    return _swap_middle_heads(y0, y1)


def _mixer_kernel(*refs, dec, l, tq, n_cast=0, n_alias=0):
    if dec:
        (ql_ref, conv_p_ref, conv_n_ref, pool_p_ref, pool_n_ref, kv_ref,
         cdk_ref, cdv_ref, cwk_ref, cwv_ref, x_ref, mod_ref, *rest) = refs
    else:
        (x_ref, mod_ref, gin_ref, win_ref, *rest) = refs
    (invcnt_ref, wbd_ref, pscale_ref, lamp_ref, subln_ref, sink_ref,
     dw_ref, dwb_ref, lng_ref, lnb_ref, pw_ref, wout_ref, gpost_ref, gpre_ref, *tail) = rest
    cast_in, outs = tail[:n_cast], tail[n_cast + n_alias:]
    d = D_MODEL
    mod = mod_ref[pl.ds(1 + pl.program_id(0), 1), :] if dec else mod_ref[0:1, :]
    x = x_ref[...]
    row = lambda ref: ref[l:l + 1, :]

    lam_init = 0.8 - 0.6 * math.exp(-0.3 * l)
    lp = lamp_ref[...]
    lam = (jnp.exp(jnp.sum(lp[0:1] * lp[1:2], axis=-1, keepdims=True))
           - jnp.exp(jnp.sum(lp[2:3] * lp[3:4], axis=-1, keepdims=True)) + lam_init)
    gain = row(subln_ref) * (1.0 - lam_init)

    zpad = jnp.zeros((HALO, 256), F32)
    pieces = []
    if dec:
        x1_ref, h2_ref, *cast_out, vp_ref = outs
        ca, cg = ql_ref[:, 0:256], ql_ref[:, 256:512]
        u_pool, dq, wq = ql_ref[:, 512:768], ql_ref[:, 768:1024], ql_ref[:, 1024:1280]
        q = pl.program_id(1)
        nq = pl.num_programs(1)
        pv = (q > 0).astype(F32)
        nv = (q < nq - 1).astype(F32)
        pool_pad = jnp.concatenate([pool_p_ref[...] * pv, u_pool, pool_n_ref[...] * nv], axis=0)
        cp = conv_p_ref[...] * pv
        cn = conv_n_ref[...] * nv
        apad = jnp.concatenate([cp[:, :256], ca, cn[:, :256]], axis=0)
        gpad = jnp.concatenate([cp[:, 256:], cg, cn[:, 256:]], axis=0)

        @pl.when(q == 0)
        def _():
            for h in range(4):
                vp_ref[h, 0:PAST_LEN, :] = _ones_outside(cdv_ref[...], h * 64, h * 64 + 64)
                vp_ref[h, PAST_LEN:, :] = _ones_outside(kv_ref[:, 256:512], h * 64, h * 64 + 64)

        dks = [cdk_ref[...], kv_ref[:, 0:256]]
        vps = lambda h: [vp_ref[h, 0:PAST_LEN, :], vp_ref[h, PAST_LEN:, :]]
        band = tq + 2 * WINDOW
        start = pl.multiple_of(jnp.clip(q * tq - WINDOW, 0, DEC_SEQ - band), WINDOW)
        wkb = kv_ref[pl.ds(start, band), 512:640]
        wvb = kv_ref[pl.ds(start, band), 640:768]
        qpos = q * tq + (lax.broadcasted_iota(jnp.int32, (2 * tq, band), 0) & (tq - 1))
        kpos = start + lax.broadcasted_iota(jnp.int32, (2 * tq, band), 1)
        wmask = jnp.abs(qpos - kpos) <= WINDOW
        yw = _win_attn(wq, [wkb, cwk_ref[...]], [wmask, None], [wvb, cwv_ref[...]], sink_ref, 4 * l, tq, group=1)
        pieces.append((_pool(pool_pad, u_pool, invcnt_ref[...], tq),
                       _diff_attn(dq, dks, vps, lam, gain, tq, group=1), yw[0], yw[1],
                       _conv_module(apad, gpad, dw_ref, row(dwb_ref), row(lng_ref), row(lnb_ref), tq)))
    else:
        x1_ref, h2_ref, kd_ref, vd_ref, kw_ref, vw_ref, *cast_out = outs
        h = _modulated_norm(x, row(gin_ref), mod, 0, 1)
        proj = _dot(h.astype(BF16), win_ref[...])
        for s in range(x.shape[0] // tq):
            ps = proj[s * tq:(s + 1) * tq]
            u_pool, dq = ps[:, 0:256], ps[:, 256:512]
            dk, dv = ps[:, 512:768], ps[:, 768:1024]
            wk, wv = ps[:, 1280:1408], ps[:, 1408:1536]
            ca, cg = ps[:, 1536:1792], ps[:, 1792:2048]
            wq = jnp.concatenate(_swap_middle_heads(ps[:, 1024:1152], ps[:, 1152:1280]), axis=1)
            for ref, new in ((kd_ref, dk.T), (vd_ref, dv.T), (kw_ref, wk.T), (vw_ref, wv.T)):
                if n_alias:
                    ref[s] = new
                else:
                    for ll in range(DEPTH):
                        ref[s, ll] = new if ll == l else jnp.zeros_like(new)
            pool_pad = jnp.concatenate([zpad, u_pool, zpad], axis=0)
            apad = jnp.concatenate([zpad, ca, zpad], axis=0)
            gpad = jnp.concatenate([zpad, cg, zpad], axis=0)
            dvb = dv.astype(BF16)
            vps = lambda h, dvb=dvb: [_ones_outside(dvb, h * 64, h * 64 + 64)]
            yw = _win_attn(wq, [wk.astype(BF16)], [None], [wv.astype(BF16)], sink_ref, 4 * l, tq, group=2)
            pieces.append((_pool(pool_pad, u_pool, invcnt_ref[...], tq),
                           _diff_attn(dq, [dk.astype(BF16)], vps, lam, gain, tq, group=4), yw[0], yw[1],
                           _conv_module(apad, gpad, dw_ref, row(dwb_ref), row(lng_ref), row(lnb_ref), tq)))

    for src, dst in zip(cast_in, cast_out):
        dst[...] = src[...].astype(BF16)

    cat = lambda k: jnp.concatenate([t[k] for t in pieces], axis=0).astype(BF16)
    y_pool = _dot(cat(0), wbd_ref[...]) * row(pscale_ref)
    y_conv = _dot(cat(4), pw_ref[...])
    o = (_dot(y_pool.astype(BF16), wout_ref[0:256, :])
         + _dot(cat(1), wout_ref[256:512, :])
         + _dot(cat(2), wout_ref[512:640, :])
         + _dot(cat(3), wout_ref[640:768, :])
         + _dot(y_conv.astype(BF16), wout_ref[768:1024, :]))
    x1 = x + mod[:, 2 * d:3 * d] * _rms(o, row(gpost_ref))
    x1_ref[...] = x1
    h2_ref[...] = _modulated_norm(x1, row(gpre_ref), mod, 3, 4).astype(BF16)


def _mixer_param_specs(l):
    return [
        _layer_spec((256, 256), l),
        _rows_spec(256),
        _layer_spec((4, DIFF_QK), l),
        _rows_spec(256),
        pl.BlockSpec(memory_space=pltpu.SMEM),
        _layer_spec((CONV_K, 256), l),
        _rows_spec(256),
        _rows_spec(256),
        _rows_spec(256),
        _layer_spec((256, 256), l),
        _whole_spec((D_MODEL, D_MODEL)),
        _rows_spec(D_MODEL),
        _rows_spec(D_MODEL),
    ]


def _mixer_param_args(P, l):
    return [P["pool_wbd"], P["pool_scale"], P["lam_p"], P["subln"], P["sink"],
            P["conv_dw"], P["conv_dw_b"], P["conv_ln_g"], P["conv_ln_b"], P["conv_pw"], P["w_out"][l],
            P["g_post_mix"], P["g_pre_ffn"]]


def _cast_specs(to_cast, cast_layer, steps, step_of):
    ins = [pl.BlockSpec((None, w.shape[1] // steps, w.shape[2]),
                        lambda *g: (cast_layer, step_of(*g), 0)) for w in to_cast]
    outs = [pl.BlockSpec((w.shape[1] // steps, w.shape[2]), lambda *g: (step_of(*g), 0)) for w in to_cast]
    shapes = [jax.ShapeDtypeStruct(w.shape[1:], BF16) for w in to_cast]
    return ins, outs, shapes


def _mixer_ctx(x, mod, P, l, to_cast=(), cast_layer=0, caches=()):
    rows = x.shape[0]
    tq = SEQ
    ns = CTX_SEQS_PER_STEP
    row_map = lambda i: (i, 0)
    seq_map = lambda i: (i, l, 0, 0)
    cast_in, cast_out, cast_shapes = _cast_specs(to_cast, cast_layer, rows // (ns * tq), lambda i: i)
    in_specs = [
        pl.BlockSpec((ns * tq, D_MODEL), row_map),
        _mod_spec(l),
        _rows_spec(D_MODEL),
        _whole_spec((D_MODEL, IN_WIDTH)),
        pl.BlockSpec((tq, 256), lambda i: (0, 0)),
    ] + _mixer_param_specs(l)
    args = [x, mod, P["g_pre_mix"], P["w_in"][l], P["invcnt_ctx"]] + _mixer_param_args(P, l) + list(to_cast)
    first_cache_out = 2
    aliases = {len(args) + k: first_cache_out + k for k in range(len(caches))}
    return pl.pallas_call(
        functools.partial(_mixer_kernel, dec=False, l=l, tq=tq, n_cast=len(to_cast), n_alias=len(caches)),
        out_shape=[jax.ShapeDtypeStruct((rows, D_MODEL), F32),
                   jax.ShapeDtypeStruct((rows, D_MODEL), BF16),
                   jax.ShapeDtypeStruct((BATCH, DEPTH, 256, tq), F32),
                   jax.ShapeDtypeStruct((BATCH, DEPTH, 256, tq), F32),
                   jax.ShapeDtypeStruct((BATCH, DEPTH, 128, tq), F32),
                   jax.ShapeDtypeStruct((BATCH, DEPTH, 128, tq), F32)] + cast_shapes,
        grid=(rows // (ns * tq),),
        in_specs=in_specs + cast_in + [pl.BlockSpec(memory_space=pl.ANY)] * len(caches),
        out_specs=[pl.BlockSpec((ns * tq, D_MODEL), row_map), pl.BlockSpec((ns * tq, D_MODEL), row_map)]
        + [pl.BlockSpec((ns, None, f, tq), seq_map) if caches
           else pl.BlockSpec((ns, DEPTH, f, tq), lambda i: (i, 0, 0, 0)) for f in (256, 256, 128, 128)]
        + cast_out,
        input_output_aliases=aliases,
        compiler_params=pltpu.CompilerParams(
            dimension_semantics=("arbitrary",), vmem_limit_bytes=VMEM_LIMIT),
        name="mixer_ctx",
    )(*args, *caches)


def _mixer_dec(ql, kvb, caches, x, mod, P, l, to_cast=(), cast_layer=0):
    rows = x.shape[0]
    tq = TQ_DEC
    nq = DEC_SEQ // tq
    cast_in_specs, cast_out_specs, cast_shapes = _cast_specs(
        to_cast, cast_layer, DEC_BATCH * nq, lambda b, q: b * nq + q)
    hb = tq // HALO
    nhb = rows // HALO
    row_map = lambda b, q: (b * nq + q, 0)

    def prev_map(col):
        return lambda b, q: (jnp.maximum((b * nq + q) * hb - 1, 0), col)

    def next_map(col):
        return lambda b, q: (jnp.minimum((b * nq + q + 1) * hb, nhb - 1), col)

    cache_map = lambda b, q: (b, l, 0, 0)
    in_specs = [
        pl.BlockSpec((tq, QL_WIDTH), row_map),
        pl.BlockSpec((HALO, 512), prev_map(0)),
        pl.BlockSpec((HALO, 512), next_map(0)),
        pl.BlockSpec((HALO, 256), prev_map(2)),
        pl.BlockSpec((HALO, 256), next_map(2)),
        pl.BlockSpec((DEC_SEQ, KV_WIDTH), lambda b, q: (b, 0)),
        pl.BlockSpec((None, None, PAST_LEN, 256), cache_map),
        pl.BlockSpec((None, None, PAST_LEN, 256), cache_map),
        pl.BlockSpec((None, None, PAST_LEN, 128), cache_map),
        pl.BlockSpec((None, None, PAST_LEN, 128), cache_map),
        pl.BlockSpec((tq, D_MODEL), row_map),
        _mod_spec(l),
        pl.BlockSpec((tq, 256), lambda b, q: (q, 0)),
    ] + _mixer_param_specs(l)
    args = [ql, ql, ql, ql, ql, kvb, *caches, x, mod, P["invcnt_dec"]] + _mixer_param_args(P, l)
    return pl.pallas_call(
        functools.partial(_mixer_kernel, dec=True, l=l, tq=tq, n_cast=len(to_cast)),
        out_shape=[jax.ShapeDtypeStruct((rows, D_MODEL), F32),
                   jax.ShapeDtypeStruct((rows, D_MODEL), BF16)]
        + cast_shapes,
        grid=(DEC_BATCH, nq),
        in_specs=in_specs + cast_in_specs,
        out_specs=[pl.BlockSpec((tq, D_MODEL), row_map), pl.BlockSpec((tq, D_MODEL), row_map)] + cast_out_specs,
        scratch_shapes=[pltpu.VMEM((4, PAST_LEN + DEC_SEQ, 256), BF16)],
        compiler_params=pltpu.CompilerParams(
            dimension_semantics=("arbitrary", "arbitrary"), vmem_limit_bytes=VMEM_LIMIT),
        name="mixer_dec",
    )(*args, *to_cast)


def _zero_rows(a, rows):
    sub = lax.broadcasted_iota(jnp.int32, (SUBLANES, 1), 0)
    pieces, done = [], 0
    for r in sorted(rows):
        g = r - r % SUBLANES
        pieces += [a[done:g], jnp.where(sub == r % SUBLANES, 0.0, a[g:g + SUBLANES])]
        done = g + SUBLANES
    pieces.append(a[done:])
    return jnp.concatenate([p for p in pieces if p.shape[0]], axis=0)


def _ffn_kernel(*refs, tm, seq_len, l, mod_row0, tiles_per_b, halo):
    i = pl.program_id(0)
    if halo:
        h_ref, hp_ref, hn_ref, *rest = refs
        t = i % tiles_per_b
        hp = jnp.where(t != 0, hp_ref[...], jnp.zeros_like(hp_ref))
        hn = jnp.where(t != tiles_per_b - 1, hn_ref[...], jnp.zeros_like(hn_ref))
        hx = jnp.concatenate([hp, h_ref[...], hn], axis=0)
    else:
        h_ref, *rest = refs
        hx = h_ref[...]
        starts = list(range(0, tm, seq_len))
        ends = [s + seq_len - 1 for s in starts]
    x_ref, mod_ref, wup_ref, dw_ref, dwb_ref, wd_ref, gpost_ref, o_ref, act_ref = rest
    for j in range(D_FF // FFN_CHUNK):
        ys = []
        for part in range(2):
            c0 = part * D_FF + j * FFN_CHUNK
            u = _dot(hx, wup_ref[:, c0:c0 + FFN_CHUNK])
            w = dw_ref[:, c0:c0 + FFN_CHUNK]
            if halo:
                below, mid, above = (u[FFN_HALO - 1:FFN_HALO - 1 + tm], u[FFN_HALO:FFN_HALO + tm],
                                     u[FFN_HALO + 1:FFN_HALO + 1 + tm])
            else:
                below, mid, above = (_zero_rows(pltpu.roll(u, 1, 0), starts), u,
                                     _zero_rows(pltpu.roll(u, tm - 1, 0), ends))
            ys.append(below * w[0:1] + mid * w[1:2] + above * w[2:3]
                      + dwb_ref[l:l + 1, c0:c0 + FFN_CHUNK])
        gate, up = ys
        act_ref[:, j * FFN_CHUNK:(j + 1) * FFN_CHUNK] = (gate * jax.nn.sigmoid(gate) * up).astype(BF16)
    o = _dot(act_ref[...], wd_ref[...])
    g2 = mod_ref[pl.ds(mod_row0 + i // tiles_per_b, 1), 5 * D_MODEL:6 * D_MODEL]
    o_ref[...] = x_ref[...] + g2 * _rms(o, gpost_ref[l:l + 1, :])


def _ffn(h2, x1, mod, P, l, seq_len, mod_row0, nb):
    rows = x1.shape[0]
    tm = TM_FFN
    tiles_per_b = rows // nb // tm
    hb = tm // FFN_HALO
    nhb = rows // FFN_HALO
    halo = tm % seq_len != 0
    h_specs = [pl.BlockSpec((tm, D_MODEL), lambda i: (i, 0))]
    if halo:
        h_specs += [pl.BlockSpec((FFN_HALO, D_MODEL), lambda i: (jnp.maximum(i * hb - 1, 0), 0)),
                    pl.BlockSpec((FFN_HALO, D_MODEL), lambda i: (jnp.minimum((i + 1) * hb, nhb - 1), 0))]
    return pl.pallas_call(
        functools.partial(_ffn_kernel, tm=tm, seq_len=seq_len, l=l, mod_row0=mod_row0,
                          tiles_per_b=tiles_per_b, halo=halo),
        out_shape=jax.ShapeDtypeStruct((rows, D_MODEL), F32),
        grid=(rows // tm,),
        in_specs=h_specs + [
            pl.BlockSpec((tm, D_MODEL), lambda i: (i, 0)),
            _mod_spec(l),
            _whole_spec((D_MODEL, 2 * D_FF)),
            _layer_spec((3, 2 * D_FF), l),
            _rows_spec(2 * D_FF),
            _whole_spec((D_FF, D_MODEL)),
            _rows_spec(D_MODEL),
        ],
        out_specs=pl.BlockSpec((tm, D_MODEL), lambda i: (i, 0)),
        scratch_shapes=[pltpu.VMEM((tm, D_FF), BF16)],
        compiler_params=pltpu.CompilerParams(
            dimension_semantics=("arbitrary",), vmem_limit_bytes=VMEM_LIMIT),
        name="ffn",
    )(*([h2] * len(h_specs)), x1, mod, P["ffn_up"][l], P["ffn_dw"], P["ffn_dw_b"], P["ffn_down"][l],
      P["g_post_ffn"])


def _pool_inv_counts(seq_len):
    inv = np.zeros((seq_len, 256), np.float32)
    pos = np.arange(seq_len)
    for g, w in enumerate(POOL_WINDOWS):
        cnt = np.clip(pos - w // 2 + w, 0, seq_len) - np.clip(pos - w // 2, 0, seq_len)
        inv[:, g * 64:(g + 1) * 64] = (1.0 / cnt.astype(np.float64))[:, None].astype(np.float32)
    return jnp.asarray(inv, F32)


def _rope_tables(seq_len, dim):
    rows = seq_len // GRID_W
    r = np.repeat(np.arange(rows), GRID_W).astype(np.float32)
    col = np.tile(np.arange(GRID_W), rows).astype(np.float32)
    n = dim // 4
    inv = (ROPE_BASE ** (-np.arange(n) / n)).astype(np.float32)
    ang = np.concatenate([r[:, None] * inv[None], col[:, None] * inv[None]], axis=-1)
    cos, sin = np.cos(ang), np.sin(ang)
    reps = LANES // dim
    cos_t = np.tile(np.concatenate([cos, cos], axis=-1), (1, reps))
    sin_t = np.tile(np.concatenate([-sin, sin], axis=-1), (1, reps))
    return jnp.asarray(cos_t, F32), jnp.asarray(sin_t, F32)


def _cache_from_feature_major(a, heads):
    return jnp.transpose(a.reshape(BATCH, DEPTH, heads, 64, SEQ), (0, 1, 4, 2, 3))


def kernel(x_prompt, x_sample, c, cache_diff_k, cache_diff_v, cache_win_k, cache_win_v, c_ctx, w_ada, b_ada, g_pre_mix, g_post_mix, g_pre_ffn, g_post_ffn, w_in, w_out, pool_w, pool_scale, diff_lq1, diff_lk1, diff_lq2, diff_lk2, diff_subln, win_sink, conv_dw, conv_dw_b, conv_ln_g, conv_ln_b, conv_pw, ffn_up, ffn_dw, ffn_dw_b, ffn_down):
    d = D_MODEL
    xp = x_prompt.reshape(BATCH * SEQ, d)
    xs = x_sample.reshape(DEC_BATCH * DEC_SEQ, d)

    cond8 = jnp.concatenate([c_ctx[None, :], c, jnp.zeros((8 - 1 - DEC_BATCH, d), F32)], axis=0)
    mod = _ada(cond8, w_ada, b_ada)

    eye4 = jnp.eye(4, dtype=F32)
    P = dict(
        invcnt_ctx=_pool_inv_counts(SEQ), invcnt_dec=_pool_inv_counts(DEC_SEQ),
        g_pre_mix=g_pre_mix,
        pool_wbd=jnp.einsum("lgcd,gh->lgchd", pool_w, eye4).reshape(DEPTH, 256, 256).astype(BF16),
        pool_scale=pool_scale,
        lam_p=jnp.stack([diff_lq1, diff_lk1, diff_lq2, diff_lk2], axis=1),
        subln=jnp.tile(diff_subln, (1, 4)),
        sink=win_sink.reshape(DEPTH * 4),
        conv_dw=conv_dw, conv_dw_b=conv_dw_b, conv_ln_g=conv_ln_g, conv_ln_b=conv_ln_b,
        conv_pw=conv_pw.astype(BF16),
        g_post_mix=g_post_mix, g_pre_ffn=g_pre_ffn,
        ffn_dw=ffn_dw, ffn_dw_b=ffn_dw_b, g_post_ffn=g_post_ffn,
    )
    mix_w = dict(w_in=w_in, w_out=w_out)
    ffn_w = dict(ffn_up=ffn_up, ffn_down=ffn_down)
    for name, w in mix_w.items():
        P[name] = [w[0].astype(BF16)]
    for name in ffn_w:
        P[name] = []
    rope_tabs = _rope_tables(DEC_SEQ, DIFF_QK) + _rope_tables(DEC_SEQ, WIN_HD)
    caches = (cache_diff_k.reshape(DEC_BATCH, DEPTH, PAST_LEN, 256).astype(BF16),
              cache_diff_v.reshape(DEC_BATCH, DEPTH, PAST_LEN, 256).astype(BF16),
              cache_win_k.reshape(DEC_BATCH, DEPTH, PAST_LEN, 128).astype(BF16),
              cache_win_v.reshape(DEC_BATCH, DEPTH, PAST_LEN, 128).astype(BF16))

    new_caches = ()
    for l in range(DEPTH):
        first = list(ffn_w.values()) if l == 0 else []
        xp1, hp2, *rest = _mixer_ctx(xp, mod, P, l, first, l, new_caches)
        new_caches, cast = tuple(rest[:4]), rest[4:]
        for name, w in zip(ffn_w, cast):
            P[name].append(w)
        xp = _ffn(hp2, xp1, mod, P, l, SEQ, 0, 1)

        ql, kvb = _inproj_dec(xs, mod, P["g_pre_mix"], P["w_in"][l], l, rope_tabs)
        big = {**mix_w, **ffn_w}
        nxt = list(big.values()) if l + 1 < DEPTH else []
        xs1, hs2, *cast = _mixer_dec(ql, kvb, caches, xs, mod, P, l, nxt, l + 1)
        for name, w in zip(big, cast):
            P[name].append(w)
        xs = _ffn(hs2, xs1, mod, P, l, DEC_SEQ, 1, DEC_BATCH)

    return (xp.reshape(BATCH, SEQ, d), xs.reshape(DEC_BATCH, DEC_SEQ, d),
            _cache_from_feature_major(new_caches[0], 4), _cache_from_feature_major(new_caches[1], 4),
            _cache_from_feature_major(new_caches[2], 2), _cache_from_feature_major(new_caches[3], 2))
```

```python
import functools
import math

import jax
import jax.numpy as jnp
import numpy as np
from jax import lax
from jax.experimental import pallas as pl
from jax.experimental.pallas import tpu as pltpu

F32 = jnp.float32
BF16 = jnp.bfloat16

D_MODEL = 1024
BATCH = 16
SEQ = 256
DEPTH = 2
DEC_BATCH = 2
DEC_SEQ = 2048
PAST_LEN = 256
GRID_W = 64
POOL_WINDOWS = (2, 4, 8, 16)
DIFF_QK = 32
WIN_HD = 64
WINDOW = 128
CONV_K = 31
IN_WIDTH = 2048
D_FF = 2816
ROPE_BASE = 10000.0
EPS = 1e-6
LOG2E = 1.4426950408889634

LANES = 128
SUBLANES = 8
HALO = 16
FFN_HALO = 16
FFN_CHUNK = 256
KV_WIDTH = 768
QL_WIDTH = 1280
VMEM_LIMIT = 56 * 1024 * 1024

TM_IN = 512
CTX_SEQS_PER_STEP = 2
TQ_DEC = 256
TM_FFN = 1024


def _rms(x, g):
    return x * lax.rsqrt(jnp.mean(x * x, axis=-1, keepdims=True) + EPS) * g


def _dot(a, b):
    return jnp.dot(a, b, preferred_element_type=F32)


def _dot_nt(a, b):
    return lax.dot_general(a, b, (((1,), (1,)), ((), ())), preferred_element_type=F32)


def _lane_range(shape, lo, hi):
    lane = lax.broadcasted_iota(jnp.int32, shape, 1)
    return (lane >= lo) & (lane < hi)


def _swap_middle_heads(t0, t1):
    lo = _lane_range(t0.shape, 0, 64)
    return (jnp.where(lo, t0, pltpu.roll(t1, 64, 1)),
            jnp.where(lo, pltpu.roll(t0, 64, 1), t1))


def _modulated_norm(x, g, mod, shift_col, scale_col):
    d = D_MODEL
    return (_rms(x, g) * (1.0 + mod[:, scale_col * d:(scale_col + 1) * d])
            + mod[:, shift_col * d:(shift_col + 1) * d])


def _layer_spec(shape, l):
    nz = len(shape)
    return pl.BlockSpec((None,) + tuple(shape), lambda *_: (l,) + (0,) * nz,
                        pipeline_mode=pl.Buffered(1))


def _whole_spec(shape):
    return pl.BlockSpec(tuple(shape), lambda *_: (0,) * len(shape), pipeline_mode=pl.Buffered(1))


def _rows_spec(width):
    return pl.BlockSpec((DEPTH, width), lambda *_: (0, 0), pipeline_mode=pl.Buffered(1))


def _mod_spec(l):
    return pl.BlockSpec((None, 8, 6 * D_MODEL), lambda *_: (l, 0, 0), pipeline_mode=pl.Buffered(1))


def _ada_kernel(c_ref, w_ref, b_ref, o_ref):
    c = c_ref[...]
    s = (c * jax.nn.sigmoid(c)).astype(BF16)
    o_ref[...] = _dot(s, w_ref[...].astype(BF16)) + b_ref[pl.ds(pl.program_id(0), 1), :]


def _ada(cond8, w_ada, b_ada):
    tn = 1024
    return pl.pallas_call(
        _ada_kernel,
        out_shape=jax.ShapeDtypeStruct((DEPTH, 8, 6 * D_MODEL), F32),
        grid=(DEPTH, 6 * D_MODEL // tn),
        in_specs=[
            pl.BlockSpec((8, D_MODEL), lambda l, j: (0, 0)),
            pl.BlockSpec((None, D_MODEL, tn), lambda l, j: (l, 0, j)),
            pl.BlockSpec((DEPTH, tn), lambda l, j: (0, j)),
        ],
        out_specs=pl.BlockSpec((None, 8, tn), lambda l, j: (l, 0, j)),
        compiler_params=pltpu.CompilerParams(
            dimension_semantics=("arbitrary", "arbitrary"), vmem_limit_bytes=VMEM_LIMIT),
        name="ada_mod",
    )(cond8, w_ada, b_ada)


def _rope_tile(v, cos, sin_signed, half):
    is_a = (lax.broadcasted_iota(jnp.int32, v.shape, 1) % (2 * half)) < half
    partner = jnp.where(is_a, pltpu.roll(v, LANES - half, 1), pltpu.roll(v, half, 1))
    return v * cos + partner * sin_signed


def _inproj_dec_kernel(x_ref, mod_ref, g_ref, w_ref, cd_ref, sd_ref, cw_ref, sw_ref, ql_ref, kv_ref,
                       *, l, tiles_per_b):
    mod = mod_ref[pl.ds(1 + pl.program_id(0) // tiles_per_b, 1), :]
    h = _modulated_norm(x_ref[...], g_ref[l:l + 1, :], mod, 0, 1)
    proj = _dot(h.astype(BF16), w_ref[...])
    tiles = [proj[:, i * LANES:(i + 1) * LANES] for i in range(IN_WIDTH // LANES)]
    cd, sd, cw, sw = cd_ref[...], sd_ref[...], cw_ref[...], sw_ref[...]
    for i in (2, 3, 4, 5):
        tiles[i] = _rope_tile(tiles[i], cd, sd, DIFF_QK // 2)
    for i in (8, 9, 10):
        tiles[i] = _rope_tile(tiles[i], cw, sw, WIN_HD // 2)
    tiles[8], tiles[9] = _swap_middle_heads(tiles[8], tiles[9])
    for j, i in enumerate((12, 13, 14, 15, 0, 1, 2, 3, 8, 9)):
        ql_ref[:, j * LANES:(j + 1) * LANES] = tiles[i]
    for j, i in enumerate((4, 5, 6, 7, 10, 11)):
        kv_ref[:, j * LANES:(j + 1) * LANES] = tiles[i].astype(BF16)


def _inproj_dec(x, mod, g, w_in_b, l, rope_tabs):
    rows = x.shape[0]
    tm = TM_IN
    tiles_per_b = DEC_SEQ // tm
    in_specs = [
        pl.BlockSpec((tm, D_MODEL), lambda i: (i, 0)),
        _mod_spec(l),
        _rows_spec(D_MODEL),
        _whole_spec((D_MODEL, IN_WIDTH)),
    ] + [pl.BlockSpec((tm, LANES), lambda i: (i % tiles_per_b, 0)) for _ in rope_tabs]
    return pl.pallas_call(
        functools.partial(_inproj_dec_kernel, l=l, tiles_per_b=tiles_per_b),
        out_shape=[jax.ShapeDtypeStruct((rows, QL_WIDTH), F32),
                   jax.ShapeDtypeStruct((rows, KV_WIDTH), BF16)],
        grid=(rows // tm,),
        in_specs=in_specs,
        out_specs=[pl.BlockSpec((tm, QL_WIDTH), lambda i: (i, 0)),
                   pl.BlockSpec((tm, KV_WIDTH), lambda i: (i, 0))],
        compiler_params=pltpu.CompilerParams(
            dimension_semantics=("arbitrary",), vmem_limit_bytes=VMEM_LIMIT),
        name="inproj_dec",
    )(x, mod, g, w_in_b, *rope_tabs)


def _pool(xpad, u, invcnt, tq):
    rows = xpad.shape[0]
    p = xpad
    sums = []
    for step, w in zip((1, 2, 4, 8), POOL_WINDOWS):
        p = p + pltpu.roll(p, step, 0)
        off = HALO + w // 2 - 1
        r = off % SUBLANES
        sh = p if r == 0 else pltpu.roll(p, rows - r, 0)
        sums.append(sh[off - r:off - r + tq])
    shape = u.shape
    s = jnp.where(_lane_range(shape, 0, 64), sums[0],
                  jnp.where(_lane_range(shape, 64, 128), sums[1],
                            jnp.where(_lane_range(shape, 128, 192), sums[2], sums[3])))
    return s * invcnt - u


def _conv_module(apad, gpad, dw_ref, dwb, lng, lnb, tq):
    u = apad * jax.nn.sigmoid(gpad)
    rows = u.shape[0]
    acc = jnp.broadcast_to(dwb, (tq, u.shape[1]))
    for r in range(SUBLANES):
        ur = u if r == 0 else pltpu.roll(u, rows - r, 0)
        for a8 in range(4):
            k = SUBLANES * a8 + r - 1
            if 0 <= k < CONV_K:
                acc = acc + ur[SUBLANES * a8:SUBLANES * a8 + tq] * dw_ref[k:k + 1, :]
    mu = jnp.mean(acc, axis=-1, keepdims=True)
    xc = acc - mu
    var = jnp.mean(xc * xc, axis=-1, keepdims=True)
    y = xc * lax.rsqrt(var + EPS) * lng + lnb
    return y * jax.nn.sigmoid(y)


def _ones_outside(v, lo, hi):
    m = jnp.where(_lane_range((1, v.shape[1]), lo, hi), 1.0, 0.0).astype(BF16)
    return v * m + (1.0 - m)


def _diff_attn(dq, ks, vps, lam, gain, tq, group):
    dqs = dq * (DIFF_QK ** -0.5 * LOG2E)
    ytiles = [jnp.zeros((tq, LANES), F32), jnp.zeros((tq, LANES), F32)]
    for h0 in range(0, 4, group):
        qs = []
        for h in range(h0, h0 + group):
            for lo in (h * 64, h * 64 + DIFF_QK):
                qs.append(jnp.where(_lane_range(dqs.shape, lo, lo + DIFF_QK), dqs, 0.0))
        qg = jnp.concatenate(qs, axis=0).astype(BF16)
        ss = [_dot_nt(qg, k) for k in ks]
        mx = functools.reduce(jnp.maximum, [jnp.max(s, axis=-1, keepdims=True) for s in ss])
        es = [jnp.exp2(s - mx).astype(BF16) for s in ss]
        for h in range(h0, h0 + group):
            r0 = (h - h0) * 2 * tq
            o = None
            for e, vp in zip(es, vps(h)):
                t = _dot(e[r0:r0 + 2 * tq], vp)
                o = t if o is None else o + t
            tile = h // 2
            ot = o[:, tile * LANES:(tile + 1) * LANES]
            hlo = (h % 2) * 64
            den = jnp.max(jnp.where(_lane_range(ot.shape, 64 - hlo, 128 - hlo), ot, 0.0), axis=-1, keepdims=True)
            r = ot / den
            od = jnp.where(_lane_range((tq, LANES), hlo, hlo + 64), r[:tq] - lam * r[tq:], 0.0)
            ms = jnp.sum(od * od, axis=-1, keepdims=True) * (1.0 / 64.0)
            ytiles[tile] = ytiles[tile] + od * lax.rsqrt(ms + EPS)
    return jnp.concatenate(ytiles, axis=1) * gain


def _win_attn(wq, ks, masks, vs, sink_ref, sink_base, tq, group):
    wqs = wq * (WIN_HD ** -0.5 * LOG2E)
    t0, t1 = wqs[:, :LANES], wqs[:, LANES:]
    y0 = jnp.zeros((tq, LANES), F32)
    y1 = jnp.zeros((tq, LANES), F32)
    rows = 2 * tq * group
    row = lax.broadcasted_iota(jnp.int32, (rows, 1), 0)
    for j0 in range(0, 2, group):
        qs = []
        sk = jnp.zeros((rows, 1), F32)
        for j in range(j0, j0 + group):
            lm = _lane_range(t0.shape, j * 64, j * 64 + 64)
            qs += [jnp.where(lm, t0, 0.0), jnp.where(lm, t1, 0.0)]
            for g in range(2):
                r0 = ((j - j0) * 2 + g) * tq
                sk = jnp.where((row >= r0) & (row < r0 + tq), sink_ref[sink_base + 2 * j + g], sk)
        sk = sk * LOG2E
        qg = jnp.concatenate(qs, axis=0).astype(BF16)
        ss = []
        for k, m in zip(ks, masks):
            s = _dot_nt(qg, k)
            if m is not None:
                s = jnp.where(m, s, -1e30)
            ss.append(s)
        mx = functools.reduce(jnp.maximum, [jnp.max(s, axis=-1, keepdims=True) for s in ss])
        mx = jnp.maximum(mx, sk)
        es = [jnp.exp2(s - mx).astype(BF16) for s in ss]
        esink = jnp.exp2(sk - mx)
        for j in range(j0, j0 + group):
            r0 = (j - j0) * 2 * tq
            o = None
            for e, v in zip(es, vs):
                t = _dot(e[r0:r0 + 2 * tq], _ones_outside(v, j * 64, j * 64 + 64))
                o = t if o is None else o + t
            other = _lane_range(o.shape, 64 - j * 64, 128 - j * 64)
            den = jnp.max(jnp.where(other, o, 0.0), axis=-1, keepdims=True)
            r = o / (den + esink[r0:r0 + 2 * tq])
            lm = _lane_range((tq, LANES), j * 64, j * 64 + 64)
            y0 = y0 + jnp.where(lm, r[:tq], 0.0)
            y1 = y1 + jnp.where(lm, r[tq:], 0.0)
    return _swap_middle_heads(y0, y1)


def _mixer_kernel(*refs, dec, l, tq, n_cast=0, n_alias=0):
    if dec:
        (ql_ref, conv_p_ref, conv_n_ref, pool_p_ref, pool_n_ref, kv_ref,
         cdk_ref, cdv_ref, cwk_ref, cwv_ref, x_ref, mod_ref, *rest) = refs
    else:
        (x_ref, mod_ref, gin_ref, win_ref, *rest) = refs
    (invcnt_ref, wbd_ref, pscale_ref, lamp_ref, subln_ref, sink_ref,
     dw_ref, dwb_ref, lng_ref, lnb_ref, pw_ref, wout_ref, gpost_ref, gpre_ref, *tail) = rest
    cast_in, outs = tail[:n_cast], tail[n_cast + n_alias:]
    d = D_MODEL
    mod = mod_ref[pl.ds(1 + pl.program_id(0), 1), :] if dec else mod_ref[0:1, :]
    x = x_ref[...]
    row = lambda ref: ref[l:l + 1, :]

    lam_init = 0.8 - 0.6 * math.exp(-0.3 * l)
    lp = lamp_ref[...]
    lam = (jnp.exp(jnp.sum(lp[0:1] * lp[1:2], axis=-1, keepdims=True))
           - jnp.exp(jnp.sum(lp[2:3] * lp[3:4], axis=-1, keepdims=True)) + lam_init)
    gain = row(subln_ref) * (1.0 - lam_init)

    zpad = jnp.zeros((HALO, 256), F32)
    pieces = []
    if dec:
        x1_ref, h2_ref, *cast_out, vp_ref = outs
        ca, cg = ql_ref[:, 0:256], ql_ref[:, 256:512]
        u_pool, dq, wq = ql_ref[:, 512:768], ql_ref[:, 768:1024], ql_ref[:, 1024:1280]
        q = pl.program_id(1)
        nq = pl.num_programs(1)
        pv = (q > 0).astype(F32)
        nv = (q < nq - 1).astype(F32)
        pool_pad = jnp.concatenate([pool_p_ref[...] * pv, u_pool, pool_n_ref[...] * nv], axis=0)
        cp = conv_p_ref[...] * pv
        cn = conv_n_ref[...] * nv
        apad = jnp.concatenate([cp[:, :256], ca, cn[:, :256]], axis=0)
        gpad = jnp.concatenate([cp[:, 256:], cg, cn[:, 256:]], axis=0)

        @pl.when(q == 0)
        def _():
            for h in range(4):
                vp_ref[h, 0:PAST_LEN, :] = _ones_outside(cdv_ref[...], h * 64, h * 64 + 64)
                vp_ref[h, PAST_LEN:, :] = _ones_outside(kv_ref[:, 256:512], h * 64, h * 64 + 64)

        dks = [cdk_ref[...], kv_ref[:, 0:256]]
        vps = lambda h: [vp_ref[h, 0:PAST_LEN, :], vp_ref[h, PAST_LEN:, :]]
        band = tq + 2 * WINDOW
        start = pl.multiple_of(jnp.clip(q * tq - WINDOW, 0, DEC_SEQ - band), WINDOW)
        wkb = kv_ref[pl.ds(start, band), 512:640]
        wvb = kv_ref[pl.ds(start, band), 640:768]
        qpos = q * tq + (lax.broadcasted_iota(jnp.int32, (2 * tq, band), 0) & (tq - 1))
        kpos = start + lax.broadcasted_iota(jnp.int32, (2 * tq, band), 1)
        wmask = jnp.abs(qpos - kpos) <= WINDOW
        yw = _win_attn(wq, [wkb, cwk_ref[...]], [wmask, None], [wvb, cwv_ref[...]], sink_ref, 4 * l, tq, group=1)
        pieces.append((_pool(pool_pad, u_pool, invcnt_ref[...], tq),
                       _diff_attn(dq, dks, vps, lam, gain, tq, group=4), yw[0], yw[1],
                       _conv_module(apad, gpad, dw_ref, row(dwb_ref), row(lng_ref), row(lnb_ref), tq)))
    else:
        x1_ref, h2_ref, kd_ref, vd_ref, kw_ref, vw_ref, *cast_out = outs
        h = _modulated_norm(x, row(gin_ref), mod, 0, 1)
        proj = _dot(h.astype(BF16), win_ref[...])
        for s in range(x.shape[0] // tq):
            ps = proj[s * tq:(s + 1) * tq]
            u_pool, dq = ps[:, 0:256], ps[:, 256:512]
            dk, dv = ps[:, 512:768], ps[:, 768:1024]
            wk, wv = ps[:, 1280:1408], ps[:, 1408:1536]
            ca, cg = ps[:, 1536:1792], ps[:, 1792:2048]
            wq = jnp.concatenate(_swap_middle_heads(ps[:, 1024:1152], ps[:, 1152:1280]), axis=1)
            for ref, new in ((kd_ref, dk.T), (vd_ref, dv.T), (kw_ref, wk.T), (vw_ref, wv.T)):
                if n_alias:
                    ref[s] = new
                else:
                    for ll in range(DEPTH):
                        ref[s, ll] = new if ll == l else jnp.zeros_like(new)
            pool_pad = jnp.concatenate([zpad, u_pool, zpad], axis=0)
            apad = jnp.concatenate([zpad, ca, zpad], axis=0)
            gpad = jnp.concatenate([zpad, cg, zpad], axis=0)
            dvb = dv.astype(BF16)
            vps = lambda h, dvb=dvb: [_ones_outside(dvb, h * 64, h * 64 + 64)]
            yw = _win_attn(wq, [wk.astype(BF16)], [None], [wv.astype(BF16)], sink_ref, 4 * l, tq, group=2)
            pieces.append((_pool(pool_pad, u_pool, invcnt_ref[...], tq),
                           _diff_attn(dq, [dk.astype(BF16)], vps, lam, gain, tq, group=4), yw[0], yw[1],
                           _conv_module(apad, gpad, dw_ref, row(dwb_ref), row(lng_ref), row(lnb_ref), tq)))

    for src, dst in zip(cast_in, cast_out):
        dst[...] = src[...].astype(BF16)

    cat = lambda k: jnp.concatenate([t[k] for t in pieces], axis=0).astype(BF16)
    y_pool = _dot(cat(0), wbd_ref[...]) * row(pscale_ref)
    y_conv = _dot(cat(4), pw_ref[...])
    o = (_dot(y_pool.astype(BF16), wout_ref[0:256, :])
         + _dot(cat(1), wout_ref[256:512, :])
         + _dot(cat(2), wout_ref[512:640, :])
         + _dot(cat(3), wout_ref[640:768, :])
         + _dot(y_conv.astype(BF16), wout_ref[768:1024, :]))
    x1 = x + mod[:, 2 * d:3 * d] * _rms(o, row(gpost_ref))
    x1_ref[...] = x1
    h2_ref[...] = _modulated_norm(x1, row(gpre_ref), mod, 3, 4).astype(BF16)


def _mixer_param_specs(l):
    return [
        _layer_spec((256, 256), l),
        _rows_spec(256),
        _layer_spec((4, DIFF_QK), l),
        _rows_spec(256),
        pl.BlockSpec(memory_space=pltpu.SMEM),
        _layer_spec((CONV_K, 256), l),
        _rows_spec(256),
        _rows_spec(256),
        _rows_spec(256),
        _layer_spec((256, 256), l),
        _whole_spec((D_MODEL, D_MODEL)),
        _rows_spec(D_MODEL),
        _rows_spec(D_MODEL),
    ]


def _mixer_param_args(P, l):
    return [P["pool_wbd"], P["pool_scale"], P["lam_p"], P["subln"], P["sink"],
            P["conv_dw"], P["conv_dw_b"], P["conv_ln_g"], P["conv_ln_b"], P["conv_pw"], P["w_out"][l],
            P["g_post_mix"], P["g_pre_ffn"]]


def _cast_specs(to_cast, cast_layer, steps, step_of):
    ins = [pl.BlockSpec((None, w.shape[1] // steps, w.shape[2]),
                        lambda *g: (cast_layer, step_of(*g), 0)) for w in to_cast]
    outs = [pl.BlockSpec((w.shape[1] // steps, w.shape[2]), lambda *g: (step_of(*g), 0)) for w in to_cast]
    shapes = [jax.ShapeDtypeStruct(w.shape[1:], BF16) for w in to_cast]
    return ins, outs, shapes


def _mixer_ctx(x, mod, P, l, to_cast=(), cast_layer=0, caches=()):
    rows = x.shape[0]
    tq = SEQ
    ns = CTX_SEQS_PER_STEP
    row_map = lambda i: (i, 0)
    seq_map = lambda i: (i, l, 0, 0)
    cast_in, cast_out, cast_shapes = _cast_specs(to_cast, cast_layer, rows // (ns * tq), lambda i: i)
    in_specs = [
        pl.BlockSpec((ns * tq, D_MODEL), row_map),
        _mod_spec(l),
        _rows_spec(D_MODEL),
        _whole_spec((D_MODEL, IN_WIDTH)),
        pl.BlockSpec((tq, 256), lambda i: (0, 0)),
    ] + _mixer_param_specs(l)
    args = [x, mod, P["g_pre_mix"], P["w_in"][l], P["invcnt_ctx"]] + _mixer_param_args(P, l) + list(to_cast)
    first_cache_out = 2
    aliases = {len(args) + k: first_cache_out + k for k in range(len(caches))}
    return pl.pallas_call(
        functools.partial(_mixer_kernel, dec=False, l=l, tq=tq, n_cast=len(to_cast), n_alias=len(caches)),
        out_shape=[jax.ShapeDtypeStruct((rows, D_MODEL), F32),
                   jax.ShapeDtypeStruct((rows, D_MODEL), BF16),
                   jax.ShapeDtypeStruct((BATCH, DEPTH, 256, tq), F32),
                   jax.ShapeDtypeStruct((BATCH, DEPTH, 256, tq), F32),
                   jax.ShapeDtypeStruct((BATCH, DEPTH, 128, tq), F32),
                   jax.ShapeDtypeStruct((BATCH, DEPTH, 128, tq), F32)] + cast_shapes,
        grid=(rows // (ns * tq),),
        in_specs=in_specs + cast_in + [pl.BlockSpec(memory_space=pl.ANY)] * len(caches),
        out_specs=[pl.BlockSpec((ns * tq, D_MODEL), row_map), pl.BlockSpec((ns * tq, D_MODEL), row_map)]
        + [pl.BlockSpec((ns, None, f, tq), seq_map) if caches
           else pl.BlockSpec((ns, DEPTH, f, tq), lambda i: (i, 0, 0, 0)) for f in (256, 256, 128, 128)]
        + cast_out,
        input_output_aliases=aliases,
        compiler_params=pltpu.CompilerParams(
            dimension_semantics=("arbitrary",), vmem_limit_bytes=VMEM_LIMIT),
        name="mixer_ctx",
    )(*args, *caches)


def _mixer_dec(ql, kvb, caches, x, mod, P, l, to_cast=(), cast_layer=0):
    rows = x.shape[0]
    tq = TQ_DEC
    nq = DEC_SEQ // tq
    cast_in_specs, cast_out_specs, cast_shapes = _cast_specs(
        to_cast, cast_layer, DEC_BATCH * nq, lambda b, q: b * nq + q)
    hb = tq // HALO
    nhb = rows // HALO
    row_map = lambda b, q: (b * nq + q, 0)

    def prev_map(col):
        return lambda b, q: (jnp.maximum((b * nq + q) * hb - 1, 0), col)

    def next_map(col):
        return lambda b, q: (jnp.minimum((b * nq + q + 1) * hb, nhb - 1), col)

    cache_map = lambda b, q: (b, l, 0, 0)
    in_specs = [
        pl.BlockSpec((tq, QL_WIDTH), row_map),
        pl.BlockSpec((HALO, 512), prev_map(0)),
        pl.BlockSpec((HALO, 512), next_map(0)),
        pl.BlockSpec((HALO, 256), prev_map(2)),
        pl.BlockSpec((HALO, 256), next_map(2)),
        pl.BlockSpec((DEC_SEQ, KV_WIDTH), lambda b, q: (b, 0)),
        pl.BlockSpec((None, None, PAST_LEN, 256), cache_map),
        pl.BlockSpec((None, None, PAST_LEN, 256), cache_map),
        pl.BlockSpec((None, None, PAST_LEN, 128), cache_map),
        pl.BlockSpec((None, None, PAST_LEN, 128), cache_map),
        pl.BlockSpec((tq, D_MODEL), row_map),
        _mod_spec(l),
        pl.BlockSpec((tq, 256), lambda b, q: (q, 0)),
    ] + _mixer_param_specs(l)
    args = [ql, ql, ql, ql, ql, kvb, *caches, x, mod, P["invcnt_dec"]] + _mixer_param_args(P, l)
    return pl.pallas_call(
        functools.partial(_mixer_kernel, dec=True, l=l, tq=tq, n_cast=len(to_cast)),
        out_shape=[jax.ShapeDtypeStruct((rows, D_MODEL), F32),
                   jax.ShapeDtypeStruct((rows, D_MODEL), BF16)]
        + cast_shapes,
        grid=(DEC_BATCH, nq),
        in_specs=in_specs + cast_in_specs,
        out_specs=[pl.BlockSpec((tq, D_MODEL), row_map), pl.BlockSpec((tq, D_MODEL), row_map)] + cast_out_specs,
        scratch_shapes=[pltpu.VMEM((4, PAST_LEN + DEC_SEQ, 256), BF16)],
        compiler_params=pltpu.CompilerParams(
            dimension_semantics=("arbitrary", "arbitrary"), vmem_limit_bytes=VMEM_LIMIT),
        name="mixer_dec",
    )(*args, *to_cast)


def _zero_rows(a, rows):
    sub = lax.broadcasted_iota(jnp.int32, (SUBLANES, 1), 0)
    pieces, done = [], 0
    for r in sorted(rows):
        g = r - r % SUBLANES
        pieces += [a[done:g], jnp.where(sub == r % SUBLANES, 0.0, a[g:g + SUBLANES])]
        done = g + SUBLANES
    pieces.append(a[done:])
    return jnp.concatenate([p for p in pieces if p.shape[0]], axis=0)


def _ffn_kernel(*refs, tm, seq_len, l, mod_row0, tiles_per_b, halo):
    i = pl.program_id(0)
    if halo:
        h_ref, hp_ref, hn_ref, *rest = refs
        t = i % tiles_per_b
        hp = jnp.where(t != 0, hp_ref[...], jnp.zeros_like(hp_ref))
        hn = jnp.where(t != tiles_per_b - 1, hn_ref[...], jnp.zeros_like(hn_ref))
        hx = jnp.concatenate([hp, h_ref[...], hn], axis=0)
    else:
        h_ref, *rest = refs
        hx = h_ref[...]
        starts = list(range(0, tm, seq_len))
        ends = [s + seq_len - 1 for s in starts]
    x_ref, mod_ref, wup_ref, dw_ref, dwb_ref, wd_ref, gpost_ref, o_ref, act_ref = rest
    for j in range(D_FF // FFN_CHUNK):
        ys = []
        for part in range(2):
            c0 = part * D_FF + j * FFN_CHUNK
            u = _dot(hx, wup_ref[:, c0:c0 + FFN_CHUNK])
            w = dw_ref[:, c0:c0 + FFN_CHUNK]
            if halo:
                below, mid, above = (u[FFN_HALO - 1:FFN_HALO - 1 + tm], u[FFN_HALO:FFN_HALO + tm],
                                     u[FFN_HALO + 1:FFN_HALO + 1 + tm])
            else:
                below, mid, above = (_zero_rows(pltpu.roll(u, 1, 0), starts), u,
                                     _zero_rows(pltpu.roll(u, tm - 1, 0), ends))
            ys.append(below * w[0:1] + mid * w[1:2] + above * w[2:3]
                      + dwb_ref[l:l + 1, c0:c0 + FFN_CHUNK])
        gate, up = ys
        act_ref[:, j * FFN_CHUNK:(j + 1) * FFN_CHUNK] = (gate * jax.nn.sigmoid(gate) * up).astype(BF16)
    o = _dot(act_ref[...], wd_ref[...])
    g2 = mod_ref[pl.ds(mod_row0 + i // tiles_per_b, 1), 5 * D_MODEL:6 * D_MODEL]
    o_ref[...] = x_ref[...] + g2 * _rms(o, gpost_ref[l:l + 1, :])


def _ffn(h2, x1, mod, P, l, seq_len, mod_row0, nb):
    rows = x1.shape[0]
    tm = TM_FFN
    tiles_per_b = rows // nb // tm
    hb = tm // FFN_HALO
    nhb = rows // FFN_HALO
    halo = tm % seq_len != 0
    h_specs = [pl.BlockSpec((tm, D_MODEL), lambda i: (i, 0))]
    if halo:
        h_specs += [pl.BlockSpec((FFN_HALO, D_MODEL), lambda i: (jnp.maximum(i * hb - 1, 0), 0)),
                    pl.BlockSpec((FFN_HALO, D_MODEL), lambda i: (jnp.minimum((i + 1) * hb, nhb - 1), 0))]
    return pl.pallas_call(
        functools.partial(_ffn_kernel, tm=tm, seq_len=seq_len, l=l, mod_row0=mod_row0,
                          tiles_per_b=tiles_per_b, halo=halo),
        out_shape=jax.ShapeDtypeStruct((rows, D_MODEL), F32),
        grid=(rows // tm,),
        in_specs=h_specs + [
            pl.BlockSpec((tm, D_MODEL), lambda i: (i, 0)),
            _mod_spec(l),
            _whole_spec((D_MODEL, 2 * D_FF)),
            _layer_spec((3, 2 * D_FF), l),
            _rows_spec(2 * D_FF),
            _whole_spec((D_FF, D_MODEL)),
            _rows_spec(D_MODEL),
        ],
        out_specs=pl.BlockSpec((tm, D_MODEL), lambda i: (i, 0)),
        scratch_shapes=[pltpu.VMEM((tm, D_FF), BF16)],
        compiler_params=pltpu.CompilerParams(
            dimension_semantics=("arbitrary",), vmem_limit_bytes=VMEM_LIMIT),
        name="ffn",
    )(*([h2] * len(h_specs)), x1, mod, P["ffn_up"][l], P["ffn_dw"], P["ffn_dw_b"], P["ffn_down"][l],
      P["g_post_ffn"])


def _pool_inv_counts(seq_len):
    inv = np.zeros((seq_len, 256), np.float32)
    pos = np.arange(seq_len)
    for g, w in enumerate(POOL_WINDOWS):
        cnt = np.clip(pos - w // 2 + w, 0, seq_len) - np.clip(pos - w // 2, 0, seq_len)
        inv[:, g * 64:(g + 1) * 64] = (1.0 / cnt.astype(np.float64))[:, None].astype(np.float32)
    return jnp.asarray(inv, F32)


def _rope_tables(seq_len, dim):
    rows = seq_len // GRID_W
    r = np.repeat(np.arange(rows), GRID_W).astype(np.float32)
    col = np.tile(np.arange(GRID_W), rows).astype(np.float32)
    n = dim // 4
    inv = (ROPE_BASE ** (-np.arange(n) / n)).astype(np.float32)
    ang = np.concatenate([r[:, None] * inv[None], col[:, None] * inv[None]], axis=-1)
    cos, sin = np.cos(ang), np.sin(ang)
    reps = LANES // dim
    cos_t = np.tile(np.concatenate([cos, cos], axis=-1), (1, reps))
    sin_t = np.tile(np.concatenate([-sin, sin], axis=-1), (1, reps))
    return jnp.asarray(cos_t, F32), jnp.asarray(sin_t, F32)


def _cache_from_feature_major(a, heads):
    return jnp.transpose(a.reshape(BATCH, DEPTH, heads, 64, SEQ), (0, 1, 4, 2, 3))


def kernel(x_prompt, x_sample, c, cache_diff_k, cache_diff_v, cache_win_k, cache_win_v, c_ctx, w_ada, b_ada, g_pre_mix, g_post_mix, g_pre_ffn, g_post_ffn, w_in, w_out, pool_w, pool_scale, diff_lq1, diff_lk1, diff_lq2, diff_lk2, diff_subln, win_sink, conv_dw, conv_dw_b, conv_ln_g, conv_ln_b, conv_pw, ffn_up, ffn_dw, ffn_dw_b, ffn_down):
    d = D_MODEL
    xp = x_prompt.reshape(BATCH * SEQ, d)
    xs = x_sample.reshape(DEC_BATCH * DEC_SEQ, d)

    cond8 = jnp.concatenate([c_ctx[None, :], c, jnp.zeros((8 - 1 - DEC_BATCH, d), F32)], axis=0)
    mod = _ada(cond8, w_ada, b_ada)

    eye4 = jnp.eye(4, dtype=F32)
    P = dict(
        invcnt_ctx=_pool_inv_counts(SEQ), invcnt_dec=_pool_inv_counts(DEC_SEQ),
        g_pre_mix=g_pre_mix,
        pool_wbd=jnp.einsum("lgcd,gh->lgchd", pool_w, eye4).reshape(DEPTH, 256, 256).astype(BF16),
        pool_scale=pool_scale,
        lam_p=jnp.stack([diff_lq1, diff_lk1, diff_lq2, diff_lk2], axis=1),
        subln=jnp.tile(diff_subln, (1, 4)),
        sink=win_sink.reshape(DEPTH * 4),
        conv_dw=conv_dw, conv_dw_b=conv_dw_b, conv_ln_g=conv_ln_g, conv_ln_b=conv_ln_b,
        conv_pw=conv_pw.astype(BF16),
        g_post_mix=g_post_mix, g_pre_ffn=g_pre_ffn,
        ffn_dw=ffn_dw, ffn_dw_b=ffn_dw_b, g_post_ffn=g_post_ffn,
    )
    mix_w = dict(w_in=w_in, w_out=w_out)
    ffn_w = dict(ffn_up=ffn_up, ffn_down=ffn_down)
    for name, w in mix_w.items():
        P[name] = [w[0].astype(BF16)]
    for name in ffn_w:
        P[name] = []
    rope_tabs = _rope_tables(DEC_SEQ, DIFF_QK) + _rope_tables(DEC_SEQ, WIN_HD)
    caches = (cache_diff_k.reshape(DEC_BATCH, DEPTH, PAST_LEN, 256).astype(BF16),
              cache_diff_v.reshape(DEC_BATCH, DEPTH, PAST_LEN, 256).astype(BF16),
              cache_win_k.reshape(DEC_BATCH, DEPTH, PAST_LEN, 128).astype(BF16),
              cache_win_v.reshape(DEC_BATCH, DEPTH, PAST_LEN, 128).astype(BF16))

    new_caches = ()
    for l in range(DEPTH):
        first = list(ffn_w.values()) if l == 0 else []
        xp1, hp2, *rest = _mixer_ctx(xp, mod, P, l, first, l, new_caches)
        new_caches, cast = tuple(rest[:4]), rest[4:]
        for name, w in zip(ffn_w, cast):
            P[name].append(w)
        xp = _ffn(hp2, xp1, mod, P, l, SEQ, 0, 1)

        ql, kvb = _inproj_dec(xs, mod, P["g_pre_mix"], P["w_in"][l], l, rope_tabs)
        big = {**mix_w, **ffn_w}
        nxt = list(big.values()) if l + 1 < DEPTH else []
        xs1, hs2, *cast = _mixer_dec(ql, kvb, caches, xs, mod, P, l, nxt, l + 1)
        for name, w in zip(big, cast):
            P[name].append(w)
        xs = _ffn(hs2, xs1, mod, P, l, DEC_SEQ, 1, DEC_BATCH)

    return (xp.reshape(BATCH, SEQ, d), xs.reshape(DEC_BATCH, DEC_SEQ, d),
            _cache_from_feature_major(new_caches[0], 4), _cache_from_feature_major(new_caches[1], 4),
            _cache_from_feature_major(new_caches[2], 2), _cache_from_feature_major(new_caches[3], 2))
```

```python
import functools
import math

import jax
import jax.numpy as jnp
import numpy as np
from jax import lax
from jax.experimental import pallas as pl
from jax.experimental.pallas import tpu as pltpu

F32 = jnp.float32
BF16 = jnp.bfloat16

D_MODEL = 1024
BATCH = 16
SEQ = 256
DEPTH = 2
DEC_BATCH = 2
DEC_SEQ = 2048
PAST_LEN = 256
GRID_W = 64
POOL_WINDOWS = (2, 4, 8, 16)
DIFF_QK = 32
WIN_HD = 64
WINDOW = 128
CONV_K = 31
IN_WIDTH = 2048
D_FF = 2816
ROPE_BASE = 10000.0
EPS = 1e-6
LOG2E = 1.4426950408889634

LANES = 128
SUBLANES = 8
HALO = 16
FFN_HALO = 16
FFN_CHUNK = 256
KV_WIDTH = 768
QL_WIDTH = 1280
VMEM_LIMIT = 56 * 1024 * 1024

TM_IN = 512
CTX_SEQS_PER_STEP = 2
TQ_DEC = 256
TM_FFN = 1024


def _rms(x, g):
    return x * lax.rsqrt(jnp.mean(x * x, axis=-1, keepdims=True) + EPS) * g


def _dot(a, b):
    return jnp.dot(a, b, preferred_element_type=F32)


def _dot_nt(a, b):
    return lax.dot_general(a, b, (((1,), (1,)), ((), ())), preferred_element_type=F32)


def _lane_range(shape, lo, hi):
    lane = lax.broadcasted_iota(jnp.int32, shape, 1)
    return (lane >= lo) & (lane < hi)


def _swap_middle_heads(t0, t1):
    lo = _lane_range(t0.shape, 0, 64)
    return (jnp.where(lo, t0, pltpu.roll(t1, 64, 1)),
            jnp.where(lo, pltpu.roll(t0, 64, 1), t1))


def _modulated_norm(x, g, mod, shift_col, scale_col):
    d = D_MODEL
    return (_rms(x, g) * (1.0 + mod[:, scale_col * d:(scale_col + 1) * d])
            + mod[:, shift_col * d:(shift_col + 1) * d])


def _layer_spec(shape, l):
    nz = len(shape)
    return pl.BlockSpec((None,) + tuple(shape), lambda *_: (l,) + (0,) * nz,
                        pipeline_mode=pl.Buffered(1))


def _whole_spec(shape):
    return pl.BlockSpec(tuple(shape), lambda *_: (0,) * len(shape), pipeline_mode=pl.Buffered(1))


def _rows_spec(width):
    return pl.BlockSpec((DEPTH, width), lambda *_: (0, 0), pipeline_mode=pl.Buffered(1))


def _mod_spec():
    return pl.BlockSpec((8, 6 * D_MODEL), lambda *_: (0, 0), pipeline_mode=pl.Buffered(1))


def _ada_slab(c_ref, w_ref, b_ref, layer):
    c = c_ref[...]
    s = (c * jax.nn.sigmoid(c)).astype(BF16)
    return _dot(s, w_ref[...].astype(BF16)) + b_ref[layer:layer + 1, :]


def _ada_kernel(c_ref, w_ref, b_ref, o_ref):
    o_ref[...] = _ada_slab(c_ref, w_ref, b_ref, 0)


def _ada_specs(layer, tn, step_of):
    ins = [pl.BlockSpec((8, D_MODEL), lambda *g: (0, 0)),
           pl.BlockSpec((None, D_MODEL, tn), lambda *g: (layer, 0, step_of(*g))),
           pl.BlockSpec((DEPTH, tn), lambda *g: (0, step_of(*g)))]
    return ins, pl.BlockSpec((8, tn), lambda *g: (0, step_of(*g)))


def _ada(cond8, w_ada, b_ada):
    tn = 1024
    in_specs, out_spec = _ada_specs(0, tn, lambda j: j)
    return pl.pallas_call(
        _ada_kernel,
        out_shape=jax.ShapeDtypeStruct((8, 6 * D_MODEL), F32),
        grid=(6 * D_MODEL // tn,),
        in_specs=in_specs,
        out_specs=out_spec,
        compiler_params=pltpu.CompilerParams(
            dimension_semantics=("arbitrary",), vmem_limit_bytes=VMEM_LIMIT),
        name="ada_mod",
    )(cond8, w_ada, b_ada)


def _rope_tile(v, cos, sin_signed, half):
    is_a = (lax.broadcasted_iota(jnp.int32, v.shape, 1) % (2 * half)) < half
    partner = jnp.where(is_a, pltpu.roll(v, LANES - half, 1), pltpu.roll(v, half, 1))
    return v * cos + partner * sin_signed


def _inproj_dec_kernel(x_ref, mod_ref, g_ref, w_ref, cd_ref, sd_ref, cw_ref, sw_ref, ql_ref, kv_ref,
                       *, l, tiles_per_b):
    mod = mod_ref[pl.ds(1 + pl.program_id(0) // tiles_per_b, 1), :]
    h = _modulated_norm(x_ref[...], g_ref[l:l + 1, :], mod, 0, 1)
    proj = _dot(h.astype(BF16), w_ref[...])
    tiles = [proj[:, i * LANES:(i + 1) * LANES] for i in range(IN_WIDTH // LANES)]
    cd, sd, cw, sw = cd_ref[...], sd_ref[...], cw_ref[...], sw_ref[...]
    for i in (2, 3, 4, 5):
        tiles[i] = _rope_tile(tiles[i], cd, sd, DIFF_QK // 2)
    for i in (8, 9, 10):
        tiles[i] = _rope_tile(tiles[i], cw, sw, WIN_HD // 2)
    tiles[8], tiles[9] = _swap_middle_heads(tiles[8], tiles[9])
    for j, i in enumerate((12, 13, 14, 15, 0, 1, 2, 3, 8, 9)):
        ql_ref[:, j * LANES:(j + 1) * LANES] = tiles[i]
    for j, i in enumerate((4, 5, 6, 7, 10, 11)):
        kv_ref[:, j * LANES:(j + 1) * LANES] = tiles[i].astype(BF16)


def _inproj_dec(x, mod, g, w_in_b, l, rope_tabs):
    rows = x.shape[0]
    tm = TM_IN
    tiles_per_b = DEC_SEQ // tm
    in_specs = [
        pl.BlockSpec((tm, D_MODEL), lambda i: (i, 0)),
        _mod_spec(),
        _rows_spec(D_MODEL),
        _whole_spec((D_MODEL, IN_WIDTH)),
    ] + [pl.BlockSpec((tm, LANES), lambda i: (i % tiles_per_b, 0)) for _ in rope_tabs]
    return pl.pallas_call(
        functools.partial(_inproj_dec_kernel, l=l, tiles_per_b=tiles_per_b),
        out_shape=[jax.ShapeDtypeStruct((rows, QL_WIDTH), F32),
                   jax.ShapeDtypeStruct((rows, KV_WIDTH), BF16)],
        grid=(rows // tm,),
        in_specs=in_specs,
        out_specs=[pl.BlockSpec((tm, QL_WIDTH), lambda i: (i, 0)),
                   pl.BlockSpec((tm, KV_WIDTH), lambda i: (i, 0))],
        compiler_params=pltpu.CompilerParams(
            dimension_semantics=("arbitrary",), vmem_limit_bytes=VMEM_LIMIT),
        name="inproj_dec",
    )(x, mod, g, w_in_b, *rope_tabs)


def _pool(xpad, u, invcnt, tq):
    rows = xpad.shape[0]
    p = xpad
    sums = []
    for step, w in zip((1, 2, 4, 8), POOL_WINDOWS):
        p = p + pltpu.roll(p, step, 0)
        off = HALO + w // 2 - 1
        r = off % SUBLANES
        sh = p if r == 0 else pltpu.roll(p, rows - r, 0)
        sums.append(sh[off - r:off - r + tq])
    shape = u.shape
    s = jnp.where(_lane_range(shape, 0, 64), sums[0],
                  jnp.where(_lane_range(shape, 64, 128), sums[1],
                            jnp.where(_lane_range(shape, 128, 192), sums[2], sums[3])))
    return s * invcnt - u


def _conv_module(apad, gpad, dw_ref, dwb, lng, lnb, tq):
    u = apad * jax.nn.sigmoid(gpad)
    rows = u.shape[0]
    acc = jnp.broadcast_to(dwb, (tq, u.shape[1]))
    for r in range(SUBLANES):
        ur = u if r == 0 else pltpu.roll(u, rows - r, 0)
        for a8 in range(4):
            k = SUBLANES * a8 + r - 1
            if 0 <= k < CONV_K:
                acc = acc + ur[SUBLANES * a8:SUBLANES * a8 + tq] * dw_ref[k:k + 1, :]
    mu = jnp.mean(acc, axis=-1, keepdims=True)
    xc = acc - mu
    var = jnp.mean(xc * xc, axis=-1, keepdims=True)
    y = xc * lax.rsqrt(var + EPS) * lng + lnb
    return y * jax.nn.sigmoid(y)


def _ones_outside(v, lo, hi):
    m = jnp.where(_lane_range((1, v.shape[1]), lo, hi), 1.0, 0.0).astype(BF16)
    return v * m + (1.0 - m)


def _diff_attn(dq, ks, vps, lam, gain, tq, group):
    dqs = dq * (DIFF_QK ** -0.5 * LOG2E)
    ytiles = [jnp.zeros((tq, LANES), F32), jnp.zeros((tq, LANES), F32)]
    for h0 in range(0, 4, group):
        qs = []
        for h in range(h0, h0 + group):
            for lo in (h * 64, h * 64 + DIFF_QK):
                qs.append(jnp.where(_lane_range(dqs.shape, lo, lo + DIFF_QK), dqs, 0.0))
        qg = jnp.concatenate(qs, axis=0).astype(BF16)
        ss = [_dot_nt(qg, k) for k in ks]
        mx = functools.reduce(jnp.maximum, [jnp.max(s, axis=-1, keepdims=True) for s in ss])
        es = [jnp.exp2(s - mx).astype(BF16) for s in ss]
        for h in range(h0, h0 + group):
            r0 = (h - h0) * 2 * tq
            o = None
            for e, vp in zip(es, vps(h)):
                t = _dot(e[r0:r0 + 2 * tq], vp)
                o = t if o is None else o + t
            tile = h // 2
            ot = o[:, tile * LANES:(tile + 1) * LANES]
            hlo = (h % 2) * 64
            den = jnp.max(jnp.where(_lane_range(ot.shape, 64 - hlo, 128 - hlo), ot, 0.0), axis=-1, keepdims=True)
            r = ot / den
            od = jnp.where(_lane_range((tq, LANES), hlo, hlo + 64), r[:tq] - lam * r[tq:], 0.0)
            ms = jnp.sum(od * od, axis=-1, keepdims=True) * (1.0 / 64.0)
            ytiles[tile] = ytiles[tile] + od * lax.rsqrt(ms + EPS)
    return jnp.concatenate(ytiles, axis=1) * gain


def _win_attn(wq, ks, masks, vs, sink_ref, sink_base, tq, group):
    wqs = wq * (WIN_HD ** -0.5 * LOG2E)
    t0, t1 = wqs[:, :LANES], wqs[:, LANES:]
    y0 = jnp.zeros((tq, LANES), F32)
    y1 = jnp.zeros((tq, LANES), F32)
    rows = 2 * tq * group
    row = lax.broadcasted_iota(jnp.int32, (rows, 1), 0)
    for j0 in range(0, 2, group):
        qs = []
        sk = jnp.zeros((rows, 1), F32)
        for j in range(j0, j0 + group):
            lm = _lane_range(t0.shape, j * 64, j * 64 + 64)
            qs += [jnp.where(lm, t0, 0.0), jnp.where(lm, t1, 0.0)]
            for g in range(2):
                r0 = ((j - j0) * 2 + g) * tq
                sk = jnp.where((row >= r0) & (row < r0 + tq), sink_ref[sink_base + 2 * j + g], sk)
        sk = sk * LOG2E
        qg = jnp.concatenate(qs, axis=0).astype(BF16)
        ss = []
        for k, m in zip(ks, masks):
            s = _dot_nt(qg, k)
            if m is not None:
                s = jnp.where(m, s, -1e30)
            ss.append(s)
        mx = functools.reduce(jnp.maximum, [jnp.max(s, axis=-1, keepdims=True) for s in ss])
        mx = jnp.maximum(mx, sk)
        es = [jnp.exp2(s - mx).astype(BF16) for s in ss]
        esink = jnp.exp2(sk - mx)
        for j in range(j0, j0 + group):
            r0 = (j - j0) * 2 * tq
            o = None
            for e, v in zip(es, vs):
                t = _dot(e[r0:r0 + 2 * tq], _ones_outside(v, j * 64, j * 64 + 64))
                o = t if o is None else o + t
            other = _lane_range(o.shape, 64 - j * 64, 128 - j * 64)
            den = jnp.max(jnp.where(other, o, 0.0), axis=-1, keepdims=True)
            r = o / (den + esink[r0:r0 + 2 * tq])
            lm = _lane_range((tq, LANES), j * 64, j * 64 + 64)
            y0 = y0 + jnp.where(lm, r[:tq], 0.0)
            y1 = y1 + jnp.where(lm, r[tq:], 0.0)
    return _swap_middle_heads(y0, y1)


def _mixer_kernel(*refs, dec, l, tq, n_cast=0, n_alias=0, next_mod=False):
    if dec:
        (ql_ref, conv_p_ref, conv_n_ref, pool_p_ref, pool_n_ref, kv_ref,
         cdk_ref, cdv_ref, cwk_ref, cwv_ref, x_ref, mod_ref, *rest) = refs
    else:
        (x_ref, mod_ref, gin_ref, win_ref, *rest) = refs
    (invcnt_ref, wbd_ref, pscale_ref, lamp_ref, subln_ref, sink_ref,
     dw_ref, dwb_ref, lng_ref, lnb_ref, pw_ref, wout_ref, gpost_ref, gpre_ref, *tail) = rest
    n_ada = 3 if next_mod else 0
    cast_in, ada_in = tail[:n_cast], tail[n_cast:n_cast + n_ada]
    outs = tail[n_cast + n_ada + n_alias:]
    d = D_MODEL
    mod = mod_ref[pl.ds(1 + pl.program_id(0), 1), :] if dec else mod_ref[0:1, :]
    x = x_ref[...]
    row = lambda ref: ref[l:l + 1, :]

    lam_init = 0.8 - 0.6 * math.exp(-0.3 * l)
    lp = lamp_ref[...]
    lam = (jnp.exp(jnp.sum(lp[0:1] * lp[1:2], axis=-1, keepdims=True))
           - jnp.exp(jnp.sum(lp[2:3] * lp[3:4], axis=-1, keepdims=True)) + lam_init)
    gain = row(subln_ref) * (1.0 - lam_init)

    zpad = jnp.zeros((HALO, 256), F32)
    pieces = []
    if dec:
        x1_ref, h2_ref, *cast_out, vp_ref = outs
        if next_mod:
            *cast_out, modn_ref = cast_out
            modn_ref[...] = _ada_slab(*ada_in, l + 1)
        ca, cg = ql_ref[:, 0:256], ql_ref[:, 256:512]
        u_pool, dq, wq = ql_ref[:, 512:768], ql_ref[:, 768:1024], ql_ref[:, 1024:1280]
        q = pl.program_id(1)
        nq = pl.num_programs(1)
        pv = (q > 0).astype(F32)
        nv = (q < nq - 1).astype(F32)
        pool_pad = jnp.concatenate([pool_p_ref[...] * pv, u_pool, pool_n_ref[...] * nv], axis=0)
        cp = conv_p_ref[...] * pv
        cn = conv_n_ref[...] * nv
        apad = jnp.concatenate([cp[:, :256], ca, cn[:, :256]], axis=0)
        gpad = jnp.concatenate([cp[:, 256:], cg, cn[:, 256:]], axis=0)

        @pl.when(q == 0)
        def _():
            for h in range(4):
                vp_ref[h, 0:PAST_LEN, :] = _ones_outside(cdv_ref[...], h * 64, h * 64 + 64)
                vp_ref[h, PAST_LEN:, :] = _ones_outside(kv_ref[:, 256:512], h * 64, h * 64 + 64)

        dks = [cdk_ref[...], kv_ref[:, 0:256]]
        vps = lambda h: [vp_ref[h, 0:PAST_LEN, :], vp_ref[h, PAST_LEN:, :]]
        band = tq + 2 * WINDOW
        start = pl.multiple_of(jnp.clip(q * tq - WINDOW, 0, DEC_SEQ - band), WINDOW)
        wkb = kv_ref[pl.ds(start, band), 512:640]
        wvb = kv_ref[pl.ds(start, band), 640:768]
        qpos = q * tq + (lax.broadcasted_iota(jnp.int32, (2 * tq, band), 0) & (tq - 1))
        kpos = start + lax.broadcasted_iota(jnp.int32, (2 * tq, band), 1)
        wmask = jnp.abs(qpos - kpos) <= WINDOW
        yw = _win_attn(wq, [wkb, cwk_ref[...]], [wmask, None], [wvb, cwv_ref[...]], sink_ref, 4 * l, tq, group=1)
        pieces.append((_pool(pool_pad, u_pool, invcnt_ref[...], tq),
                       _diff_attn(dq, dks, vps, lam, gain, tq, group=4), yw[0], yw[1],
                       _conv_module(apad, gpad, dw_ref, row(dwb_ref), row(lng_ref), row(lnb_ref), tq)))
    else:
        x1_ref, h2_ref, kd_ref, vd_ref, kw_ref, vw_ref, *cast_out = outs
        h = _modulated_norm(x, row(gin_ref), mod, 0, 1)
        proj = _dot(h.astype(BF16), win_ref[...])
        for s in range(x.shape[0] // tq):
            ps = proj[s * tq:(s + 1) * tq]
            u_pool, dq = ps[:, 0:256], ps[:, 256:512]
            dk, dv = ps[:, 512:768], ps[:, 768:1024]
            wk, wv = ps[:, 1280:1408], ps[:, 1408:1536]
            ca, cg = ps[:, 1536:1792], ps[:, 1792:2048]
            wq = jnp.concatenate(_swap_middle_heads(ps[:, 1024:1152], ps[:, 1152:1280]), axis=1)
            for ref, new in ((kd_ref, dk.T), (vd_ref, dv.T), (kw_ref, wk.T), (vw_ref, wv.T)):
                if n_alias:
                    ref[s] = new
                else:
                    for ll in range(DEPTH):
                        ref[s, ll] = new if ll == l else jnp.zeros_like(new)
            pool_pad = jnp.concatenate([zpad, u_pool, zpad], axis=0)
            apad = jnp.concatenate([zpad, ca, zpad], axis=0)
            gpad = jnp.concatenate([zpad, cg, zpad], axis=0)
            dvb = dv.astype(BF16)
            vps = lambda h, dvb=dvb: [_ones_outside(dvb, h * 64, h * 64 + 64)]
            yw = _win_attn(wq, [wk.astype(BF16)], [None], [wv.astype(BF16)], sink_ref, 4 * l, tq, group=2)
            pieces.append((_pool(pool_pad, u_pool, invcnt_ref[...], tq),
                           _diff_attn(dq, [dk.astype(BF16)], vps, lam, gain, tq, group=4), yw[0], yw[1],
                           _conv_module(apad, gpad, dw_ref, row(dwb_ref), row(lng_ref), row(lnb_ref), tq)))

    for src, dst in zip(cast_in, cast_out):
        dst[...] = src[...].astype(BF16)

    cat = lambda k: jnp.concatenate([t[k] for t in pieces], axis=0).astype(BF16)
    y_pool = _dot(cat(0), wbd_ref[...]) * row(pscale_ref)
    y_conv = _dot(cat(4), pw_ref[...])
    o = (_dot(y_pool.astype(BF16), wout_ref[0:256, :])
         + _dot(cat(1), wout_ref[256:512, :])
         + _dot(cat(2), wout_ref[512:640, :])
         + _dot(cat(3), wout_ref[640:768, :])
         + _dot(y_conv.astype(BF16), wout_ref[768:1024, :]))
    x1 = x + mod[:, 2 * d:3 * d] * _rms(o, row(gpost_ref))
    x1_ref[...] = x1
    h2_ref[...] = _modulated_norm(x1, row(gpre_ref), mod, 3, 4).astype(BF16)


def _mixer_param_specs(l):
    return [
        _layer_spec((256, 256), l),
        _rows_spec(256),
        _layer_spec((4, DIFF_QK), l),
        _rows_spec(256),
        pl.BlockSpec(memory_space=pltpu.SMEM),
        _layer_spec((CONV_K, 256), l),
        _rows_spec(256),
        _rows_spec(256),
        _rows_spec(256),
        _layer_spec((256, 256), l),
        _whole_spec((D_MODEL, D_MODEL)),
        _rows_spec(D_MODEL),
        _rows_spec(D_MODEL),
    ]


def _mixer_param_args(P, l):
    return [P["pool_wbd"], P["pool_scale"], P["lam_p"], P["subln"], P["sink"],
            P["conv_dw"], P["conv_dw_b"], P["conv_ln_g"], P["conv_ln_b"], P["conv_pw"], P["w_out"][l],
            P["g_post_mix"], P["g_pre_ffn"]]


def _cast_specs(to_cast, cast_layer, steps, step_of):
    ins = [pl.BlockSpec((None, w.shape[1] // steps, w.shape[2]),
                        lambda *g: (cast_layer, step_of(*g), 0)) for w in to_cast]
    outs = [pl.BlockSpec((w.shape[1] // steps, w.shape[2]), lambda *g: (step_of(*g), 0)) for w in to_cast]
    shapes = [jax.ShapeDtypeStruct(w.shape[1:], BF16) for w in to_cast]
    return ins, outs, shapes


def _mixer_ctx(x, mod, P, l, to_cast=(), cast_layer=0, caches=()):
    rows = x.shape[0]
    tq = SEQ
    ns = CTX_SEQS_PER_STEP
    row_map = lambda i: (i, 0)
    seq_map = lambda i: (i, l, 0, 0)
    cast_in, cast_out, cast_shapes = _cast_specs(to_cast, cast_layer, rows // (ns * tq), lambda i: i)
    in_specs = [
        pl.BlockSpec((ns * tq, D_MODEL), row_map),
        _mod_spec(),
        _rows_spec(D_MODEL),
        _whole_spec((D_MODEL, IN_WIDTH)),
        pl.BlockSpec((tq, 256), lambda i: (0, 0)),
    ] + _mixer_param_specs(l)
    args = [x, mod, P["g_pre_mix"], P["w_in"][l], P["invcnt_ctx"]] + _mixer_param_args(P, l) + list(to_cast)
    first_cache_out = 2
    aliases = {len(args) + k: first_cache_out + k for k in range(len(caches))}
    return pl.pallas_call(
        functools.partial(_mixer_kernel, dec=False, l=l, tq=tq, n_cast=len(to_cast), n_alias=len(caches)),
        out_shape=[jax.ShapeDtypeStruct((rows, D_MODEL), F32),
                   jax.ShapeDtypeStruct((rows, D_MODEL), BF16),
                   jax.ShapeDtypeStruct((BATCH, DEPTH, 256, tq), F32),
                   jax.ShapeDtypeStruct((BATCH, DEPTH, 256, tq), F32),
                   jax.ShapeDtypeStruct((BATCH, DEPTH, 128, tq), F32),
                   jax.ShapeDtypeStruct((BATCH, DEPTH, 128, tq), F32)] + cast_shapes,
        grid=(rows // (ns * tq),),
        in_specs=in_specs + cast_in + [pl.BlockSpec(memory_space=pl.ANY)] * len(caches),
        out_specs=[pl.BlockSpec((ns * tq, D_MODEL), row_map), pl.BlockSpec((ns * tq, D_MODEL), row_map)]
        + [pl.BlockSpec((ns, None, f, tq), seq_map) if caches
           else pl.BlockSpec((ns, DEPTH, f, tq), lambda i: (i, 0, 0, 0)) for f in (256, 256, 128, 128)]
        + cast_out,
        input_output_aliases=aliases,
        compiler_params=pltpu.CompilerParams(
            dimension_semantics=("arbitrary",), vmem_limit_bytes=VMEM_LIMIT),
        name="mixer_ctx",
    )(*args, *caches)


def _mixer_dec(ql, kvb, caches, x, mod, P, l, to_cast=(), cast_layer=0, ada_next=()):
    rows = x.shape[0]
    tq = TQ_DEC
    nq = DEC_SEQ // tq
    cast_in_specs, cast_out_specs, cast_shapes = _cast_specs(
        to_cast, cast_layer, DEC_BATCH * nq, lambda b, q: b * nq + q)
    hb = tq // HALO
    nhb = rows // HALO
    row_map = lambda b, q: (b * nq + q, 0)

    def prev_map(col):
        return lambda b, q: (jnp.maximum((b * nq + q) * hb - 1, 0), col)

    def next_map(col):
        return lambda b, q: (jnp.minimum((b * nq + q + 1) * hb, nhb - 1), col)

    cache_map = lambda b, q: (b, l, 0, 0)
    in_specs = [
        pl.BlockSpec((tq, QL_WIDTH), row_map),
        pl.BlockSpec((HALO, 512), prev_map(0)),
        pl.BlockSpec((HALO, 512), next_map(0)),
        pl.BlockSpec((HALO, 256), prev_map(2)),
        pl.BlockSpec((HALO, 256), next_map(2)),
        pl.BlockSpec((DEC_SEQ, KV_WIDTH), lambda b, q: (b, 0)),
        pl.BlockSpec((None, None, PAST_LEN, 256), cache_map),
        pl.BlockSpec((None, None, PAST_LEN, 256), cache_map),
        pl.BlockSpec((None, None, PAST_LEN, 128), cache_map),
        pl.BlockSpec((None, None, PAST_LEN, 128), cache_map),
        pl.BlockSpec((tq, D_MODEL), row_map),
        _mod_spec(),
        pl.BlockSpec((tq, 256), lambda b, q: (q, 0)),
    ] + _mixer_param_specs(l)
    args = [ql, ql, ql, ql, ql, kvb, *caches, x, mod, P["invcnt_dec"]] + _mixer_param_args(P, l)
    ada_in_specs, ada_out_specs, ada_shapes = [], [], []
    if ada_next:
        steps = DEC_BATCH * nq
        ada_in_specs, ada_out = _ada_specs(l + 1, 6 * D_MODEL // steps, lambda b, q: b * nq + q)
        ada_out_specs, ada_shapes = [ada_out], [jax.ShapeDtypeStruct((8, 6 * D_MODEL), F32)]
    return pl.pallas_call(
        functools.partial(_mixer_kernel, dec=True, l=l, tq=tq, n_cast=len(to_cast), next_mod=bool(ada_next)),
        out_shape=[jax.ShapeDtypeStruct((rows, D_MODEL), F32),
                   jax.ShapeDtypeStruct((rows, D_MODEL), BF16)]
        + cast_shapes + ada_shapes,
        grid=(DEC_BATCH, nq),
        in_specs=in_specs + cast_in_specs + ada_in_specs,
        out_specs=[pl.BlockSpec((tq, D_MODEL), row_map), pl.BlockSpec((tq, D_MODEL), row_map)]
        + cast_out_specs + ada_out_specs,
        scratch_shapes=[pltpu.VMEM((4, PAST_LEN + DEC_SEQ, 256), BF16)],
        compiler_params=pltpu.CompilerParams(
            dimension_semantics=("arbitrary", "arbitrary"), vmem_limit_bytes=VMEM_LIMIT),
        name="mixer_dec",
    )(*args, *to_cast, *ada_next)


def _zero_rows(a, rows):
    sub = lax.broadcasted_iota(jnp.int32, (SUBLANES, 1), 0)
    pieces, done = [], 0
    for r in sorted(rows):
        g = r - r % SUBLANES
        pieces += [a[done:g], jnp.where(sub == r % SUBLANES, 0.0, a[g:g + SUBLANES])]
        done = g + SUBLANES
    pieces.append(a[done:])
    return jnp.concatenate([p for p in pieces if p.shape[0]], axis=0)


def _ffn_kernel(*refs, tm, seq_len, l, mod_row0, tiles_per_b, halo):
    i = pl.program_id(0)
    if halo:
        h_ref, hp_ref, hn_ref, *rest = refs
        t = i % tiles_per_b
        hp = jnp.where(t != 0, hp_ref[...], jnp.zeros_like(hp_ref))
        hn = jnp.where(t != tiles_per_b - 1, hn_ref[...], jnp.zeros_like(hn_ref))
        hx = jnp.concatenate([hp, h_ref[...], hn], axis=0)
    else:
        h_ref, *rest = refs
        hx = h_ref[...]
        starts = list(range(0, tm, seq_len))
        ends = [s + seq_len - 1 for s in starts]
    x_ref, mod_ref, wup_ref, dw_ref, dwb_ref, wd_ref, gpost_ref, o_ref, act_ref = rest
    for j in range(D_FF // FFN_CHUNK):
        ys = []
        for part in range(2):
            c0 = part * D_FF + j * FFN_CHUNK
            u = _dot(hx, wup_ref[:, c0:c0 + FFN_CHUNK])
            w = dw_ref[:, c0:c0 + FFN_CHUNK]
            if halo:
                below, mid, above = (u[FFN_HALO - 1:FFN_HALO - 1 + tm], u[FFN_HALO:FFN_HALO + tm],
                                     u[FFN_HALO + 1:FFN_HALO + 1 + tm])
            else:
                below, mid, above = (_zero_rows(pltpu.roll(u, 1, 0), starts), u,
                                     _zero_rows(pltpu.roll(u, tm - 1, 0), ends))
            ys.append(below * w[0:1] + mid * w[1:2] + above * w[2:3]
                      + dwb_ref[l:l + 1, c0:c0 + FFN_CHUNK])
        gate, up = ys
        act_ref[:, j * FFN_CHUNK:(j + 1) * FFN_CHUNK] = (gate * jax.nn.sigmoid(gate) * up).astype(BF16)
    o = _dot(act_ref[...], wd_ref[...])
    g2 = mod_ref[pl.ds(mod_row0 + i // tiles_per_b, 1), 5 * D_MODEL:6 * D_MODEL]
    o_ref[...] = x_ref[...] + g2 * _rms(o, gpost_ref[l:l + 1, :])


def _ffn(h2, x1, mod, P, l, seq_len, mod_row0, nb):
    rows = x1.shape[0]
    tm = TM_FFN
    tiles_per_b = rows // nb // tm
    hb = tm // FFN_HALO
    nhb = rows // FFN_HALO
    halo = tm % seq_len != 0
    h_specs = [pl.BlockSpec((tm, D_MODEL), lambda i: (i, 0))]
    if halo:
        h_specs += [pl.BlockSpec((FFN_HALO, D_MODEL), lambda i: (jnp.maximum(i * hb - 1, 0), 0)),
                    pl.BlockSpec((FFN_HALO, D_MODEL), lambda i: (jnp.minimum((i + 1) * hb, nhb - 1), 0))]
    return pl.pallas_call(
        functools.partial(_ffn_kernel, tm=tm, seq_len=seq_len, l=l, mod_row0=mod_row0,
                          tiles_per_b=tiles_per_b, halo=halo),
        out_shape=jax.ShapeDtypeStruct((rows, D_MODEL), F32),
        grid=(rows // tm,),
        in_specs=h_specs + [
            pl.BlockSpec((tm, D_MODEL), lambda i: (i, 0)),
            _mod_spec(),
            _whole_spec((D_MODEL, 2 * D_FF)),
            _layer_spec((3, 2 * D_FF), l),
            _rows_spec(2 * D_FF),
            _whole_spec((D_FF, D_MODEL)),
            _rows_spec(D_MODEL),
        ],
        out_specs=pl.BlockSpec((tm, D_MODEL), lambda i: (i, 0)),
        scratch_shapes=[pltpu.VMEM((tm, D_FF), BF16)],
        compiler_params=pltpu.CompilerParams(
            dimension_semantics=("arbitrary",), vmem_limit_bytes=VMEM_LIMIT),
        name="ffn",
    )(*([h2] * len(h_specs)), x1, mod, P["ffn_up"][l], P["ffn_dw"], P["ffn_dw_b"], P["ffn_down"][l],
      P["g_post_ffn"])


def _pool_inv_counts(seq_len):
    inv = np.zeros((seq_len, 256), np.float32)
    pos = np.arange(seq_len)
    for g, w in enumerate(POOL_WINDOWS):
        cnt = np.clip(pos - w // 2 + w, 0, seq_len) - np.clip(pos - w // 2, 0, seq_len)
        inv[:, g * 64:(g + 1) * 64] = (1.0 / cnt.astype(np.float64))[:, None].astype(np.float32)
    return jnp.asarray(inv, F32)


def _rope_tables(seq_len, dim):
    rows = seq_len // GRID_W
    r = np.repeat(np.arange(rows), GRID_W).astype(np.float32)
    col = np.tile(np.arange(GRID_W), rows).astype(np.float32)
    n = dim // 4
    inv = (ROPE_BASE ** (-np.arange(n) / n)).astype(np.float32)
    ang = np.concatenate([r[:, None] * inv[None], col[:, None] * inv[None]], axis=-1)
    cos, sin = np.cos(ang), np.sin(ang)
    reps = LANES // dim
    cos_t = np.tile(np.concatenate([cos, cos], axis=-1), (1, reps))
    sin_t = np.tile(np.concatenate([-sin, sin], axis=-1), (1, reps))
    return jnp.asarray(cos_t, F32), jnp.asarray(sin_t, F32)


def _cache_from_feature_major(a, heads):
    return jnp.transpose(a.reshape(BATCH, DEPTH, heads, 64, SEQ), (0, 1, 4, 2, 3))


def kernel(x_prompt, x_sample, c, cache_diff_k, cache_diff_v, cache_win_k, cache_win_v, c_ctx, w_ada, b_ada, g_pre_mix, g_post_mix, g_pre_ffn, g_post_ffn, w_in, w_out, pool_w, pool_scale, diff_lq1, diff_lk1, diff_lq2, diff_lk2, diff_subln, win_sink, conv_dw, conv_dw_b, conv_ln_g, conv_ln_b, conv_pw, ffn_up, ffn_dw, ffn_dw_b, ffn_down):
    d = D_MODEL
    xp = x_prompt.reshape(BATCH * SEQ, d)
    xs = x_sample.reshape(DEC_BATCH * DEC_SEQ, d)

    cond8 = jnp.concatenate([c_ctx[None, :], c, jnp.zeros((8 - 1 - DEC_BATCH, d), F32)], axis=0)
    mods = [_ada(cond8, w_ada, b_ada)]

    eye4 = jnp.eye(4, dtype=F32)
    P = dict(
        invcnt_ctx=_pool_inv_counts(SEQ), invcnt_dec=_pool_inv_counts(DEC_SEQ),
        g_pre_mix=g_pre_mix,
        pool_wbd=jnp.einsum("lgcd,gh->lgchd", pool_w, eye4).reshape(DEPTH, 256, 256).astype(BF16),
        pool_scale=pool_scale,
        lam_p=jnp.stack([diff_lq1, diff_lk1, diff_lq2, diff_lk2], axis=1),
        subln=jnp.tile(diff_subln, (1, 4)),
        sink=win_sink.reshape(DEPTH * 4),
        conv_dw=conv_dw, conv_dw_b=conv_dw_b, conv_ln_g=conv_ln_g, conv_ln_b=conv_ln_b,
        conv_pw=conv_pw.astype(BF16),
        g_post_mix=g_post_mix, g_pre_ffn=g_pre_ffn,
        ffn_dw=ffn_dw, ffn_dw_b=ffn_dw_b, g_post_ffn=g_post_ffn,
    )
    mix_w = dict(w_in=w_in, w_out=w_out)
    ffn_w = dict(ffn_up=ffn_up, ffn_down=ffn_down)
    for name, w in mix_w.items():
        P[name] = [w[0].astype(BF16)]
    for name in ffn_w:
        P[name] = []
    rope_tabs = _rope_tables(DEC_SEQ, DIFF_QK) + _rope_tables(DEC_SEQ, WIN_HD)
    caches = (cache_diff_k.reshape(DEC_BATCH, DEPTH, PAST_LEN, 256).astype(BF16),
              cache_diff_v.reshape(DEC_BATCH, DEPTH, PAST_LEN, 256).astype(BF16),
              cache_win_k.reshape(DEC_BATCH, DEPTH, PAST_LEN, 128).astype(BF16),
              cache_win_v.reshape(DEC_BATCH, DEPTH, PAST_LEN, 128).astype(BF16))

    new_caches = ()
    for l in range(DEPTH):
        mod = mods[l]
        first = list(ffn_w.values()) if l == 0 else []
        xp1, hp2, *rest = _mixer_ctx(xp, mod, P, l, first, l, new_caches)
        new_caches, cast = tuple(rest[:4]), rest[4:]
        for name, w in zip(ffn_w, cast):
            P[name].append(w)
        xp = _ffn(hp2, xp1, mod, P, l, SEQ, 0, 1)

        ql, kvb = _inproj_dec(xs, mod, P["g_pre_mix"], P["w_in"][l], l, rope_tabs)
        big = {**mix_w, **ffn_w}
        more = l + 1 < DEPTH
        nxt = list(big.values()) if more else []
        xs1, hs2, *cast = _mixer_dec(ql, kvb, caches, xs, mod, P, l, nxt, l + 1,
                                     (cond8, w_ada, b_ada) if more else ())
        for name, w in zip(big, cast):
            P[name].append(w)
        mods += cast[len(nxt):]
        xs = _ffn(hs2, xs1, mod, P, l, DEC_SEQ, 1, DEC_BATCH)

    return (xp.reshape(BATCH, SEQ, d), xs.reshape(DEC_BATCH, DEC_SEQ, d),
            _cache_from_feature_major(new_caches[0], 4), _cache_from_feature_major(new_caches[1], 4),
            _cache_from_feature_major(new_caches[2], 2), _cache_from_feature_major(new_caches[3], 2))
```

```python
import functools
import math

import jax
import jax.numpy as jnp
import numpy as np
from jax import lax
from jax.experimental import pallas as pl
from jax.experimental.pallas import tpu as pltpu

F32 = jnp.float32
BF16 = jnp.bfloat16

D_MODEL = 1024
BATCH = 16
SEQ = 256
DEPTH = 2
DEC_BATCH = 2
DEC_SEQ = 2048
PAST_LEN = 256
GRID_W = 64
POOL_WINDOWS = (2, 4, 8, 16)
DIFF_QK = 32
WIN_HD = 64
WINDOW = 128
CONV_K = 31
IN_WIDTH = 2048
D_FF = 2816
ROPE_BASE = 10000.0
EPS = 1e-6
LOG2E = 1.4426950408889634

LANES = 128
SUBLANES = 8
HALO = 16
FFN_HALO = 16
FFN_CHUNK = 256
KV_WIDTH = 768
QL_WIDTH = 1280
VMEM_LIMIT = 56 * 1024 * 1024

TM_IN = 512
CTX_SEQS_PER_STEP = 2
TQ_DEC = 256
TM_FFN = 1024


def _rms(x, g):
    return x * lax.rsqrt(jnp.mean(x * x, axis=-1, keepdims=True) + EPS) * g


def _dot(a, b):
    return jnp.dot(a, b, preferred_element_type=F32)


def _dot_nt(a, b):
    return lax.dot_general(a, b, (((1,), (1,)), ((), ())), preferred_element_type=F32)


def _lane_range(shape, lo, hi):
    lane = lax.broadcasted_iota(jnp.int32, shape, 1)
    return (lane >= lo) & (lane < hi)


def _swap_middle_heads(t0, t1):
    lo = _lane_range(t0.shape, 0, 64)
    return (jnp.where(lo, t0, pltpu.roll(t1, 64, 1)),
            jnp.where(lo, pltpu.roll(t0, 64, 1), t1))


def _modulated_norm(x, g, mod, shift_col, scale_col):
    d = D_MODEL
    return (_rms(x, g) * (1.0 + mod[:, scale_col * d:(scale_col + 1) * d])
            + mod[:, shift_col * d:(shift_col + 1) * d])


def _layer_spec(shape, l):
    nz = len(shape)
    return pl.BlockSpec((None,) + tuple(shape), lambda *_: (l,) + (0,) * nz,
                        pipeline_mode=pl.Buffered(1))


def _whole_spec(shape):
    return pl.BlockSpec(tuple(shape), lambda *_: (0,) * len(shape), pipeline_mode=pl.Buffered(1))


def _rows_spec(width):
    return pl.BlockSpec((DEPTH, width), lambda *_: (0, 0), pipeline_mode=pl.Buffered(1))


def _mod_spec():
    return pl.BlockSpec((8, 6 * D_MODEL), lambda *_: (0, 0), pipeline_mode=pl.Buffered(1))


def _ada_slab(c_ref, w_ref, b_ref, layer):
    c = c_ref[...]
    s = (c * jax.nn.sigmoid(c)).astype(BF16)
    return _dot(s, w_ref[...].astype(BF16)) + b_ref[layer:layer + 1, :]


def _ada_kernel(c_ref, w_ref, b_ref, o_ref):
    o_ref[...] = _ada_slab(c_ref, w_ref, b_ref, 0)


def _ada_specs(layer, tn, step_of):
    ins = [pl.BlockSpec((8, D_MODEL), lambda *g: (0, 0)),
           pl.BlockSpec((None, D_MODEL, tn), lambda *g: (layer, 0, step_of(*g))),
           pl.BlockSpec((DEPTH, tn), lambda *g: (0, step_of(*g)))]
    return ins, pl.BlockSpec((8, tn), lambda *g: (0, step_of(*g)))


def _ada(cond8, w_ada, b_ada):
    tn = 1024
    in_specs, out_spec = _ada_specs(0, tn, lambda j: j)
    return pl.pallas_call(
        _ada_kernel,
        out_shape=jax.ShapeDtypeStruct((8, 6 * D_MODEL), F32),
        grid=(6 * D_MODEL // tn,),
        in_specs=in_specs,
        out_specs=out_spec,
        compiler_params=pltpu.CompilerParams(
            dimension_semantics=("arbitrary",), vmem_limit_bytes=VMEM_LIMIT),
        name="ada_mod",
    )(cond8, w_ada, b_ada)


def _rope_tile(v, cos, sin_signed, half):
    is_a = (lax.broadcasted_iota(jnp.int32, v.shape, 1) % (2 * half)) < half
    partner = jnp.where(is_a, pltpu.roll(v, LANES - half, 1), pltpu.roll(v, half, 1))
    return v * cos + partner * sin_signed


def _inproj_dec_kernel(x_ref, mod_ref, g_ref, w_ref, cd_ref, sd_ref, cw_ref, sw_ref, ql_ref, kv_ref,
                       *, l, tiles_per_b):
    mod = mod_ref[pl.ds(1 + pl.program_id(0) // tiles_per_b, 1), :]
    h = _modulated_norm(x_ref[...], g_ref[l:l + 1, :], mod, 0, 1)
    proj = _dot(h.astype(BF16), w_ref[...])
    tiles = [proj[:, i * LANES:(i + 1) * LANES] for i in range(IN_WIDTH // LANES)]
    cd, sd, cw, sw = cd_ref[...], sd_ref[...], cw_ref[...], sw_ref[...]
    for i in (2, 3, 4, 5):
        tiles[i] = _rope_tile(tiles[i], cd, sd, DIFF_QK // 2)
    for i in (8, 9, 10):
        tiles[i] = _rope_tile(tiles[i], cw, sw, WIN_HD // 2)
    tiles[8], tiles[9] = _swap_middle_heads(tiles[8], tiles[9])
    for j, i in enumerate((12, 13, 14, 15, 0, 1, 2, 3, 8, 9)):
        ql_ref[:, j * LANES:(j + 1) * LANES] = tiles[i]
    for j, i in enumerate((4, 5, 6, 7, 10, 11)):
        kv_ref[:, j * LANES:(j + 1) * LANES] = tiles[i].astype(BF16)


def _inproj_dec(x, mod, g, w_in_b, l, rope_tabs):
    rows = x.shape[0]
    tm = TM_IN
    tiles_per_b = DEC_SEQ // tm
    in_specs = [
        pl.BlockSpec((tm, D_MODEL), lambda i: (i, 0)),
        _mod_spec(),
        _rows_spec(D_MODEL),
        _whole_spec((D_MODEL, IN_WIDTH)),
    ] + [pl.BlockSpec((tm, LANES), lambda i: (i % tiles_per_b, 0)) for _ in rope_tabs]
    return pl.pallas_call(
        functools.partial(_inproj_dec_kernel, l=l, tiles_per_b=tiles_per_b),
        out_shape=[jax.ShapeDtypeStruct((rows, QL_WIDTH), F32),
                   jax.ShapeDtypeStruct((rows, KV_WIDTH), BF16)],
        grid=(rows // tm,),
        in_specs=in_specs,
        out_specs=[pl.BlockSpec((tm, QL_WIDTH), lambda i: (i, 0)),
                   pl.BlockSpec((tm, KV_WIDTH), lambda i: (i, 0))],
        compiler_params=pltpu.CompilerParams(
            dimension_semantics=("arbitrary",), vmem_limit_bytes=VMEM_LIMIT),
        name="inproj_dec",
    )(x, mod, g, w_in_b, *rope_tabs)


def _pool(xpad, u, invcnt, tq):
    rows = xpad.shape[0]
    p = xpad
    sums = []
    for step, w in zip((1, 2, 4, 8), POOL_WINDOWS):
        p = p + pltpu.roll(p, step, 0)
        off = HALO + w // 2 - 1
        r = off % SUBLANES
        sh = p if r == 0 else pltpu.roll(p, rows - r, 0)
        sums.append(sh[off - r:off - r + tq])
    shape = u.shape
    s = jnp.where(_lane_range(shape, 0, 64), sums[0],
                  jnp.where(_lane_range(shape, 64, 128), sums[1],
                            jnp.where(_lane_range(shape, 128, 192), sums[2], sums[3])))
    return s * invcnt - u


def _conv_module(apad, gpad, dw_ref, dwb, lng, lnb, tq):
    u = apad * jax.nn.sigmoid(gpad)
    rows = u.shape[0]
    acc = jnp.broadcast_to(dwb, (tq, u.shape[1]))
    for r in range(SUBLANES):
        ur = u if r == 0 else pltpu.roll(u, rows - r, 0)
        for a8 in range(4):
            k = SUBLANES * a8 + r - 1
            if 0 <= k < CONV_K:
                acc = acc + ur[SUBLANES * a8:SUBLANES * a8 + tq] * dw_ref[k:k + 1, :]
    mu = jnp.mean(acc, axis=-1, keepdims=True)
    xc = acc - mu
    var = jnp.mean(xc * xc, axis=-1, keepdims=True)
    y = xc * lax.rsqrt(var + EPS) * lng + lnb
    return y * jax.nn.sigmoid(y)


def _ones_outside(v, lo, hi):
    m = jnp.where(_lane_range((1, v.shape[1]), lo, hi), 1.0, 0.0).astype(BF16)
    return v * m + (1.0 - m)


def _diff_attn(dq, ks, vps, lam, gain, tq, group):
    dqs = dq * (DIFF_QK ** -0.5 * LOG2E)
    ytiles = [jnp.zeros((tq, LANES), F32), jnp.zeros((tq, LANES), F32)]
    for h0 in range(0, 4, group):
        qs = []
        for h in range(h0, h0 + group):
            for lo in (h * 64, h * 64 + DIFF_QK):
                qs.append(jnp.where(_lane_range(dqs.shape, lo, lo + DIFF_QK), dqs, 0.0))
        qg = jnp.concatenate(qs, axis=0).astype(BF16)
        ss = [_dot_nt(qg, k) for k in ks]
        mx = functools.reduce(jnp.maximum, [jnp.max(s, axis=-1, keepdims=True) for s in ss])
        es = [jnp.exp2(s - mx).astype(BF16) for s in ss]
        for h in range(h0, h0 + group):
            r0 = (h - h0) * 2 * tq
            o = None
            for e, vp in zip(es, vps(h)):
                t = _dot(e[r0:r0 + 2 * tq], vp)
                o = t if o is None else o + t
            tile = h // 2
            ot = o[:, tile * LANES:(tile + 1) * LANES]
            hlo = (h % 2) * 64
            den = jnp.max(jnp.where(_lane_range(ot.shape, 64 - hlo, 128 - hlo), ot, 0.0), axis=-1, keepdims=True)
            r = ot / den
            od = jnp.where(_lane_range((tq, LANES), hlo, hlo + 64), r[:tq] - lam * r[tq:], 0.0)
            ms = jnp.sum(od * od, axis=-1, keepdims=True) * (1.0 / 64.0)
            ytiles[tile] = ytiles[tile] + od * lax.rsqrt(ms + EPS)
    return jnp.concatenate(ytiles, axis=1) * gain


def _win_attn(wq, ks, masks, vs, sink_ref, sink_base, tq, group):
    wqs = wq * (WIN_HD ** -0.5 * LOG2E)
    t0, t1 = wqs[:, :LANES], wqs[:, LANES:]
    y0 = jnp.zeros((tq, LANES), F32)
    y1 = jnp.zeros((tq, LANES), F32)
    rows = 2 * tq * group
    row = lax.broadcasted_iota(jnp.int32, (rows, 1), 0)
    for j0 in range(0, 2, group):
        qs = []
        sk = jnp.zeros((rows, 1), F32)
        for j in range(j0, j0 + group):
            lm = _lane_range(t0.shape, j * 64, j * 64 + 64)
            qs += [jnp.where(lm, t0, 0.0), jnp.where(lm, t1, 0.0)]
            for g in range(2):
                r0 = ((j - j0) * 2 + g) * tq
                sk = jnp.where((row >= r0) & (row < r0 + tq), sink_ref[sink_base + 2 * j + g], sk)
        sk = sk * LOG2E
        qg = jnp.concatenate(qs, axis=0).astype(BF16)
        ss = []
        for k, m in zip(ks, masks):
            s = _dot_nt(qg, k)
            if m is not None:
                s = jnp.where(m, s, -1e30)
            ss.append(s)
        mx = functools.reduce(jnp.maximum, [jnp.max(s, axis=-1, keepdims=True) for s in ss])
        mx = jnp.maximum(mx, sk)
        es = [jnp.exp2(s - mx).astype(BF16) for s in ss]
        esink = jnp.exp2(sk - mx)
        for j in range(j0, j0 + group):
            r0 = (j - j0) * 2 * tq
            o = None
            for e, v in zip(es, vs):
                t = _dot(e[r0:r0 + 2 * tq], _ones_outside(v, j * 64, j * 64 + 64))
                o = t if o is None else o + t
            other = _lane_range(o.shape, 64 - j * 64, 128 - j * 64)
            den = jnp.max(jnp.where(other, o, 0.0), axis=-1, keepdims=True)
            r = o / (den + esink[r0:r0 + 2 * tq])
            lm = _lane_range((tq, LANES), j * 64, j * 64 + 64)
            y0 = y0 + jnp.where(lm, r[:tq], 0.0)
            y1 = y1 + jnp.where(lm, r[tq:], 0.0)
    return _swap_middle_heads(y0, y1)


def _mixer_kernel(*refs, dec, l, tq, n_cast=0, n_alias=0, next_mod=False):
    if dec:
        (ql_ref, conv_p_ref, conv_n_ref, pool_p_ref, pool_n_ref, kv_ref,
         cdk_ref, cdv_ref, cwk_ref, cwv_ref, x_ref, mod_ref, *rest) = refs
    else:
        (x_ref, mod_ref, gin_ref, win_ref, *rest) = refs
    (invcnt_ref, wbd_ref, pscale_ref, lamp_ref, subln_ref, sink_ref,
     dw_ref, dwb_ref, lng_ref, lnb_ref, pw_ref, wout_ref, gpost_ref, gpre_ref, *tail) = rest
    n_ada = 3 if next_mod else 0
    cast_in, ada_in = tail[:n_cast], tail[n_cast:n_cast + n_ada]
    outs = tail[n_cast + n_ada + n_alias:]
    d = D_MODEL
    mod = mod_ref[pl.ds(1 + pl.program_id(0), 1), :] if dec else mod_ref[0:1, :]
    x = x_ref[...]
    row = lambda ref: ref[l:l + 1, :]

    lam_init = 0.8 - 0.6 * math.exp(-0.3 * l)
    lp = lamp_ref[...]
    lam = (jnp.exp(jnp.sum(lp[0:1] * lp[1:2], axis=-1, keepdims=True))
           - jnp.exp(jnp.sum(lp[2:3] * lp[3:4], axis=-1, keepdims=True)) + lam_init)
    gain = row(subln_ref) * (1.0 - lam_init)

    zpad = jnp.zeros((HALO, 256), F32)
    pieces = []
    if dec:
        x1_ref, h2_ref, *cast_out, vp_ref = outs
        if next_mod:
            *cast_out, modn_ref = cast_out
            modn_ref[...] = _ada_slab(*ada_in, l + 1)
        ca, cg = ql_ref[:, 0:256], ql_ref[:, 256:512]
        u_pool, dq, wq = ql_ref[:, 512:768], ql_ref[:, 768:1024], ql_ref[:, 1024:1280]
        q = pl.program_id(1)
        nq = pl.num_programs(1)
        pv = (q > 0).astype(F32)
        nv = (q < nq - 1).astype(F32)
        pool_pad = jnp.concatenate([pool_p_ref[...] * pv, u_pool, pool_n_ref[...] * nv], axis=0)
        cp = conv_p_ref[...] * pv
        cn = conv_n_ref[...] * nv
        apad = jnp.concatenate([cp[:, :256], ca, cn[:, :256]], axis=0)
        gpad = jnp.concatenate([cp[:, 256:], cg, cn[:, 256:]], axis=0)

        @pl.when(q == 0)
        def _():
            for h in range(4):
                vp_ref[h, 0:PAST_LEN, :] = _ones_outside(cdv_ref[...], h * 64, h * 64 + 64)
                vp_ref[h, PAST_LEN:, :] = _ones_outside(kv_ref[:, 256:512], h * 64, h * 64 + 64)

        dks = [cdk_ref[...], kv_ref[:, 0:256]]
        vps = lambda h: [vp_ref[h, 0:PAST_LEN, :], vp_ref[h, PAST_LEN:, :]]
        band = tq + 2 * WINDOW
        start = pl.multiple_of(jnp.clip(q * tq - WINDOW, 0, DEC_SEQ - band), WINDOW)
        wkb = kv_ref[pl.ds(start, band), 512:640]
        wvb = kv_ref[pl.ds(start, band), 640:768]
        qpos = q * tq + (lax.broadcasted_iota(jnp.int32, (2 * tq, band), 0) & (tq - 1))
        kpos = start + lax.broadcasted_iota(jnp.int32, (2 * tq, band), 1)
        wmask = jnp.abs(qpos - kpos) <= WINDOW
        yw = _win_attn(wq, [wkb, cwk_ref[...]], [wmask, None], [wvb, cwv_ref[...]], sink_ref, 4 * l, tq, group=1)
        pieces.append((_pool(pool_pad, u_pool, invcnt_ref[...], tq),
                       _diff_attn(dq, dks, vps, lam, gain, tq, group=4 * TQ_DEC // tq), yw[0], yw[1],
                       _conv_module(apad, gpad, dw_ref, row(dwb_ref), row(lng_ref), row(lnb_ref), tq)))
    else:
        x1_ref, h2_ref, kd_ref, vd_ref, kw_ref, vw_ref, *cast_out = outs
        h = _modulated_norm(x, row(gin_ref), mod, 0, 1)
        proj = _dot(h.astype(BF16), win_ref[...])
        for s in range(x.shape[0] // tq):
            ps = proj[s * tq:(s + 1) * tq]
            u_pool, dq = ps[:, 0:256], ps[:, 256:512]
            dk, dv = ps[:, 512:768], ps[:, 768:1024]
            wk, wv = ps[:, 1280:1408], ps[:, 1408:1536]
            ca, cg = ps[:, 1536:1792], ps[:, 1792:2048]
            wq = jnp.concatenate(_swap_middle_heads(ps[:, 1024:1152], ps[:, 1152:1280]), axis=1)
            for ref, new in ((kd_ref, dk.T), (vd_ref, dv.T), (kw_ref, wk.T), (vw_ref, wv.T)):
                if n_alias:
                    ref[s] = new
                else:
                    for ll in range(DEPTH):
                        ref[s, ll] = new if ll == l else jnp.zeros_like(new)
            pool_pad = jnp.concatenate([zpad, u_pool, zpad], axis=0)
            apad = jnp.concatenate([zpad, ca, zpad], axis=0)
            gpad = jnp.concatenate([zpad, cg, zpad], axis=0)
            dvb = dv.astype(BF16)
            vps = lambda h, dvb=dvb: [_ones_outside(dvb, h * 64, h * 64 + 64)]
            yw = _win_attn(wq, [wk.astype(BF16)], [None], [wv.astype(BF16)], sink_ref, 4 * l, tq, group=2)
            pieces.append((_pool(pool_pad, u_pool, invcnt_ref[...], tq),
                           _diff_attn(dq, [dk.astype(BF16)], vps, lam, gain, tq, group=4), yw[0], yw[1],
                           _conv_module(apad, gpad, dw_ref, row(dwb_ref), row(lng_ref), row(lnb_ref), tq)))

    for src, dst in zip(cast_in, cast_out):
        dst[...] = src[...].astype(BF16)

    cat = lambda k: jnp.concatenate([t[k] for t in pieces], axis=0).astype(BF16)
    y_pool = _dot(cat(0), wbd_ref[...]) * row(pscale_ref)
    y_conv = _dot(cat(4), pw_ref[...])
    o = (_dot(y_pool.astype(BF16), wout_ref[0:256, :])
         + _dot(cat(1), wout_ref[256:512, :])
         + _dot(cat(2), wout_ref[512:640, :])
         + _dot(cat(3), wout_ref[640:768, :])
         + _dot(y_conv.astype(BF16), wout_ref[768:1024, :]))
    x1 = x + mod[:, 2 * d:3 * d] * _rms(o, row(gpost_ref))
    x1_ref[...] = x1
    h2_ref[...] = _modulated_norm(x1, row(gpre_ref), mod, 3, 4).astype(BF16)


def _mixer_param_specs(l):
    return [
        _layer_spec((256, 256), l),
        _rows_spec(256),
        _layer_spec((4, DIFF_QK), l),
        _rows_spec(256),
        pl.BlockSpec(memory_space=pltpu.SMEM),
        _layer_spec((CONV_K, 256), l),
        _rows_spec(256),
        _rows_spec(256),
        _rows_spec(256),
        _layer_spec((256, 256), l),
        _whole_spec((D_MODEL, D_MODEL)),
        _rows_spec(D_MODEL),
        _rows_spec(D_MODEL),
    ]


def _mixer_param_args(P, l):
    return [P["pool_wbd"], P["pool_scale"], P["lam_p"], P["subln"], P["sink"],
            P["conv_dw"], P["conv_dw_b"], P["conv_ln_g"], P["conv_ln_b"], P["conv_pw"], P["w_out"][l],
            P["g_post_mix"], P["g_pre_ffn"]]


def _cast_specs(to_cast, cast_layer, steps, step_of):
    ins = [pl.BlockSpec((None, w.shape[1] // steps, w.shape[2]),
                        lambda *g: (cast_layer, step_of(*g), 0)) for w in to_cast]
    outs = [pl.BlockSpec((w.shape[1] // steps, w.shape[2]), lambda *g: (step_of(*g), 0)) for w in to_cast]
    shapes = [jax.ShapeDtypeStruct(w.shape[1:], BF16) for w in to_cast]
    return ins, outs, shapes


def _mixer_ctx(x, mod, P, l, to_cast=(), cast_layer=0, caches=()):
    rows = x.shape[0]
    tq = SEQ
    ns = CTX_SEQS_PER_STEP
    row_map = lambda i: (i, 0)
    seq_map = lambda i: (i, l, 0, 0)
    cast_in, cast_out, cast_shapes = _cast_specs(to_cast, cast_layer, rows // (ns * tq), lambda i: i)
    in_specs = [
        pl.BlockSpec((ns * tq, D_MODEL), row_map),
        _mod_spec(),
        _rows_spec(D_MODEL),
        _whole_spec((D_MODEL, IN_WIDTH)),
        pl.BlockSpec((tq, 256), lambda i: (0, 0)),
    ] + _mixer_param_specs(l)
    args = [x, mod, P["g_pre_mix"], P["w_in"][l], P["invcnt_ctx"]] + _mixer_param_args(P, l) + list(to_cast)
    first_cache_out = 2
    aliases = {len(args) + k: first_cache_out + k for k in range(len(caches))}
    return pl.pallas_call(
        functools.partial(_mixer_kernel, dec=False, l=l, tq=tq, n_cast=len(to_cast), n_alias=len(caches)),
        out_shape=[jax.ShapeDtypeStruct((rows, D_MODEL), F32),
                   jax.ShapeDtypeStruct((rows, D_MODEL), BF16),
                   jax.ShapeDtypeStruct((BATCH, DEPTH, 256, tq), F32),
                   jax.ShapeDtypeStruct((BATCH, DEPTH, 256, tq), F32),
                   jax.ShapeDtypeStruct((BATCH, DEPTH, 128, tq), F32),
                   jax.ShapeDtypeStruct((BATCH, DEPTH, 128, tq), F32)] + cast_shapes,
        grid=(rows // (ns * tq),),
        in_specs=in_specs + cast_in + [pl.BlockSpec(memory_space=pl.ANY)] * len(caches),
        out_specs=[pl.BlockSpec((ns * tq, D_MODEL), row_map), pl.BlockSpec((ns * tq, D_MODEL), row_map)]
        + [pl.BlockSpec((ns, None, f, tq), seq_map) if caches
           else pl.BlockSpec((ns, DEPTH, f, tq), lambda i: (i, 0, 0, 0)) for f in (256, 256, 128, 128)]
        + cast_out,
        input_output_aliases=aliases,
        compiler_params=pltpu.CompilerParams(
            dimension_semantics=("arbitrary",), vmem_limit_bytes=VMEM_LIMIT),
        name="mixer_ctx",
    )(*args, *caches)


def _mixer_dec(ql, kvb, caches, x, mod, P, l, to_cast=(), cast_layer=0, ada_next=()):
    rows = x.shape[0]
    tq = TQ_DEC if (to_cast or ada_next) else 2 * TQ_DEC
    nq = DEC_SEQ // tq
    cast_in_specs, cast_out_specs, cast_shapes = _cast_specs(
        to_cast, cast_layer, DEC_BATCH * nq, lambda b, q: b * nq + q)
    hb = tq // HALO
    nhb = rows // HALO
    row_map = lambda b, q: (b * nq + q, 0)

    def prev_map(col):
        return lambda b, q: (jnp.maximum((b * nq + q) * hb - 1, 0), col)

    def next_map(col):
        return lambda b, q: (jnp.minimum((b * nq + q + 1) * hb, nhb - 1), col)

    cache_map = lambda b, q: (b, l, 0, 0)
    in_specs = [
        pl.BlockSpec((tq, QL_WIDTH), row_map),
        pl.BlockSpec((HALO, 512), prev_map(0)),
        pl.BlockSpec((HALO, 512), next_map(0)),
        pl.BlockSpec((HALO, 256), prev_map(2)),
        pl.BlockSpec((HALO, 256), next_map(2)),
        pl.BlockSpec((DEC_SEQ, KV_WIDTH), lambda b, q: (b, 0)),
        pl.BlockSpec((None, None, PAST_LEN, 256), cache_map),
        pl.BlockSpec((None, None, PAST_LEN, 256), cache_map),
        pl.BlockSpec((None, None, PAST_LEN, 128), cache_map),
        pl.BlockSpec((None, None, PAST_LEN, 128), cache_map),
        pl.BlockSpec((tq, D_MODEL), row_map),
        _mod_spec(),
        pl.BlockSpec((tq, 256), lambda b, q: (q, 0)),
    ] + _mixer_param_specs(l)
    args = [ql, ql, ql, ql, ql, kvb, *caches, x, mod, P["invcnt_dec"]] + _mixer_param_args(P, l)
    ada_in_specs, ada_out_specs, ada_shapes = [], [], []
    if ada_next:
        steps = DEC_BATCH * nq
        ada_in_specs, ada_out = _ada_specs(l + 1, 6 * D_MODEL // steps, lambda b, q: b * nq + q)
        ada_out_specs, ada_shapes = [ada_out], [jax.ShapeDtypeStruct((8, 6 * D_MODEL), F32)]
    return pl.pallas_call(
        functools.partial(_mixer_kernel, dec=True, l=l, tq=tq, n_cast=len(to_cast), next_mod=bool(ada_next)),
        out_shape=[jax.ShapeDtypeStruct((rows, D_MODEL), F32),
                   jax.ShapeDtypeStruct((rows, D_MODEL), BF16)]
        + cast_shapes + ada_shapes,
        grid=(DEC_BATCH, nq),
        in_specs=in_specs + cast_in_specs + ada_in_specs,
        out_specs=[pl.BlockSpec((tq, D_MODEL), row_map), pl.BlockSpec((tq, D_MODEL), row_map)]
        + cast_out_specs + ada_out_specs,
        scratch_shapes=[pltpu.VMEM((4, PAST_LEN + DEC_SEQ, 256), BF16)],
        compiler_params=pltpu.CompilerParams(
            dimension_semantics=("arbitrary", "arbitrary"), vmem_limit_bytes=VMEM_LIMIT),
        name="mixer_dec",
    )(*args, *to_cast, *ada_next)


def _zero_rows(a, rows):
    sub = lax.broadcasted_iota(jnp.int32, (SUBLANES, 1), 0)
    pieces, done = [], 0
    for r in sorted(rows):
        g = r - r % SUBLANES
        pieces += [a[done:g], jnp.where(sub == r % SUBLANES, 0.0, a[g:g + SUBLANES])]
        done = g + SUBLANES
    pieces.append(a[done:])
    return jnp.concatenate([p for p in pieces if p.shape[0]], axis=0)


def _ffn_kernel(*refs, tm, seq_len, l, mod_row0, tiles_per_b, halo):
    i = pl.program_id(0)
    if halo:
        h_ref, hp_ref, hn_ref, *rest = refs
        t = i % tiles_per_b
        hp = jnp.where(t != 0, hp_ref[...], jnp.zeros_like(hp_ref))
        hn = jnp.where(t != tiles_per_b - 1, hn_ref[...], jnp.zeros_like(hn_ref))
        hx = jnp.concatenate([hp, h_ref[...], hn], axis=0)
    else:
        h_ref, *rest = refs
        hx = h_ref[...]
        starts = list(range(0, tm, seq_len))
        ends = [s + seq_len - 1 for s in starts]
    x_ref, mod_ref, wup_ref, dw_ref, dwb_ref, wd_ref, gpost_ref, o_ref, act_ref = rest
    for j in range(D_FF // FFN_CHUNK):
        ys = []
        for part in range(2):
            c0 = part * D_FF + j * FFN_CHUNK
            u = _dot(hx, wup_ref[:, c0:c0 + FFN_CHUNK])
            w = dw_ref[:, c0:c0 + FFN_CHUNK]
            if halo:
                below, mid, above = (u[FFN_HALO - 1:FFN_HALO - 1 + tm], u[FFN_HALO:FFN_HALO + tm],
                                     u[FFN_HALO + 1:FFN_HALO + 1 + tm])
            else:
                below, mid, above = (_zero_rows(pltpu.roll(u, 1, 0), starts), u,
                                     _zero_rows(pltpu.roll(u, tm - 1, 0), ends))
            ys.append(below * w[0:1] + mid * w[1:2] + above * w[2:3]
                      + dwb_ref[l:l + 1, c0:c0 + FFN_CHUNK])
        gate, up = ys
        act_ref[:, j * FFN_CHUNK:(j + 1) * FFN_CHUNK] = (gate * jax.nn.sigmoid(gate) * up).astype(BF16)
    o = _dot(act_ref[...], wd_ref[...])
    g2 = mod_ref[pl.ds(mod_row0 + i // tiles_per_b, 1), 5 * D_MODEL:6 * D_MODEL]
    o_ref[...] = x_ref[...] + g2 * _rms(o, gpost_ref[l:l + 1, :])


def _ffn(h2, x1, mod, P, l, seq_len, mod_row0, nb):
    rows = x1.shape[0]
    tm = TM_FFN
    tiles_per_b = rows // nb // tm
    hb = tm // FFN_HALO
    nhb = rows // FFN_HALO
    halo = tm % seq_len != 0
    h_specs = [pl.BlockSpec((tm, D_MODEL), lambda i: (i, 0))]
    if halo:
        h_specs += [pl.BlockSpec((FFN_HALO, D_MODEL), lambda i: (jnp.maximum(i * hb - 1, 0), 0)),
                    pl.BlockSpec((FFN_HALO, D_MODEL), lambda i: (jnp.minimum((i + 1) * hb, nhb - 1), 0))]
    return pl.pallas_call(
        functools.partial(_ffn_kernel, tm=tm, seq_len=seq_len, l=l, mod_row0=mod_row0,
                          tiles_per_b=tiles_per_b, halo=halo),
        out_shape=jax.ShapeDtypeStruct((rows, D_MODEL), F32),
        grid=(rows // tm,),
        in_specs=h_specs + [
            pl.BlockSpec((tm, D_MODEL), lambda i: (i, 0)),
            _mod_spec(),
            _whole_spec((D_MODEL, 2 * D_FF)),
            _layer_spec((3, 2 * D_FF), l),
            _rows_spec(2 * D_FF),
            _whole_spec((D_FF, D_MODEL)),
            _rows_spec(D_MODEL),
        ],
        out_specs=pl.BlockSpec((tm, D_MODEL), lambda i: (i, 0)),
        scratch_shapes=[pltpu.VMEM((tm, D_FF), BF16)],
        compiler_params=pltpu.CompilerParams(
            dimension_semantics=("arbitrary",), vmem_limit_bytes=VMEM_LIMIT),
        name="ffn",
    )(*([h2] * len(h_specs)), x1, mod, P["ffn_up"][l], P["ffn_dw"], P["ffn_dw_b"], P["ffn_down"][l],
      P["g_post_ffn"])


def _pool_inv_counts(seq_len):
    inv = np.zeros((seq_len, 256), np.float32)
    pos = np.arange(seq_len)
    for g, w in enumerate(POOL_WINDOWS):
        cnt = np.clip(pos - w // 2 + w, 0, seq_len) - np.clip(pos - w // 2, 0, seq_len)
        inv[:, g * 64:(g + 1) * 64] = (1.0 / cnt.astype(np.float64))[:, None].astype(np.float32)
    return jnp.asarray(inv, F32)


def _rope_tables(seq_len, dim):
    rows = seq_len // GRID_W
    r = np.repeat(np.arange(rows), GRID_W).astype(np.float32)
    col = np.tile(np.arange(GRID_W), rows).astype(np.float32)
    n = dim // 4
    inv = (ROPE_BASE ** (-np.arange(n) / n)).astype(np.float32)
    ang = np.concatenate([r[:, None] * inv[None], col[:, None] * inv[None]], axis=-1)
    cos, sin = np.cos(ang), np.sin(ang)
    reps = LANES // dim
    cos_t = np.tile(np.concatenate([cos, cos], axis=-1), (1, reps))
    sin_t = np.tile(np.concatenate([-sin, sin], axis=-1), (1, reps))
    return jnp.asarray(cos_t, F32), jnp.asarray(sin_t, F32)


def _cache_from_feature_major(a, heads):
    return jnp.transpose(a.reshape(BATCH, DEPTH, heads, 64, SEQ), (0, 1, 4, 2, 3))


def kernel(x_prompt, x_sample, c, cache_diff_k, cache_diff_v, cache_win_k, cache_win_v, c_ctx, w_ada, b_ada, g_pre_mix, g_post_mix, g_pre_ffn, g_post_ffn, w_in, w_out, pool_w, pool_scale, diff_lq1, diff_lk1, diff_lq2, diff_lk2, diff_subln, win_sink, conv_dw, conv_dw_b, conv_ln_g, conv_ln_b, conv_pw, ffn_up, ffn_dw, ffn_dw_b, ffn_down):
    d = D_MODEL
    xp = x_prompt.reshape(BATCH * SEQ, d)
    xs = x_sample.reshape(DEC_BATCH * DEC_SEQ, d)

    cond8 = jnp.concatenate([c_ctx[None, :], c, jnp.zeros((8 - 1 - DEC_BATCH, d), F32)], axis=0)
    mods = [_ada(cond8, w_ada, b_ada)]

    eye4 = jnp.eye(4, dtype=F32)
    P = dict(
        invcnt_ctx=_pool_inv_counts(SEQ), invcnt_dec=_pool_inv_counts(DEC_SEQ),
        g_pre_mix=g_pre_mix,
        pool_wbd=jnp.einsum("lgcd,gh->lgchd", pool_w, eye4).reshape(DEPTH, 256, 256).astype(BF16),
        pool_scale=pool_scale,
        lam_p=jnp.stack([diff_lq1, diff_lk1, diff_lq2, diff_lk2], axis=1),
        subln=jnp.tile(diff_subln, (1, 4)),
        sink=win_sink.reshape(DEPTH * 4),
        conv_dw=conv_dw, conv_dw_b=conv_dw_b, conv_ln_g=conv_ln_g, conv_ln_b=conv_ln_b,
        conv_pw=conv_pw.astype(BF16),
        g_post_mix=g_post_mix, g_pre_ffn=g_pre_ffn,
        ffn_dw=ffn_dw, ffn_dw_b=ffn_dw_b, g_post_ffn=g_post_ffn,
    )
    mix_w = dict(w_in=w_in, w_out=w_out)
    ffn_w = dict(ffn_up=ffn_up, ffn_down=ffn_down)
    for name, w in mix_w.items():
        P[name] = [w[0].astype(BF16)]
    for name in ffn_w:
        P[name] = []
    rope_tabs = _rope_tables(DEC_SEQ, DIFF_QK) + _rope_tables(DEC_SEQ, WIN_HD)
    caches = (cache_diff_k.reshape(DEC_BATCH, DEPTH, PAST_LEN, 256).astype(BF16),
              cache_diff_v.reshape(DEC_BATCH, DEPTH, PAST_LEN, 256).astype(BF16),
              cache_win_k.reshape(DEC_BATCH, DEPTH, PAST_LEN, 128).astype(BF16),
              cache_win_v.reshape(DEC_BATCH, DEPTH, PAST_LEN, 128).astype(BF16))

    new_caches = ()
    for l in range(DEPTH):
        mod = mods[l]
        first = list(ffn_w.values()) if l == 0 else []
        xp1, hp2, *rest = _mixer_ctx(xp, mod, P, l, first, l, new_caches)
        new_caches, cast = tuple(rest[:4]), rest[4:]
        for name, w in zip(ffn_w, cast):
            P[name].append(w)
        xp = _ffn(hp2, xp1, mod, P, l, SEQ, 0, 1)

        ql, kvb = _inproj_dec(xs, mod, P["g_pre_mix"], P["w_in"][l], l, rope_tabs)
        big = {**mix_w, **ffn_w}
        more = l + 1 < DEPTH
        nxt = list(big.values()) if more else []
        xs1, hs2, *cast = _mixer_dec(ql, kvb, caches, xs, mod, P, l, nxt, l + 1,
                                     (cond8, w_ada, b_ada) if more else ())
        for name, w in zip(big, cast):
            P[name].append(w)
        mods += cast[len(nxt):]
        xs = _ffn(hs2, xs1, mod, P, l, DEC_SEQ, 1, DEC_BATCH)

    return (xp.reshape(BATCH, SEQ, d), xs.reshape(DEC_BATCH, DEC_SEQ, d),
            _cache_from_feature_major(new_caches[0], 4), _cache_from_feature_major(new_caches[1], 4),
            _cache_from_feature_major(new_caches[2], 2), _cache_from_feature_major(new_caches[3], 2))
```

```python
import functools
import math

import jax
import jax.numpy as jnp
import numpy as np
from jax import lax
from jax.experimental import pallas as pl
from jax.experimental.pallas import tpu as pltpu

F32 = jnp.float32
BF16 = jnp.bfloat16

D_MODEL = 1024
BATCH = 16
SEQ = 256
DEPTH = 2
DEC_BATCH = 2
DEC_SEQ = 2048
PAST_LEN = 256
GRID_W = 64
POOL_WINDOWS = (2, 4, 8, 16)
DIFF_QK = 32
WIN_HD = 64
WINDOW = 128
CONV_K = 31
IN_WIDTH = 2048
D_FF = 2816
ROPE_BASE = 10000.0
EPS = 1e-6
LOG2E = 1.4426950408889634

LANES = 128
SUBLANES = 8
HALO = 16
FFN_HALO = 16
FFN_CHUNK = 256
KV_WIDTH = 768
QL_WIDTH = 1280
VMEM_LIMIT = 56 * 1024 * 1024

TM_IN = 512
CTX_SEQS_PER_STEP = 2
TQ_DEC = 256
TM_FFN = 1024


def _rms(x, g):
    return x * lax.rsqrt(jnp.mean(x * x, axis=-1, keepdims=True) + EPS) * g


def _dot(a, b):
    return jnp.dot(a, b, preferred_element_type=F32)


def _dot_nt(a, b):
    return lax.dot_general(a, b, (((1,), (1,)), ((), ())), preferred_element_type=F32)


def _lane_range(shape, lo, hi):
    lane = lax.broadcasted_iota(jnp.int32, shape, 1)
    return (lane >= lo) & (lane < hi)


def _swap_middle_heads(t0, t1):
    lo = _lane_range(t0.shape, 0, 64)
    return (jnp.where(lo, t0, pltpu.roll(t1, 64, 1)),
            jnp.where(lo, pltpu.roll(t0, 64, 1), t1))


def _modulated_norm(x, g, mod, shift_col, scale_col):
    d = D_MODEL
    return (_rms(x, g) * (1.0 + mod[:, scale_col * d:(scale_col + 1) * d])
            + mod[:, shift_col * d:(shift_col + 1) * d])


def _layer_spec(shape, l):
    nz = len(shape)
    return pl.BlockSpec((None,) + tuple(shape), lambda *_: (l,) + (0,) * nz,
                        pipeline_mode=pl.Buffered(1))


def _whole_spec(shape):
    return pl.BlockSpec(tuple(shape), lambda *_: (0,) * len(shape), pipeline_mode=pl.Buffered(1))


def _rows_spec(width):
    return pl.BlockSpec((DEPTH, width), lambda *_: (0, 0), pipeline_mode=pl.Buffered(1))


def _mod_spec():
    return pl.BlockSpec((8, 6 * D_MODEL), lambda *_: (0, 0), pipeline_mode=pl.Buffered(1))


def _ada_slab(c_ref, w_ref, b_ref, layer):
    c = c_ref[...]
    s = (c * jax.nn.sigmoid(c)).astype(BF16)
    return _dot(s, w_ref[...].astype(BF16)) + b_ref[layer:layer + 1, :]


def _ada_kernel(c_ref, w_ref, b_ref, o_ref):
    o_ref[...] = _ada_slab(c_ref, w_ref, b_ref, 0)


def _ada_specs(layer, tn, step_of):
    ins = [pl.BlockSpec((8, D_MODEL), lambda *g: (0, 0)),
           pl.BlockSpec((None, D_MODEL, tn), lambda *g: (layer, 0, step_of(*g))),
           pl.BlockSpec((DEPTH, tn), lambda *g: (0, step_of(*g)))]
    return ins, pl.BlockSpec((8, tn), lambda *g: (0, step_of(*g)))


def _ada(cond8, w_ada, b_ada):
    tn = 1024
    in_specs, out_spec = _ada_specs(0, tn, lambda j: j)
    return pl.pallas_call(
        _ada_kernel,
        out_shape=jax.ShapeDtypeStruct((8, 6 * D_MODEL), F32),
        grid=(6 * D_MODEL // tn,),
        in_specs=in_specs,
        out_specs=out_spec,
        compiler_params=pltpu.CompilerParams(
            dimension_semantics=("arbitrary",), vmem_limit_bytes=VMEM_LIMIT),
        name="ada_mod",
    )(cond8, w_ada, b_ada)


def _rope_tile(v, cos, sin_signed, half):
    is_a = (lax.broadcasted_iota(jnp.int32, v.shape, 1) % (2 * half)) < half
    partner = jnp.where(is_a, pltpu.roll(v, LANES - half, 1), pltpu.roll(v, half, 1))
    return v * cos + partner * sin_signed


def _inproj_dec_kernel(x_ref, mod_ref, g_ref, w_ref, cd_ref, sd_ref, cw_ref, sw_ref, ql_ref, kv_ref,
                       *, l, tiles_per_b):
    mod = mod_ref[pl.ds(1 + pl.program_id(0) // tiles_per_b, 1), :]
    h = _modulated_norm(x_ref[...], g_ref[l:l + 1, :], mod, 0, 1)
    proj = _dot(h.astype(BF16), w_ref[...])
    tiles = [proj[:, i * LANES:(i + 1) * LANES] for i in range(IN_WIDTH // LANES)]
    cd, sd, cw, sw = cd_ref[...], sd_ref[...], cw_ref[...], sw_ref[...]
    for i in (2, 3, 4, 5):
        tiles[i] = _rope_tile(tiles[i], cd, sd, DIFF_QK // 2)
    for i in (8, 9, 10):
        tiles[i] = _rope_tile(tiles[i], cw, sw, WIN_HD // 2)
    tiles[8], tiles[9] = _swap_middle_heads(tiles[8], tiles[9])
    for j, i in enumerate((12, 13, 14, 15, 0, 1, 2, 3, 8, 9)):
        ql_ref[:, j * LANES:(j + 1) * LANES] = tiles[i]
    for j, i in enumerate((4, 5, 6, 7, 10, 11)):
        kv_ref[:, j * LANES:(j + 1) * LANES] = tiles[i].astype(BF16)


def _inproj_dec(x, mod, g, w_in_b, l, rope_tabs):
    rows = x.shape[0]
    tm = TM_IN
    tiles_per_b = DEC_SEQ // tm
    in_specs = [
        pl.BlockSpec((tm, D_MODEL), lambda i: (i, 0)),
        _mod_spec(),
        _rows_spec(D_MODEL),
        _whole_spec((D_MODEL, IN_WIDTH)),
    ] + [pl.BlockSpec((tm, LANES), lambda i: (i % tiles_per_b, 0)) for _ in rope_tabs]
    return pl.pallas_call(
        functools.partial(_inproj_dec_kernel, l=l, tiles_per_b=tiles_per_b),
        out_shape=[jax.ShapeDtypeStruct((rows, QL_WIDTH), F32),
                   jax.ShapeDtypeStruct((rows, KV_WIDTH), BF16)],
        grid=(rows // tm,),
        in_specs=in_specs,
        out_specs=[pl.BlockSpec((tm, QL_WIDTH), lambda i: (i, 0)),
                   pl.BlockSpec((tm, KV_WIDTH), lambda i: (i, 0))],
        compiler_params=pltpu.CompilerParams(
            dimension_semantics=("arbitrary",), vmem_limit_bytes=VMEM_LIMIT),
        name="inproj_dec",
    )(x, mod, g, w_in_b, *rope_tabs)


def _pool(xpad, u, invcnt, tq):
    rows = xpad.shape[0]
    p = xpad
    sums = []
    for step, w in zip((1, 2, 4, 8), POOL_WINDOWS):
        p = p + pltpu.roll(p, step, 0)
        off = HALO + w // 2 - 1
        r = off % SUBLANES
        sh = p if r == 0 else pltpu.roll(p, rows - r, 0)
        sums.append(sh[off - r:off - r + tq])
    shape = u.shape
    s = jnp.where(_lane_range(shape, 0, 64), sums[0],
                  jnp.where(_lane_range(shape, 64, 128), sums[1],
                            jnp.where(_lane_range(shape, 128, 192), sums[2], sums[3])))
    return s * invcnt - u


def _conv_module(apad, gpad, dw_ref, dwb, lng, lnb, tq):
    u = apad * jax.nn.sigmoid(gpad)
    rows = u.shape[0]
    acc = jnp.broadcast_to(dwb, (tq, u.shape[1]))
    for r in range(SUBLANES):
        ur = u if r == 0 else pltpu.roll(u, rows - r, 0)
        for a8 in range(4):
            k = SUBLANES * a8 + r - 1
            if 0 <= k < CONV_K:
                acc = acc + ur[SUBLANES * a8:SUBLANES * a8 + tq] * dw_ref[k:k + 1, :]
    mu = jnp.mean(acc, axis=-1, keepdims=True)
    xc = acc - mu
    var = jnp.mean(xc * xc, axis=-1, keepdims=True)
    y = xc * lax.rsqrt(var + EPS) * lng + lnb
    return y * jax.nn.sigmoid(y)


def _ones_outside(v, lo, hi):
    m = jnp.where(_lane_range((1, v.shape[1]), lo, hi), 1.0, 0.0).astype(BF16)
    return v * m + (1.0 - m)


def _diff_attn(dq, ks, vps, lam, gain, tq, group):
    dqs = dq * (DIFF_QK ** -0.5 * LOG2E)
    ytiles = [jnp.zeros((tq, LANES), F32), jnp.zeros((tq, LANES), F32)]
    for h0 in range(0, 4, group):
        qs = []
        for h in range(h0, h0 + group):
            for lo in (h * 64, h * 64 + DIFF_QK):
                qs.append(jnp.where(_lane_range(dqs.shape, lo, lo + DIFF_QK), dqs, 0.0))
        qg = jnp.concatenate(qs, axis=0).astype(BF16)
        ss = [_dot_nt(qg, k) for k in ks]
        mx = functools.reduce(jnp.maximum, [jnp.max(s, axis=-1, keepdims=True) for s in ss])
        es = [jnp.exp2(s - mx).astype(BF16) for s in ss]
        for h in range(h0, h0 + group):
            r0 = (h - h0) * 2 * tq
            o = None
            for e, vp in zip(es, vps(h)):
                t = _dot(e[r0:r0 + 2 * tq], vp)
                o = t if o is None else o + t
            tile = h // 2
            ot = o[:, tile * LANES:(tile + 1) * LANES]
            hlo = (h % 2) * 64
            den = jnp.max(jnp.where(_lane_range(ot.shape, 64 - hlo, 128 - hlo), ot, 0.0), axis=-1, keepdims=True)
            r = ot / den
            od = jnp.where(_lane_range((tq, LANES), hlo, hlo + 64), r[:tq] - lam * r[tq:], 0.0)
            ms = jnp.sum(od * od, axis=-1, keepdims=True) * (1.0 / 64.0)
            ytiles[tile] = ytiles[tile] + od * lax.rsqrt(ms + EPS)
    return jnp.concatenate(ytiles, axis=1) * gain


def _win_attn(wq, ks, masks, vs, sink_ref, sink_base, tq, group):
    wqs = wq * (WIN_HD ** -0.5 * LOG2E)
    t0, t1 = wqs[:, :LANES], wqs[:, LANES:]
    y0 = jnp.zeros((tq, LANES), F32)
    y1 = jnp.zeros((tq, LANES), F32)
    rows = 2 * tq * group
    row = lax.broadcasted_iota(jnp.int32, (rows, 1), 0)
    for j0 in range(0, 2, group):
        qs = []
        sk = jnp.zeros((rows, 1), F32)
        for j in range(j0, j0 + group):
            lm = _lane_range(t0.shape, j * 64, j * 64 + 64)
            qs += [jnp.where(lm, t0, 0.0), jnp.where(lm, t1, 0.0)]
            for g in range(2):
                r0 = ((j - j0) * 2 + g) * tq
                sk = jnp.where((row >= r0) & (row < r0 + tq), sink_ref[sink_base + 2 * j + g], sk)
        sk = sk * LOG2E
        qg = jnp.concatenate(qs, axis=0).astype(BF16)
        ss = []
        for k, m in zip(ks, masks):
            s = _dot_nt(qg, k)
            if m is not None:
                s = jnp.where(m, s, -1e30)
            ss.append(s)
        mx = functools.reduce(jnp.maximum, [jnp.max(s, axis=-1, keepdims=True) for s in ss])
        mx = jnp.maximum(mx, sk)
        es = [jnp.exp2(s - mx).astype(BF16) for s in ss]
        esink = jnp.exp2(sk - mx)
        for j in range(j0, j0 + group):
            r0 = (j - j0) * 2 * tq
            o = None
            for e, v in zip(es, vs):
                t = _dot(e[r0:r0 + 2 * tq], _ones_outside(v, j * 64, j * 64 + 64))
                o = t if o is None else o + t
            other = _lane_range(o.shape, 64 - j * 64, 128 - j * 64)
            den = jnp.max(jnp.where(other, o, 0.0), axis=-1, keepdims=True)
            r = o / (den + esink[r0:r0 + 2 * tq])
            lm = _lane_range((tq, LANES), j * 64, j * 64 + 64)
            y0 = y0 + jnp.where(lm, r[:tq], 0.0)
            y1 = y1 + jnp.where(lm, r[tq:], 0.0)
    return _swap_middle_heads(y0, y1)


def _mixer_kernel(*refs, dec, l, tq, n_cast=0, n_alias=0, next_mod=False):
    if dec:
        (ql_ref, conv_p_ref, conv_n_ref, pool_p_ref, pool_n_ref, kv_ref,
         cdk_ref, cdv_ref, cwk_ref, cwv_ref, x_ref, mod_ref, *rest) = refs
    else:
        (x_ref, mod_ref, gin_ref, win_ref, *rest) = refs
    (invcnt_ref, wbd_ref, pscale_ref, lamp_ref, subln_ref, sink_ref,
     dw_ref, dwb_ref, lng_ref, lnb_ref, pw_ref, wout_ref, gpost_ref, gpre_ref, *tail) = rest
    n_ada = 3 if next_mod else 0
    cast_in, ada_in = tail[:n_cast], tail[n_cast:n_cast + n_ada]
    outs = tail[n_cast + n_ada + n_alias:]
    d = D_MODEL
    mod = mod_ref[pl.ds(1 + pl.program_id(0), 1), :] if dec else mod_ref[0:1, :]
    x = x_ref[...]
    row = lambda ref: ref[l:l + 1, :]

    lam_init = 0.8 - 0.6 * math.exp(-0.3 * l)
    lp = lamp_ref[...]
    lam = (jnp.exp(jnp.sum(lp[0:1] * lp[1:2], axis=-1, keepdims=True))
           - jnp.exp(jnp.sum(lp[2:3] * lp[3:4], axis=-1, keepdims=True)) + lam_init)
    gain = row(subln_ref) * (1.0 - lam_init)

    zpad = jnp.zeros((HALO, 256), F32)
    pieces = []
    if dec:
        x1_ref, h2_ref, *cast_out, vp_ref = outs
        if next_mod:
            *cast_out, modn_ref = cast_out
            modn_ref[...] = _ada_slab(*ada_in, l + 1)
        ca, cg = ql_ref[:, 0:256], ql_ref[:, 256:512]
        u_pool, dq, wq = ql_ref[:, 512:768], ql_ref[:, 768:1024], ql_ref[:, 1024:1280]
        q = pl.program_id(1)
        nq = pl.num_programs(1)
        pv = (q > 0).astype(F32)
        nv = (q < nq - 1).astype(F32)
        pool_pad = jnp.concatenate([pool_p_ref[...] * pv, u_pool, pool_n_ref[...] * nv], axis=0)
        cp = conv_p_ref[...] * pv
        cn = conv_n_ref[...] * nv
        apad = jnp.concatenate([cp[:, :256], ca, cn[:, :256]], axis=0)
        gpad = jnp.concatenate([cp[:, 256:], cg, cn[:, 256:]], axis=0)

        @pl.when(q == 0)
        def _():
            for h in range(4):
                vp_ref[h, 0:PAST_LEN, :] = _ones_outside(cdv_ref[...], h * 64, h * 64 + 64)
                vp_ref[h, PAST_LEN:, :] = _ones_outside(kv_ref[:, 256:512], h * 64, h * 64 + 64)

        dks = [cdk_ref[...], kv_ref[:, 0:256]]
        vps = lambda h: [vp_ref[h, 0:PAST_LEN, :], vp_ref[h, PAST_LEN:, :]]
        band = tq + 2 * WINDOW
        start = pl.multiple_of(jnp.clip(q * tq - WINDOW, 0, DEC_SEQ - band), WINDOW)
        wkb = kv_ref[pl.ds(start, band), 512:640]
        wvb = kv_ref[pl.ds(start, band), 640:768]
        qpos = q * tq + (lax.broadcasted_iota(jnp.int32, (2 * tq, band), 0) & (tq - 1))
        kpos = start + lax.broadcasted_iota(jnp.int32, (2 * tq, band), 1)
        wmask = jnp.abs(qpos - kpos) <= WINDOW
        yw = _win_attn(wq, [wkb, cwk_ref[...]], [wmask, None], [wvb, cwv_ref[...]], sink_ref, 4 * l, tq, group=1)
        pieces.append((_pool(pool_pad, u_pool, invcnt_ref[...], tq),
                       _diff_attn(dq, dks, vps, lam, gain, tq, group=4), yw[0], yw[1],
                       _conv_module(apad, gpad, dw_ref, row(dwb_ref), row(lng_ref), row(lnb_ref), tq)))
    else:
        x1_ref, h2_ref, kd_ref, vd_ref, kw_ref, vw_ref, *cast_out = outs
        h = _modulated_norm(x, row(gin_ref), mod, 0, 1)
        proj = _dot(h.astype(BF16), win_ref[...])
        for s in range(x.shape[0] // tq):
            ps = proj[s * tq:(s + 1) * tq]
            u_pool, dq = ps[:, 0:256], ps[:, 256:512]
            dk, dv = ps[:, 512:768], ps[:, 768:1024]
            wk, wv = ps[:, 1280:1408], ps[:, 1408:1536]
            ca, cg = ps[:, 1536:1792], ps[:, 1792:2048]
            wq = jnp.concatenate(_swap_middle_heads(ps[:, 1024:1152], ps[:, 1152:1280]), axis=1)
            for ref, new in ((kd_ref, dk.T), (vd_ref, dv.T), (kw_ref, wk.T), (vw_ref, wv.T)):
                if n_alias:
                    ref[s] = new
                else:
                    for ll in range(DEPTH):
                        ref[s, ll] = new if ll == l else jnp.zeros_like(new)
            pool_pad = jnp.concatenate([zpad, u_pool, zpad], axis=0)
            apad = jnp.concatenate([zpad, ca, zpad], axis=0)
            gpad = jnp.concatenate([zpad, cg, zpad], axis=0)
            dvb = dv.astype(BF16)
            vps = lambda h, dvb=dvb: [_ones_outside(dvb, h * 64, h * 64 + 64)]
            yw = _win_attn(wq, [wk.astype(BF16)], [None], [wv.astype(BF16)], sink_ref, 4 * l, tq, group=2)
            pieces.append((_pool(pool_pad, u_pool, invcnt_ref[...], tq),
                           _diff_attn(dq, [dk.astype(BF16)], vps, lam, gain, tq, group=4), yw[0], yw[1],
                           _conv_module(apad, gpad, dw_ref, row(dwb_ref), row(lng_ref), row(lnb_ref), tq)))

    for src, dst in zip(cast_in, cast_out):
        dst[...] = src[...].astype(BF16)

    cat = lambda k: jnp.concatenate([t[k] for t in pieces], axis=0).astype(BF16)
    y_pool = _dot(cat(0), wbd_ref[...]) * row(pscale_ref)
    y_conv = _dot(cat(4), pw_ref[...])
    o = (_dot(y_pool.astype(BF16), wout_ref[0:256, :])
         + _dot(cat(1), wout_ref[256:512, :])
         + _dot(cat(2), wout_ref[512:640, :])
         + _dot(cat(3), wout_ref[640:768, :])
         + _dot(y_conv.astype(BF16), wout_ref[768:1024, :]))
    x1 = x + mod[:, 2 * d:3 * d] * _rms(o, row(gpost_ref))
    x1_ref[...] = x1
    h2_ref[...] = _modulated_norm(x1, row(gpre_ref), mod, 3, 4).astype(BF16)


def _mixer_param_specs(l):
    return [
        _layer_spec((256, 256), l),
        _rows_spec(256),
        _layer_spec((4, DIFF_QK), l),
        _rows_spec(256),
        pl.BlockSpec(memory_space=pltpu.SMEM),
        _layer_spec((CONV_K, 256), l),
        _rows_spec(256),
        _rows_spec(256),
        _rows_spec(256),
        _layer_spec((256, 256), l),
        _whole_spec((D_MODEL, D_MODEL)),
        _rows_spec(D_MODEL),
        _rows_spec(D_MODEL),
    ]


def _mixer_param_args(P, l):
    return [P["pool_wbd"], P["pool_scale"], P["lam_p"], P["subln"], P["sink"],
            P["conv_dw"], P["conv_dw_b"], P["conv_ln_g"], P["conv_ln_b"], P["conv_pw"], P["w_out"][l],
            P["g_post_mix"], P["g_pre_ffn"]]


def _cast_specs(to_cast, cast_layer, steps, step_of):
    ins = [pl.BlockSpec((None, w.shape[1] // steps, w.shape[2]),
                        lambda *g: (cast_layer, step_of(*g), 0)) for w in to_cast]
    outs = [pl.BlockSpec((w.shape[1] // steps, w.shape[2]), lambda *g: (step_of(*g), 0)) for w in to_cast]
    shapes = [jax.ShapeDtypeStruct(w.shape[1:], BF16) for w in to_cast]
    return ins, outs, shapes


def _mixer_ctx(x, mod, P, l, to_cast=(), cast_layer=0, caches=()):
    rows = x.shape[0]
    tq = SEQ
    ns = CTX_SEQS_PER_STEP
    row_map = lambda i: (i, 0)
    seq_map = lambda i: (i, l, 0, 0)
    cast_in, cast_out, cast_shapes = _cast_specs(to_cast, cast_layer, rows // (ns * tq), lambda i: i)
    in_specs = [
        pl.BlockSpec((ns * tq, D_MODEL), row_map),
        _mod_spec(),
        _rows_spec(D_MODEL),
        _whole_spec((D_MODEL, IN_WIDTH)),
        pl.BlockSpec((tq, 256), lambda i: (0, 0)),
    ] + _mixer_param_specs(l)
    args = [x, mod, P["g_pre_mix"], P["w_in"][l], P["invcnt_ctx"]] + _mixer_param_args(P, l) + list(to_cast)
    first_cache_out = 2
    aliases = {len(args) + k: first_cache_out + k for k in range(len(caches))}
    return pl.pallas_call(
        functools.partial(_mixer_kernel, dec=False, l=l, tq=tq, n_cast=len(to_cast), n_alias=len(caches)),
        out_shape=[jax.ShapeDtypeStruct((rows, D_MODEL), F32),
                   jax.ShapeDtypeStruct((rows, D_MODEL), BF16),
                   jax.ShapeDtypeStruct((BATCH, DEPTH, 256, tq), F32),
                   jax.ShapeDtypeStruct((BATCH, DEPTH, 256, tq), F32),
                   jax.ShapeDtypeStruct((BATCH, DEPTH, 128, tq), F32),
                   jax.ShapeDtypeStruct((BATCH, DEPTH, 128, tq), F32)] + cast_shapes,
        grid=(rows // (ns * tq),),
        in_specs=in_specs + cast_in + [pl.BlockSpec(memory_space=pl.ANY)] * len(caches),
        out_specs=[pl.BlockSpec((ns * tq, D_MODEL), row_map), pl.BlockSpec((ns * tq, D_MODEL), row_map)]
        + [pl.BlockSpec((ns, None, f, tq), seq_map) if caches
           else pl.BlockSpec((ns, DEPTH, f, tq), lambda i: (i, 0, 0, 0)) for f in (256, 256, 128, 128)]
        + cast_out,
        input_output_aliases=aliases,
        compiler_params=pltpu.CompilerParams(
            dimension_semantics=("arbitrary",), vmem_limit_bytes=VMEM_LIMIT),
        name="mixer_ctx",
    )(*args, *caches)


def _mixer_dec(ql, kvb, caches, x, mod, P, l, to_cast=(), cast_layer=0, ada_next=()):
    rows = x.shape[0]
    tq = TQ_DEC
    nq = DEC_SEQ // tq
    cast_in_specs, cast_out_specs, cast_shapes = _cast_specs(
        to_cast, cast_layer, DEC_BATCH * nq, lambda b, q: b * nq + q)
    hb = tq // HALO
    nhb = rows // HALO
    row_map = lambda b, q: (b * nq + q, 0)

    def prev_map(col):
        return lambda b, q: (jnp.maximum((b * nq + q) * hb - 1, 0), col)

    def next_map(col):
        return lambda b, q: (jnp.minimum((b * nq + q + 1) * hb, nhb - 1), col)

    cache_map = lambda b, q: (b, l, 0, 0)
    in_specs = [
        pl.BlockSpec((tq, QL_WIDTH), row_map),
        pl.BlockSpec((HALO, 512), prev_map(0)),
        pl.BlockSpec((HALO, 512), next_map(0)),
        pl.BlockSpec((HALO, 256), prev_map(2)),
        pl.BlockSpec((HALO, 256), next_map(2)),
        pl.BlockSpec((DEC_SEQ, KV_WIDTH), lambda b, q: (b, 0)),
        pl.BlockSpec((None, None, PAST_LEN, 256), cache_map),
        pl.BlockSpec((None, None, PAST_LEN, 256), cache_map),
        pl.BlockSpec((None, None, PAST_LEN, 128), cache_map),
        pl.BlockSpec((None, None, PAST_LEN, 128), cache_map),
        pl.BlockSpec((tq, D_MODEL), row_map),
        _mod_spec(),
        pl.BlockSpec((tq, 256), lambda b, q: (q, 0)),
    ] + _mixer_param_specs(l)
    args = [ql, ql, ql, ql, ql, kvb, *caches, x, mod, P["invcnt_dec"]] + _mixer_param_args(P, l)
    ada_in_specs, ada_out_specs, ada_shapes = [], [], []
    if ada_next:
        steps = DEC_BATCH * nq
        ada_in_specs, ada_out = _ada_specs(l + 1, 6 * D_MODEL // steps, lambda b, q: b * nq + q)
        ada_out_specs, ada_shapes = [ada_out], [jax.ShapeDtypeStruct((8, 6 * D_MODEL), F32)]
    return pl.pallas_call(
        functools.partial(_mixer_kernel, dec=True, l=l, tq=tq, n_cast=len(to_cast), next_mod=bool(ada_next)),
        out_shape=[jax.ShapeDtypeStruct((rows, D_MODEL), F32),
                   jax.ShapeDtypeStruct((rows, D_MODEL), BF16)]
        + cast_shapes + ada_shapes,
        grid=(DEC_BATCH, nq),
        in_specs=in_specs + cast_in_specs + ada_in_specs,
        out_specs=[pl.BlockSpec((tq, D_MODEL), row_map), pl.BlockSpec((tq, D_MODEL), row_map)]
        + cast_out_specs + ada_out_specs,
        scratch_shapes=[pltpu.VMEM((4, PAST_LEN + DEC_SEQ, 256), BF16)],
        compiler_params=pltpu.CompilerParams(
            dimension_semantics=("arbitrary", "arbitrary"), vmem_limit_bytes=VMEM_LIMIT),
        name="mixer_dec",
    )(*args, *to_cast, *ada_next)


def _zero_rows(a, rows):
    sub = lax.broadcasted_iota(jnp.int32, (SUBLANES, 1), 0)
    pieces, done = [], 0
    for r in sorted(rows):
        g = r - r % SUBLANES
        pieces += [a[done:g], jnp.where(sub == r % SUBLANES, 0.0, a[g:g + SUBLANES])]
        done = g + SUBLANES
    pieces.append(a[done:])
    return jnp.concatenate([p for p in pieces if p.shape[0]], axis=0)


def _ffn_kernel(*refs, tm, seq_len, l, mod_row0, tiles_per_b, halo):
    i = pl.program_id(0)
    if halo:
        h_ref, hp_ref, hn_ref, *rest = refs
        t = i % tiles_per_b
        hp = jnp.where(t != 0, hp_ref[...], jnp.zeros_like(hp_ref))
        hn = jnp.where(t != tiles_per_b - 1, hn_ref[...], jnp.zeros_like(hn_ref))
        hx = jnp.concatenate([hp, h_ref[...], hn], axis=0)
    else:
        h_ref, *rest = refs
        hx = h_ref[...]
        starts = list(range(0, tm, seq_len))
        ends = [s + seq_len - 1 for s in starts]
    x_ref, mod_ref, wup_ref, dw_ref, dwb_ref, wd_hbm, gpost_ref, o_ref, act_ref, wd_ref, wd_sem = rest
    wd_copy = pltpu.make_async_copy(wd_hbm, wd_ref, wd_sem)

    @pl.when(i == 0)
    def _():
        wd_copy.start()

    for j in range(D_FF // FFN_CHUNK):
        ys = []
        for part in range(2):
            c0 = part * D_FF + j * FFN_CHUNK
            u = _dot(hx, wup_ref[:, c0:c0 + FFN_CHUNK])
            w = dw_ref[:, c0:c0 + FFN_CHUNK]
            if halo:
                below, mid, above = (u[FFN_HALO - 1:FFN_HALO - 1 + tm], u[FFN_HALO:FFN_HALO + tm],
                                     u[FFN_HALO + 1:FFN_HALO + 1 + tm])
            else:
                below, mid, above = (_zero_rows(pltpu.roll(u, 1, 0), starts), u,
                                     _zero_rows(pltpu.roll(u, tm - 1, 0), ends))
            ys.append(below * w[0:1] + mid * w[1:2] + above * w[2:3]
                      + dwb_ref[l:l + 1, c0:c0 + FFN_CHUNK])
        gate, up = ys
        act_ref[:, j * FFN_CHUNK:(j + 1) * FFN_CHUNK] = (gate * jax.nn.sigmoid(gate) * up).astype(BF16)
    @pl.when(i == 0)
    def _():
        wd_copy.wait()

    o = _dot(act_ref[...], wd_ref[...])
    g2 = mod_ref[pl.ds(mod_row0 + i // tiles_per_b, 1), 5 * D_MODEL:6 * D_MODEL]
    o_ref[...] = x_ref[...] + g2 * _rms(o, gpost_ref[l:l + 1, :])


def _ffn(h2, x1, mod, P, l, seq_len, mod_row0, nb):
    rows = x1.shape[0]
    tm = TM_FFN
    tiles_per_b = rows // nb // tm
    hb = tm // FFN_HALO
    nhb = rows // FFN_HALO
    halo = tm % seq_len != 0
    h_specs = [pl.BlockSpec((tm, D_MODEL), lambda i: (i, 0))]
    if halo:
        h_specs += [pl.BlockSpec((FFN_HALO, D_MODEL), lambda i: (jnp.maximum(i * hb - 1, 0), 0)),
                    pl.BlockSpec((FFN_HALO, D_MODEL), lambda i: (jnp.minimum((i + 1) * hb, nhb - 1), 0))]
    return pl.pallas_call(
        functools.partial(_ffn_kernel, tm=tm, seq_len=seq_len, l=l, mod_row0=mod_row0,
                          tiles_per_b=tiles_per_b, halo=halo),
        out_shape=jax.ShapeDtypeStruct((rows, D_MODEL), F32),
        grid=(rows // tm,),
        in_specs=h_specs + [
            pl.BlockSpec((tm, D_MODEL), lambda i: (i, 0)),
            _mod_spec(),
            _whole_spec((D_MODEL, 2 * D_FF)),
            _layer_spec((3, 2 * D_FF), l),
            _rows_spec(2 * D_FF),
            pl.BlockSpec(memory_space=pl.ANY),
            _rows_spec(D_MODEL),
        ],
        out_specs=pl.BlockSpec((tm, D_MODEL), lambda i: (i, 0)),
        scratch_shapes=[pltpu.VMEM((tm, D_FF), BF16), pltpu.VMEM((D_FF, D_MODEL), BF16),
                        pltpu.SemaphoreType.DMA(())],
        compiler_params=pltpu.CompilerParams(
            dimension_semantics=("arbitrary",), vmem_limit_bytes=VMEM_LIMIT),
        name="ffn",
    )(*([h2] * len(h_specs)), x1, mod, P["ffn_up"][l], P["ffn_dw"], P["ffn_dw_b"], P["ffn_down"][l],
      P["g_post_ffn"])


def _pool_inv_counts(seq_len):
    inv = np.zeros((seq_len, 256), np.float32)
    pos = np.arange(seq_len)
    for g, w in enumerate(POOL_WINDOWS):
        cnt = np.clip(pos - w // 2 + w, 0, seq_len) - np.clip(pos - w // 2, 0, seq_len)
        inv[:, g * 64:(g + 1) * 64] = (1.0 / cnt.astype(np.float64))[:, None].astype(np.float32)
    return jnp.asarray(inv, F32)


def _rope_tables(seq_len, dim):
    rows = seq_len // GRID_W
    r = np.repeat(np.arange(rows), GRID_W).astype(np.float32)
    col = np.tile(np.arange(GRID_W), rows).astype(np.float32)
    n = dim // 4
    inv = (ROPE_BASE ** (-np.arange(n) / n)).astype(np.float32)
    ang = np.concatenate([r[:, None] * inv[None], col[:, None] * inv[None]], axis=-1)
    cos, sin = np.cos(ang), np.sin(ang)
    reps = LANES // dim
    cos_t = np.tile(np.concatenate([cos, cos], axis=-1), (1, reps))
    sin_t = np.tile(np.concatenate([-sin, sin], axis=-1), (1, reps))
    return jnp.asarray(cos_t, F32), jnp.asarray(sin_t, F32)


def _cache_from_feature_major(a, heads):
    return jnp.transpose(a.reshape(BATCH, DEPTH, heads, 64, SEQ), (0, 1, 4, 2, 3))


def kernel(x_prompt, x_sample, c, cache_diff_k, cache_diff_v, cache_win_k, cache_win_v, c_ctx, w_ada, b_ada, g_pre_mix, g_post_mix, g_pre_ffn, g_post_ffn, w_in, w_out, pool_w, pool_scale, diff_lq1, diff_lk1, diff_lq2, diff_lk2, diff_subln, win_sink, conv_dw, conv_dw_b, conv_ln_g, conv_ln_b, conv_pw, ffn_up, ffn_dw, ffn_dw_b, ffn_down):
    d = D_MODEL
    xp = x_prompt.reshape(BATCH * SEQ, d)
    xs = x_sample.reshape(DEC_BATCH * DEC_SEQ, d)

    cond8 = jnp.concatenate([c_ctx[None, :], c, jnp.zeros((8 - 1 - DEC_BATCH, d), F32)], axis=0)
    mods = [_ada(cond8, w_ada, b_ada)]

    eye4 = jnp.eye(4, dtype=F32)
    P = dict(
        invcnt_ctx=_pool_inv_counts(SEQ), invcnt_dec=_pool_inv_counts(DEC_SEQ),
        g_pre_mix=g_pre_mix,
        pool_wbd=jnp.einsum("lgcd,gh->lgchd", pool_w, eye4).reshape(DEPTH, 256, 256).astype(BF16),
        pool_scale=pool_scale,
        lam_p=jnp.stack([diff_lq1, diff_lk1, diff_lq2, diff_lk2], axis=1),
        subln=jnp.tile(diff_subln, (1, 4)),
        sink=win_sink.reshape(DEPTH * 4),
        conv_dw=conv_dw, conv_dw_b=conv_dw_b, conv_ln_g=conv_ln_g, conv_ln_b=conv_ln_b,
        conv_pw=conv_pw.astype(BF16),
        g_post_mix=g_post_mix, g_pre_ffn=g_pre_ffn,
        ffn_dw=ffn_dw, ffn_dw_b=ffn_dw_b, g_post_ffn=g_post_ffn,
    )
    mix_w = dict(w_in=w_in, w_out=w_out)
    ffn_w = dict(ffn_up=ffn_up, ffn_down=ffn_down)
    for name, w in mix_w.items():
        P[name] = [w[0].astype(BF16)]
    for name in ffn_w:
        P[name] = []
    rope_tabs = _rope_tables(DEC_SEQ, DIFF_QK) + _rope_tables(DEC_SEQ, WIN_HD)
    caches = (cache_diff_k.reshape(DEC_BATCH, DEPTH, PAST_LEN, 256).astype(BF16),
              cache_diff_v.reshape(DEC_BATCH, DEPTH, PAST_LEN, 256).astype(BF16),
              cache_win_k.reshape(DEC_BATCH, DEPTH, PAST_LEN, 128).astype(BF16),
              cache_win_v.reshape(DEC_BATCH, DEPTH, PAST_LEN, 128).astype(BF16))

    new_caches = ()
    for l in range(DEPTH):
        mod = mods[l]
        first = list(ffn_w.values()) if l == 0 else []
        xp1, hp2, *rest = _mixer_ctx(xp, mod, P, l, first, l, new_caches)
        new_caches, cast = tuple(rest[:4]), rest[4:]
        for name, w in zip(ffn_w, cast):
            P[name].append(w)
        xp = _ffn(hp2, xp1, mod, P, l, SEQ, 0, 1)

        ql, kvb = _inproj_dec(xs, mod, P["g_pre_mix"], P["w_in"][l], l, rope_tabs)
        big = {**mix_w, **ffn_w}
        more = l + 1 < DEPTH
        nxt = list(big.values()) if more else []
        xs1, hs2, *cast = _mixer_dec(ql, kvb, caches, xs, mod, P, l, nxt, l + 1,
                                     (cond8, w_ada, b_ada) if more else ())
        for name, w in zip(big, cast):
            P[name].append(w)
        mods += cast[len(nxt):]
        xs = _ffn(hs2, xs1, mod, P, l, DEC_SEQ, 1, DEC_BATCH)

    return (xp.reshape(BATCH, SEQ, d), xs.reshape(DEC_BATCH, DEC_SEQ, d),
            _cache_from_feature_major(new_caches[0], 4), _cache_from_feature_major(new_caches[1], 4),
            _cache_from_feature_major(new_caches[2], 2), _cache_from_feature_major(new_caches[3], 2))
```
